```python
import jax, jax.numpy as jnp
from jax import lax
import numpy as np

D_MODEL = 1024
BATCH = 8
SEQ = 2048
DEPTH = 1
DEC_BATCH = 128
DEC_SEQ = 1
PAST_LEN = 16384
PAGE_SIZE = 128

D_MIX = 2 * D_MODEL
SSD_WIDTH = D_MIX // 2
SC_WIDTH = D_MIX - SSD_WIDTH
SSD_HEAD_DIM = 64
SSD_HEADS = SSD_WIDTH // SSD_HEAD_DIM
SSD_GROUPS = 2
D_STATE = 128
SSD_CONV_W = 4
SSD_CONV_DIM = SSD_WIDTH + 2 * SSD_GROUPS * D_STATE
SSD_CHUNK = 256
SC_CONV_W = 3
SC_GROUP_DIM = 64
SC_GROUPS = SC_WIDTH // SC_GROUP_DIM
D_FF = 2816
D_IN_PROJ = SSD_WIDTH + SSD_CONV_DIM + SSD_HEADS + 3 * SC_WIDTH
NORM_EPS = 1e-6

kernel_name = 'hymba_ssd_shortconv_macaron_step'


def rmsnorm(x, w):
    xf = x.astype(jnp.float32)
    y = xf * lax.rsqrt(jnp.mean(xf * xf, axis=-1, keepdims=True) + NORM_EPS)
    return (y * w.astype(jnp.float32)).astype(x.dtype)


def swiglu(x, w_gate, w_up, w_down):
    return (jax.nn.silu(x @ w_gate) * (x @ w_up)) @ w_down


def causal_dwconv(u, buf, w):
    K = w.shape[0]
    T = u.shape[1]
    up = jnp.concatenate([buf.astype(u.dtype), u], axis=1)
    y = up[:, 0:T] * w[0]
    for k in range(1, K):
        y = y + up[:, k:k + T] * w[k]
    return y, up[:, T:]


def ssd_scan(x, dt, A, B, C, s0):
    b, T, h, p = x.shape
    g, n = B.shape[2], B.shape[3]
    r = h // g
    l = min(SSD_CHUNK, T)
    pad = (-T) % l
    Tp = T + pad
    c = Tp // l
    f32 = jnp.float32
    padt = lambda a: jnp.pad(a.astype(f32), [(0, 0), (0, pad)] + [(0, 0)] * (a.ndim - 2))
    xc = padt(x).reshape(b, c, l, g, r, p)
    dtc = padt(dt).reshape(b, c, l, g, r)
    Bc = padt(B).reshape(b, c, l, g, n)
    Cc = padt(C).reshape(b, c, l, g, n)
    a = dtc * A.astype(f32).reshape(g, r)
    acum = jnp.cumsum(a, axis=2)
    xdt = xc * dtc[..., None]
    seg = acum[:, :, :, None] - acum[:, :, None, :]
    mask = jnp.tril(jnp.ones((l, l), dtype=bool))[None, None, :, :, None, None]
    Lm = jnp.exp(jnp.where(mask, seg, -jnp.inf))
    CB = jnp.einsum('bclgn,bcsgn->bclsg', Cc, Bc)
    y_diag = jnp.einsum('bclsg,bclsgr,bcsgrp->bclgrp', CB, Lm, xdt)
    decay_end = jnp.exp(acum[:, :, -1:] - acum)
    states = jnp.einsum('bclgn,bclgr,bclgrp->bcgrpn', Bc, decay_end, xdt)
    chunk_decay = jnp.exp(acum[:, :, -1])

    def step(S, inp):
        st, dc = inp
        return S * dc[..., None, None] + st, S

    S_final, S_enter = lax.scan(
        step, s0.astype(f32).reshape(b, g, r, p, n),
        (jnp.moveaxis(states, 1, 0), jnp.moveaxis(chunk_decay, 1, 0)))
    S_enter = jnp.moveaxis(S_enter, 0, 1)
    y_off = jnp.einsum('bclgn,bcgrpn,bclgr->bclgrp', Cc, S_enter, jnp.exp(acum))
    y = (y_diag + y_off).reshape(b, Tp, h, p)[:, :T]
    return y, S_final.reshape(b, h, p, n)


def token_mix(h, ssm0, conv0, sconv0, w_in, ssd_conv_w, ssd_conv_b, dt_bias, a_log,
              d_skip, ssd_norm_w, sconv_w, w_out):
    b, T, _ = h.shape
    proj = h @ w_in
    o1 = SSD_WIDTH
    o2 = o1 + SSD_CONV_DIM
    o3 = o2 + SSD_HEADS
    o4 = o3 + SC_WIDTH
    o5 = o4 + SC_WIDTH
    z = proj[..., :o1]
    xbc = proj[..., o1:o2]
    dt_raw = proj[..., o2:o3]
    sc_b = proj[..., o3:o4]
    sc_c = proj[..., o4:o5]
    sc_h = proj[..., o5:]
    xbc_c, conv_new = causal_dwconv(xbc, conv0, ssd_conv_w)
    xbc_c = jax.nn.silu(xbc_c + ssd_conv_b)
    xs = xbc_c[..., :SSD_WIDTH].reshape(b, T, SSD_HEADS, SSD_HEAD_DIM)
    Bm = xbc_c[..., SSD_WIDTH:SSD_WIDTH + SSD_GROUPS * D_STATE].reshape(b, T, SSD_GROUPS, D_STATE)
    Cm = xbc_c[..., SSD_WIDTH + SSD_GROUPS * D_STATE:].reshape(b, T, SSD_GROUPS, D_STATE)
    dt = jax.nn.softplus(dt_raw.astype(jnp.float32) + dt_bias.astype(jnp.float32))
    A = -jnp.exp(a_log.astype(jnp.float32))
    y_ssd, S_new = ssd_scan(xs, dt, A, Bm, Cm, ssm0)
    y_ssd = (y_ssd + d_skip.astype(jnp.float32)[:, None] * xs.astype(jnp.float32)).astype(h.dtype)
    y_ssd = y_ssd.reshape(b, T, SSD_WIDTH) * jax.nn.silu(z)
    y_ssd = rmsnorm(y_ssd.reshape(b, T, SSD_GROUPS, SSD_WIDTH // SSD_GROUPS),
                    ssd_norm_w.reshape(SSD_GROUPS, SSD_WIDTH // SSD_GROUPS)).reshape(b, T, SSD_WIDTH)
    u = sc_c * sc_h
    v, sconv_new = causal_dwconv(u, sconv0, sconv_w)
    y_sc = sc_b * v
    out = jnp.concatenate([y_ssd, y_sc], axis=-1) @ w_out
    return out, S_new.astype(ssm0.dtype), conv_new.astype(conv0.dtype), sconv_new.astype(sconv0.dtype)


def layer(x, ssm0, conv0, sconv0, norm_ffn1_w, ffn1_w_gate, ffn1_w_up, ffn1_w_down,
          norm_mix_w, w_in, ssd_conv_w, ssd_conv_b, dt_bias, a_log, d_skip, ssd_norm_w,
          sconv_w, w_out, norm_ffn2_w, ffn2_w_gate, ffn2_w_up, ffn2_w_down):
    x = x + 0.5 * swiglu(rmsnorm(x, norm_ffn1_w), ffn1_w_gate, ffn1_w_up, ffn1_w_down)
    mix, S, cb, sb = token_mix(rmsnorm(x, norm_mix_w), ssm0, conv0, sconv0, w_in, ssd_conv_w,
                               ssd_conv_b, dt_bias, a_log, d_skip, ssd_norm_w, sconv_w, w_out)
    x = x + mix
    x = x + 0.5 * swiglu(rmsnorm(x, norm_ffn2_w), ffn2_w_gate, ffn2_w_up, ffn2_w_down)
    return x, S, cb, sb


def setup_inputs(seed: int = 0) -> dict:
    key = jax.random.key(seed)
    ks = jax.random.split(key, 32)
    f32 = jnp.float32
    nrm = lambda k, shape, scale: jax.random.normal(k, shape, f32) * scale
    Ld = DEPTH
    dt0 = jnp.exp(jax.random.uniform(ks[10], (Ld, SSD_HEADS), f32, np.log(1e-3), np.log(1e-1)))
    return {
        'x_prompt': nrm(ks[0], (BATCH, SEQ, D_MODEL), 1.0),
        'x_sample': nrm(ks[1], (DEC_BATCH, DEC_SEQ, D_MODEL), 1.0),
        'state_ssm': nrm(ks[2], (Ld, DEC_BATCH, SSD_HEADS, SSD_HEAD_DIM, D_STATE), 0.1),
        'state_ssd_conv': nrm(ks[3], (Ld, DEC_BATCH, SSD_CONV_W - 1, SSD_CONV_DIM), 1.0),
        'state_sconv': nrm(ks[4], (Ld, DEC_BATCH, SC_CONV_W - 1, SC_WIDTH), 1.0),
        'norm_ffn1_w': 1.0 + nrm(ks[5], (Ld, D_MODEL), 0.02),
        'ffn1_w_gate': nrm(ks[6], (Ld, D_MODEL, D_FF), D_MODEL ** -0.5),
        'ffn1_w_up': nrm(ks[7], (Ld, D_MODEL, D_FF), D_MODEL ** -0.5),
        'ffn1_w_down': nrm(ks[8], (Ld, D_FF, D_MODEL), D_FF ** -0.5),
        'norm_mix_w': 1.0 + nrm(ks[9], (Ld, D_MODEL), 0.02),
        'w_in': nrm(ks[11], (Ld, D_MODEL, D_IN_PROJ), D_MODEL ** -0.5),
        'ssd_conv_w': nrm(ks[12], (Ld, SSD_CONV_W, SSD_CONV_DIM), SSD_CONV_W ** -0.5),
        'ssd_conv_b': nrm(ks[13], (Ld, SSD_CONV_DIM), 0.02),
        'dt_bias': dt0 + jnp.log(-jnp.expm1(-dt0)),
        'a_log': jnp.log(jax.random.uniform(ks[14], (Ld, SSD_HEADS), f32, 1.0, 16.0)),
        'd_skip': 1.0 + nrm(ks[15], (Ld, SSD_HEADS), 0.1),
        'ssd_norm_w': 1.0 + nrm(ks[16], (Ld, SSD_WIDTH), 0.02),
        'sconv_w': nrm(ks[17], (Ld, SC_CONV_W, SC_WIDTH), SC_CONV_W ** -0.5),
        'w_out': nrm(ks[18], (Ld, D_MIX, D_MODEL), D_MIX ** -0.5),
        'norm_ffn2_w': 1.0 + nrm(ks[19], (Ld, D_MODEL), 0.02),
        'ffn2_w_gate': nrm(ks[20], (Ld, D_MODEL, D_FF), D_MODEL ** -0.5),
        'ffn2_w_up': nrm(ks[21], (Ld, D_MODEL, D_FF), D_MODEL ** -0.5),
        'ffn2_w_down': nrm(ks[22], (Ld, D_FF, D_MODEL), D_FF ** -0.5),
        'final_norm_w': 1.0 + nrm(ks[23], (D_MODEL,), 0.02),
    }


def reference(x_prompt, x_sample, state_ssm, state_ssd_conv, state_sconv,
              norm_ffn1_w, ffn1_w_gate, ffn1_w_up, ffn1_w_down, norm_mix_w, w_in,
              ssd_conv_w, ssd_conv_b, dt_bias, a_log, d_skip, ssd_norm_w, sconv_w, w_out,
              norm_ffn2_w, ffn2_w_gate, ffn2_w_up, ffn2_w_down, final_norm_w):
    xp = x_prompt
    xs = x_sample
    bp = x_prompt.shape[0]
    sp_list, cp_list, scp_list = [], [], []
    ss_list, cs_list, scs_list = [], [], []
    for i in range(DEPTH):
        w = (norm_ffn1_w[i], ffn1_w_gate[i], ffn1_w_up[i], ffn1_w_down[i], norm_mix_w[i],
             w_in[i], ssd_conv_w[i], ssd_conv_b[i], dt_bias[i], a_log[i], d_skip[i],
             ssd_norm_w[i], sconv_w[i], w_out[i], norm_ffn2_w[i], ffn2_w_gate[i],
             ffn2_w_up[i], ffn2_w_down[i])
        ssm0 = jnp.zeros((bp,) + state_ssm.shape[2:], state_ssm.dtype)
        conv0 = jnp.zeros((bp,) + state_ssd_conv.shape[2:], state_ssd_conv.dtype)
        sconv0 = jnp.zeros((bp,) + state_sconv.shape[2:], state_sconv.dtype)
        xp, S_p, c_p, sc_p = layer(xp, ssm0, conv0, sconv0, *w)
        xs, S_s, c_s, sc_s = layer(xs, state_ssm[i], state_ssd_conv[i], state_sconv[i], *w)
        sp_list.append(S_p); cp_list.append(c_p); scp_list.append(sc_p)
        ss_list.append(S_s); cs_list.append(c_s); scs_list.append(sc_s)
    y_prompt = rmsnorm(xp, final_norm_w)
    y_sample = rmsnorm(xs, final_norm_w)
    return (y_prompt, y_sample,
            jnp.stack(sp_list), jnp.stack(cp_list), jnp.stack(scp_list),
            jnp.stack(ss_list), jnp.stack(cs_list), jnp.stack(scs_list))
```

```python
import functools

import jax
import jax.numpy as jnp
from jax import lax
from jax.experimental import pallas as pl
from jax.experimental.pallas import tpu as pltpu

F32 = jnp.float32
BF16 = jnp.bfloat16

D_MODEL = 1024
SSD_WIDTH = 1024
SC_WIDTH = 1024
HEAD_DIM = 64
N_HEADS = SSD_WIDTH // HEAD_DIM
N_GROUPS = 2
HEADS_PER_GROUP = N_HEADS // N_GROUPS
GROUP_WIDTH = SSD_WIDTH // N_GROUPS
D_STATE = 128
SSD_CONV_W = 4
SSD_CONV_DIM = SSD_WIDTH + 2 * N_GROUPS * D_STATE
SC_CONV_W = 3
CHUNK = 256
NORM_EPS = 1e-6

LANES = 128
SUBLANES = 8

OFF_Z = 0
OFF_XBC = OFF_Z + SSD_WIDTH
OFF_SCB = OFF_XBC + SSD_CONV_DIM
OFF_SCC = OFF_SCB + SC_WIDTH
OFF_SCH = OFF_SCC + SC_WIDTH
OFF_DT = OFF_SCH + SC_WIDTH
D_PROJ = OFF_DT + LANES

VMEM_LIMIT_BYTES = 56 * 1024 * 1024


def _rmsnorm(x, w):
    ms = jnp.mean(x * x, axis=-1, keepdims=True)
    return x * lax.rsqrt(ms + NORM_EPS) * w


def _silu(x):
    return x * (1.0 / (1.0 + jnp.exp(-x)))


def _softplus(x):
    return jnp.maximum(x, 0.0) + jnp.log1p(jnp.exp(-jnp.abs(x)))


def _dot(a, b):
    return jnp.dot(a, b, preferred_element_type=F32)


def _dot_nt(a, b):
    return lax.dot_general(a, b, (((1,), (1,)), ((), ())), preferred_element_type=F32)


def _split3(x):
    hi = x.astype(BF16)
    r1 = x - hi.astype(F32)
    mid = r1.astype(BF16)
    lo = (r1 - mid.astype(F32)).astype(BF16)
    return hi, mid, lo


def _expand_heads(v, rows):
    lane = lax.broadcasted_iota(jnp.int32, (rows, LANES), 1)
    first_half = lane < HEAD_DIM
    blocks = []
    for q in range(N_HEADS // 2):
        c0 = jnp.broadcast_to(v[:, 2 * q:2 * q + 1], (rows, LANES))
        c1 = jnp.broadcast_to(v[:, 2 * q + 1:2 * q + 2], (rows, LANES))
        blocks.append(jnp.where(first_half, c0, c1))
    return jnp.concatenate(blocks, axis=1)


def _group_rmsnorm(y, w):
    outs = []
    for g in range(N_GROUPS):
        sl = slice(g * GROUP_WIDTH, (g + 1) * GROUP_WIDTH)
        outs.append(_rmsnorm(y[:, sl], w[:, sl]))
    return jnp.concatenate(outs, axis=1)


def _ffn_kernel(*refs, final_norm):
    if final_norm:
        x_ref, nw_ref, wg_ref, wu_ref, wd_ref, fnw_ref, o_ref = refs
    else:
        x_ref, nw_ref, wg_ref, wu_ref, wd_ref, o_ref = refs
    x = x_ref[...]
    xn = _rmsnorm(x, nw_ref[...]).astype(BF16)
    g = _dot(xn, wg_ref[...])
    u = _dot(xn, wu_ref[...])
    hmid = (_silu(g) * u).astype(BF16)
    y = x + 0.5 * _dot(hmid, wd_ref[...])
    if final_norm:
        y = _rmsnorm(y, fnw_ref[...])
    o_ref[...] = y


def _resident(shape):
    return pl.BlockSpec(shape, lambda *_: (0,) * len(shape), pipeline_mode=pl.Buffered(1))


def _ffn(x, nw, wg, wu, wd, fnw=None, *, tile_m):
    m, d = x.shape
    dff = wg.shape[1]
    tile_m = min(tile_m, m)
    assert m % tile_m == 0
    final_norm = fnw is not None
    in_specs = [pl.BlockSpec((tile_m, d), lambda i: (i, 0)),
                _resident((1, d)), _resident((d, dff)), _resident((d, dff)), _resident((dff, d))]
    args = [x, nw, wg, wu, wd]
    if final_norm:
        in_specs.append(_resident((1, d)))
        args.append(fnw)
    return pl.pallas_call(
        functools.partial(_ffn_kernel, final_norm=final_norm),
        out_shape=jax.ShapeDtypeStruct((m, d), F32),
        grid=(m // tile_m,),
        in_specs=in_specs,
        out_specs=pl.BlockSpec((tile_m, d), lambda i: (i, 0)),
        compiler_params=pltpu.CompilerParams(
            dimension_semantics=("arbitrary",), vmem_limit_bytes=VMEM_LIMIT_BYTES),
        name="ffn_final" if final_norm else "ffn",
    )(*args)


def _mixer_kernel(x_ref, nw_ref, win_ref, cw_ref, cb_ref, dtb_ref, alog_ref, dskip_ref, snw_ref,
                  scw_ref, wout_ref,
                  o_ref, sfin_ref, cfin_ref, scfin_ref,
                  st_scr, xpad_scr, upad_scr):
    L = CHUNK
    c = pl.program_id(1)

    @pl.when(c == 0)
    def _():
        st_scr[...] = jnp.zeros_like(st_scr)
        xpad_scr[0:SUBLANES, :] = jnp.zeros((SUBLANES, SSD_CONV_DIM), F32)
        upad_scr[0:SUBLANES, :] = jnp.zeros((SUBLANES, SC_WIDTH), F32)

    x = x_ref[0]
    h = _rmsnorm(x, nw_ref[...]).astype(BF16)
    proj = _dot(h, win_ref[...])
    z = proj[:, OFF_Z:OFF_XBC]
    xbc = proj[:, OFF_XBC:OFF_SCB]
    scb = proj[:, OFF_SCB:OFF_SCC]
    scc = proj[:, OFF_SCC:OFF_SCH]
    sch = proj[:, OFF_SCH:OFF_DT]
    dt_raw = proj[:, OFF_DT:D_PROJ]

    xpad_scr[SUBLANES:SUBLANES + L, :] = xbc
    cw = cw_ref[...]
    acc = xbc * cw[SSD_CONV_W - 1:SSD_CONV_W, :]
    for j in range(1, SSD_CONV_W):
        acc = acc + xpad_scr[SUBLANES - j:SUBLANES - j + L, :] * cw[SSD_CONV_W - 1 - j:SSD_CONV_W - j, :]
    hist = xpad_scr[L + SUBLANES - (SSD_CONV_W - 1):L + SUBLANES, :]
    xpad_scr[SUBLANES - (SSD_CONV_W - 1):SUBLANES, :] = hist
    cfin_ref[0] = hist
    xbc_c = _silu(acc + cb_ref[...])
    xs = xbc_c[:, 0:SSD_WIDTH]
    b_all = xbc_c[:, SSD_WIDTH:SSD_WIDTH + N_GROUPS * D_STATE]
    c_all = xbc_c[:, SSD_WIDTH + N_GROUPS * D_STATE:SSD_CONV_DIM]

    dt = _softplus(dt_raw + dtb_ref[...])
    a = dt * (-jnp.exp(alog_ref[...]))
    row = lax.broadcasted_iota(jnp.int32, (L, L), 0)
    col = lax.broadcasted_iota(jnp.int32, (L, L), 1)
    causal = row >= col
    tril = jnp.where(causal, 1.0, 0.0).astype(BF16)
    a_hi, a_mid, a_lo = _split3(a)
    acum = _dot(tril, a_hi) + _dot(tril, a_mid) + _dot(tril, a_lo)
    acum_t = acum.T

    acum_e = _expand_heads(acum, L)
    dt_e = _expand_heads(dt, L)
    exp_acum_e = jnp.exp(acum_e)
    decay_end_e = jnp.exp(acum_e[L - 1:L, :] - acum_e)
    xdt = xs * dt_e
    xdt_b = xdt.astype(BF16)
    xdecay_b = (xdt * decay_end_e).astype(BF16)

    lane = lax.broadcasted_iota(jnp.int32, (L, LANES), 1)
    first_half = lane < HEAD_DIM
    y_diag_blocks = []
    y_off_blocks = []
    for g in range(N_GROUPS):
        b_g = b_all[:, g * D_STATE:(g + 1) * D_STATE]
        c_g = c_all[:, g * D_STATE:(g + 1) * D_STATE]
        b_gb = b_g.astype(BF16)
        c_gb = c_g.astype(BF16)
        cb = _dot_nt(c_gb, b_gb)
        for q in range(g * HEADS_PER_GROUP // 2, (g + 1) * HEADS_PER_GROUP // 2):
            ms = []
            for hh in (2 * q, 2 * q + 1):
                seg = acum[:, hh:hh + 1] - acum_t[hh:hh + 1, :]
                decay = jnp.exp(jnp.where(causal, seg, -jnp.inf))
                ms.append((cb * decay).astype(BF16))
            m_cat = jnp.concatenate(ms, axis=1)
            x2 = xdt_b[:, q * LANES:(q + 1) * LANES]
            zero = jnp.zeros_like(x2)
            rhs = jnp.concatenate([jnp.where(first_half, x2, zero),
                                   jnp.where(first_half, zero, x2)], axis=0)
            y_diag_blocks.append(_dot(m_cat, rhs))
        gs = slice(g * GROUP_WIDTH, (g + 1) * GROUP_WIDTH)
        s_enter = st_scr[:, gs]
        y_off_blocks.append(_dot(c_gb, s_enter.astype(BF16)))
        new_states = _dot(b_g.T.astype(BF16), xdecay_b[:, gs])
        st_scr[:, gs] = s_enter * exp_acum_e[L - 1:L, gs] + new_states
    y_diag = jnp.concatenate(y_diag_blocks, axis=1)
    y_off = jnp.concatenate(y_off_blocks, axis=1) * exp_acum_e

    y = y_diag + y_off + dskip_ref[...] * xs
    y = y * _silu(z)
    y_ssd = _group_rmsnorm(y, snw_ref[...])

    u = scc * sch
    upad_scr[SUBLANES:SUBLANES + L, :] = u
    scw = scw_ref[...]
    v = u * scw[SC_CONV_W - 1:SC_CONV_W, :]
    for j in range(1, SC_CONV_W):
        v = v + upad_scr[SUBLANES - j:SUBLANES - j + L, :] * scw[SC_CONV_W - 1 - j:SC_CONV_W - j, :]
    uhist = upad_scr[L + SUBLANES - (SC_CONV_W - 1):L + SUBLANES, :]
    upad_scr[SUBLANES - (SC_CONV_W - 1):SUBLANES, :] = uhist
    scfin_ref[0] = uhist
    y_sc = scb * v

    mixed = jnp.concatenate([y_ssd, y_sc], axis=1).astype(BF16)
    o_ref[0] = x + _dot(mixed, wout_ref[...])
    sfin_ref[0] = st_scr[...].T


def _mixer_prompt(x, nw, win, cw, cb, dtb, alog, dskip, snw, scw, wout):
    nb, seq, d = x.shape
    assert seq % CHUNK == 0
    nc = seq // CHUNK
    bmap = lambda b, c: (b, 0, 0)
    out_shape = (
        jax.ShapeDtypeStruct((nb, seq, d), F32),
        jax.ShapeDtypeStruct((nb, SSD_WIDTH, D_STATE), F32),
        jax.ShapeDtypeStruct((nb, SSD_CONV_W - 1, SSD_CONV_DIM), F32),
        jax.ShapeDtypeStruct((nb, SC_CONV_W - 1, SC_WIDTH), F32),
    )
    return pl.pallas_call(
        _mixer_kernel,
        out_shape=out_shape,
        grid=(nb, nc),
        in_specs=[pl.BlockSpec((1, CHUNK, d), lambda b, c: (b, c, 0)),
                  _resident((1, d)), _resident((d, D_PROJ)),
                  _resident((SSD_CONV_W, SSD_CONV_DIM)), _resident((1, SSD_CONV_DIM)),
                  _resident((1, LANES)), _resident((1, LANES)), _resident((1, SSD_WIDTH)),
                  _resident((1, SSD_WIDTH)), _resident((SC_CONV_W, SC_WIDTH)),
                  _resident((SSD_WIDTH + SC_WIDTH, d))],
        out_specs=(pl.BlockSpec((1, CHUNK, d), lambda b, c: (b, c, 0)),
                   pl.BlockSpec((1, SSD_WIDTH, D_STATE), bmap),
                   pl.BlockSpec((1, SSD_CONV_W - 1, SSD_CONV_DIM), bmap),
                   pl.BlockSpec((1, SC_CONV_W - 1, SC_WIDTH), bmap)),
        scratch_shapes=[pltpu.VMEM((D_STATE, SSD_WIDTH), F32),
                        pltpu.VMEM((CHUNK + SUBLANES, SSD_CONV_DIM), F32),
                        pltpu.VMEM((CHUNK + SUBLANES, SC_WIDTH), F32)],
        compiler_params=pltpu.CompilerParams(
            dimension_semantics=("arbitrary", "arbitrary"), vmem_limit_bytes=VMEM_LIMIT_BYTES),
        name="mixer_prompt",
    )(x, nw, win, cw, cb, dtb, alog, dskip, snw, scw, wout)


def _sample_pre_kernel(x_ref, nw_ref, win_ref, cw_ref, cb_ref, dtb_ref, alog_ref, scw_ref,
                       cst_ref, scst_ref,
                       z_ref, xs_ref, b_ref, c_ref, xdt_t_ref, da_t_ref, ysc_ref, cnew_ref, scnew_ref):
    nb = x_ref.shape[0]
    x = x_ref[...]
    h = _rmsnorm(x, nw_ref[...]).astype(BF16)
    proj = _dot(h, win_ref[...])
    xbc = proj[:, OFF_XBC:OFF_SCB]
    scb = proj[:, OFF_SCB:OFF_SCC]
    scc = proj[:, OFF_SCC:OFF_SCH]
    sch = proj[:, OFF_SCH:OFF_DT]
    dt_raw = proj[:, OFF_DT:D_PROJ]
    z_ref[...] = proj[:, OFF_Z:OFF_XBC]

    cw = cw_ref[...]
    acc = xbc * cw[SSD_CONV_W - 1:SSD_CONV_W, :]
    for k in range(SSD_CONV_W - 1):
        acc = acc + cst_ref[k] * cw[k:k + 1, :]
    for k in range(SSD_CONV_W - 2):
        cnew_ref[k] = cst_ref[k + 1]
    cnew_ref[SSD_CONV_W - 2] = xbc
    xbc_c = _silu(acc + cb_ref[...])
    xs = xbc_c[:, 0:SSD_WIDTH]
    xs_ref[...] = xs
    b_ref[...] = xbc_c[:, SSD_WIDTH:SSD_WIDTH + N_GROUPS * D_STATE]
    c_ref[...] = xbc_c[:, SSD_WIDTH + N_GROUPS * D_STATE:SSD_CONV_DIM]

    dt = _softplus(dt_raw + dtb_ref[...])
    decay = jnp.exp(dt * (-jnp.exp(alog_ref[...])))
    xdt_t_ref[...] = (xs * _expand_heads(dt, nb)).T
    da_t_ref[...] = _expand_heads(decay, nb).T

    u = scc * sch
    scw = scw_ref[...]
    v = u * scw[SC_CONV_W - 1:SC_CONV_W, :]
    for k in range(SC_CONV_W - 1):
        v = v + scst_ref[k] * scw[k:k + 1, :]
    for k in range(SC_CONV_W - 2):
        scnew_ref[k] = scst_ref[k + 1]
    scnew_ref[SC_CONV_W - 2] = u
    ysc_ref[...] = scb * v


def _sample_pre(x, nw, win, cw, cb, dtb, alog, scw, cst, scst):
    nb, d = x.shape
    f = lambda *s: jax.ShapeDtypeStruct(s, F32)
    out_shape = (f(nb, SSD_WIDTH), f(nb, SSD_WIDTH), f(nb, N_GROUPS * D_STATE), f(nb, N_GROUPS * D_STATE),
                 f(SSD_WIDTH, nb), f(SSD_WIDTH, nb), f(nb, SC_WIDTH),
                 f(SSD_CONV_W - 1, nb, SSD_CONV_DIM), f(SC_CONV_W - 1, nb, SC_WIDTH))
    return pl.pallas_call(
        _sample_pre_kernel,
        out_shape=out_shape,
        compiler_params=pltpu.CompilerParams(vmem_limit_bytes=VMEM_LIMIT_BYTES),
        name="sample_pre",
    )(x, nw, win, cw, cb, dtb, alog, scw, cst, scst)


STATE_BATCH_BLOCK = 8


def _sample_state_kernel(s0_ref, xdt_ref, da_ref, b_ref, c_ref, snew_ref, y_ref):
    bb = STATE_BATCH_BLOCK
    xdt_t = xdt_ref[0]
    da_t = da_ref[0]
    b_rows = b_ref[...]
    c_rows = c_ref[...]
    lane = lax.broadcasted_iota(jnp.int32, (SSD_WIDTH, bb), 1)
    y_t = jnp.zeros((SSD_WIDTH, bb), F32)
    for i in range(bb):
        b_mat = jnp.concatenate(
            [jnp.broadcast_to(b_rows[i:i + 1, g * D_STATE:(g + 1) * D_STATE], (GROUP_WIDTH, D_STATE))
             for g in range(N_GROUPS)], axis=0)
        c_mat = jnp.concatenate(
            [jnp.broadcast_to(c_rows[i:i + 1, g * D_STATE:(g + 1) * D_STATE], (GROUP_WIDTH, D_STATE))
             for g in range(N_GROUPS)], axis=0)
        s_new = s0_ref[i] * da_t[:, i:i + 1] + xdt_t[:, i:i + 1] * b_mat
        snew_ref[i] = s_new
        y_col = jnp.sum(s_new * c_mat, axis=-1, keepdims=True)
        y_t = jnp.where(lane == i, y_col, y_t)
    y_ref[0] = y_t


def _sample_state(s0, xdt_t, da_t, b_rows, c_rows):
    nb = s0.shape[0]
    bb = STATE_BATCH_BLOCK
    assert nb % bb == 0
    nblk = nb // bb
    to_blocks = lambda t: t.reshape(SSD_WIDTH, nblk, bb).transpose(1, 0, 2)
    snew, y_blk = pl.pallas_call(
        _sample_state_kernel,
        out_shape=(jax.ShapeDtypeStruct((nb, SSD_WIDTH, D_STATE), F32),
                   jax.ShapeDtypeStruct((nblk, SSD_WIDTH, bb), F32)),
        grid=(nblk,),
        in_specs=[pl.BlockSpec((bb, SSD_WIDTH, D_STATE), lambda i: (i, 0, 0)),
                  pl.BlockSpec((1, SSD_WIDTH, bb), lambda i: (i, 0, 0)),
                  pl.BlockSpec((1, SSD_WIDTH, bb), lambda i: (i, 0, 0)),
                  pl.BlockSpec((bb, N_GROUPS * D_STATE), lambda i: (i, 0)),
                  pl.BlockSpec((bb, N_GROUPS * D_STATE), lambda i: (i, 0))],
        out_specs=(pl.BlockSpec((bb, SSD_WIDTH, D_STATE), lambda i: (i, 0, 0)),
                   pl.BlockSpec((1, SSD_WIDTH, bb), lambda i: (i, 0, 0))),
        compiler_params=pltpu.CompilerParams(
            dimension_semantics=("arbitrary",), vmem_limit_bytes=VMEM_LIMIT_BYTES),
        name="sample_state",
    )(s0, to_blocks(xdt_t), to_blocks(da_t), b_rows, c_rows)
    y = y_blk.transpose(0, 2, 1).reshape(nb, SSD_WIDTH)
    return snew, y


def _sample_post_kernel(x_ref, yraw_ref, xs_ref, z_ref, ysc_ref, dskip_ref, snw_ref, wout_ref, o_ref):
    y = yraw_ref[...] + dskip_ref[...] * xs_ref[...]
    y = y * _silu(z_ref[...])
    y_ssd = _group_rmsnorm(y, snw_ref[...])
    mixed = jnp.concatenate([y_ssd, ysc_ref[...]], axis=1).astype(BF16)
    o_ref[...] = x_ref[...] + _dot(mixed, wout_ref[...])


def _sample_post(x, yraw, xs, z, ysc, dskip, snw, wout):
    return pl.pallas_call(
        _sample_post_kernel,
        out_shape=jax.ShapeDtypeStruct(x.shape, F32),
        compiler_params=pltpu.CompilerParams(vmem_limit_bytes=VMEM_LIMIT_BYTES),
        name="sample_post",
    )(x, yraw, xs, z, ysc, dskip, snw, wout)


FFN_TILE_M = 512


def _layer_params(i, norm_ffn1_w, ffn1_w_gate, ffn1_w_up, ffn1_w_down, norm_mix_w, w_in,
                  ssd_conv_w, ssd_conv_b, dt_bias, a_log, d_skip, ssd_norm_w, sconv_w, w_out,
                  norm_ffn2_w, ffn2_w_gate, ffn2_w_up, ffn2_w_down):
    o_dt = SSD_WIDTH + SSD_CONV_DIM
    o_sc = o_dt + N_HEADS
    wi = w_in[i]
    win = jnp.concatenate(
        [wi[:, :o_dt], wi[:, o_sc:], wi[:, o_dt:o_sc],
         jnp.zeros((D_MODEL, LANES - N_HEADS), wi.dtype)], axis=1).astype(BF16)
    pad_heads = lambda v: jnp.pad(v, (0, LANES - N_HEADS)).reshape(1, LANES)
    row = lambda v: v.reshape(1, -1)
    return dict(
        ffn1=(row(norm_ffn1_w[i]), ffn1_w_gate[i].astype(BF16), ffn1_w_up[i].astype(BF16),
              ffn1_w_down[i].astype(BF16)),
        ffn2=(row(norm_ffn2_w[i]), ffn2_w_gate[i].astype(BF16), ffn2_w_up[i].astype(BF16),
              ffn2_w_down[i].astype(BF16)),
        nw=row(norm_mix_w[i]), win=win, cw=ssd_conv_w[i], cb=row(ssd_conv_b[i]),
        dtb=pad_heads(dt_bias[i]), alog=pad_heads(a_log[i]),
        dskip=row(jnp.repeat(d_skip[i], HEAD_DIM)), snw=row(ssd_norm_w[i]), scw=sconv_w[i],
        wout=w_out[i].astype(BF16))


def kernel(x_prompt, x_sample, state_ssm, state_ssd_conv, state_sconv, norm_ffn1_w, ffn1_w_gate, ffn1_w_up, ffn1_w_down, norm_mix_w, w_in, ssd_conv_w, ssd_conv_b, dt_bias, a_log, d_skip, ssd_norm_w, sconv_w, w_out, norm_ffn2_w, ffn2_w_gate, ffn2_w_up, ffn2_w_down, final_norm_w):
    depth = w_in.shape[0]
    bp, seq, d = x_prompt.shape
    bs, dec_seq, _ = x_sample.shape
    assert dec_seq == 1, "sample group is one token per sequence"
    fnw = final_norm_w.reshape(1, d)

    xp = x_prompt.reshape(bp * seq, d)
    xs = x_sample.reshape(bs, d)
    outs = [[] for _ in range(6)]
    for i in range(depth):
        p = _layer_params(i, norm_ffn1_w, ffn1_w_gate, ffn1_w_up, ffn1_w_down, norm_mix_w, w_in,
                          ssd_conv_w, ssd_conv_b, dt_bias, a_log, d_skip, ssd_norm_w, sconv_w,
                          w_out, norm_ffn2_w, ffn2_w_gate, ffn2_w_up, ffn2_w_down)
        last = i == depth - 1
        xp = _ffn(xp, *p["ffn1"], tile_m=FFN_TILE_M)
        xp3, s_p, c_p, sc_p = _mixer_prompt(
            xp.reshape(bp, seq, d), p["nw"], p["win"], p["cw"], p["cb"], p["dtb"], p["alog"],
            p["dskip"], p["snw"], p["scw"], p["wout"])
        xp = _ffn(xp3.reshape(bp * seq, d), *p["ffn2"], fnw if last else None, tile_m=FFN_TILE_M)
        xs = _ffn(xs, *p["ffn1"], tile_m=FFN_TILE_M)
        z, xs_conv, b_rows, c_rows, xdt_t, da_t, ysc, c_s, sc_s = _sample_pre(
            xs, p["nw"], p["win"], p["cw"], p["cb"], p["dtb"], p["alog"], p["scw"],
            state_ssd_conv[i].transpose(1, 0, 2), state_sconv[i].transpose(1, 0, 2))
        s_s, yraw = _sample_state(state_ssm[i].reshape(bs, SSD_WIDTH, D_STATE), xdt_t, da_t, b_rows, c_rows)
        xs = _sample_post(xs, yraw, xs_conv, z, ysc, p["dskip"], p["snw"], p["wout"])
        xs = _ffn(xs, *p["ffn2"], fnw if last else None, tile_m=FFN_TILE_M)
        for lst, v in zip(outs, (s_p.reshape(bp, N_HEADS, HEAD_DIM, D_STATE), c_p, sc_p,
                                 s_s.reshape(bs, N_HEADS, HEAD_DIM, D_STATE),
                                 c_s.transpose(1, 0, 2), sc_s.transpose(1, 0, 2))):
            lst.append(v)
    return (xp.reshape(bp, seq, d), xs.reshape(bs, dec_seq, d)) + tuple(jnp.stack(l) for l in outs)
```

```python
import functools

import jax
import jax.numpy as jnp
from jax import lax
from jax.experimental import pallas as pl
from jax.experimental.pallas import tpu as pltpu

F32 = jnp.float32
BF16 = jnp.bfloat16

D_MODEL = 1024
SSD_WIDTH = 1024
SC_WIDTH = 1024
HEAD_DIM = 64
N_HEADS = SSD_WIDTH // HEAD_DIM
N_GROUPS = 2
HEADS_PER_GROUP = N_HEADS // N_GROUPS
GROUP_WIDTH = SSD_WIDTH // N_GROUPS
D_STATE = 128
SSD_CONV_W = 4
SSD_CONV_DIM = SSD_WIDTH + 2 * N_GROUPS * D_STATE
SC_CONV_W = 3
CHUNK = 256
NORM_EPS = 1e-6

LANES = 128
SUBLANES = 8

D_ZX = SSD_WIDTH + SSD_CONV_DIM
D_SC = 3 * SC_WIDTH
PROJ_BLOCK = 512

VMEM_LIMIT_BYTES = 56 * 1024 * 1024


def _rmsnorm(x, w):
    ms = jnp.mean(x * x, axis=-1, keepdims=True)
    return x * lax.rsqrt(ms + NORM_EPS) * w


def _silu(x):
    return x * (0.5 * jnp.tanh(0.5 * x) + 0.5)


def _softplus(x):
    return jnp.maximum(x, 0.0) + jnp.log1p(jnp.exp(-jnp.abs(x)))


def _dot(a, b):
    return jnp.dot(a, b, preferred_element_type=F32)


def _dot_nt(a, b):
    return lax.dot_general(a, b, (((1,), (1,)), ((), ())), preferred_element_type=F32)


def _split3(x):
    hi = x.astype(BF16)
    r1 = x - hi.astype(F32)
    mid = r1.astype(BF16)
    lo = (r1 - mid.astype(F32)).astype(BF16)
    return hi, mid, lo


def _expand_heads(v, rows):
    lane = lax.broadcasted_iota(jnp.int32, (rows, LANES), 1)
    first_half = lane < HEAD_DIM
    blocks = []
    for q in range(N_HEADS // 2):
        c0 = jnp.broadcast_to(v[:, 2 * q:2 * q + 1], (rows, LANES))
        c1 = jnp.broadcast_to(v[:, 2 * q + 1:2 * q + 2], (rows, LANES))
        blocks.append(jnp.where(first_half, c0, c1))
    return jnp.concatenate(blocks, axis=1)


def _group_rmsnorm(y, w):
    outs = []
    for g in range(N_GROUPS):
        sl = slice(g * GROUP_WIDTH, (g + 1) * GROUP_WIDTH)
        outs.append(_rmsnorm(y[:, sl], w[:, sl]))
    return jnp.concatenate(outs, axis=1)


def _shift_rows(x, prev_tile, j):
    rolled = pltpu.roll(x, j, 0)
    row = lax.broadcasted_iota(jnp.int32, prev_tile.shape, 0)
    head = jnp.where(row < j, pltpu.roll(prev_tile, j, 0), rolled[0:SUBLANES])
    return jnp.concatenate([head, rolled[SUBLANES:]], axis=0)


def _causal_conv(u, prev_tile, w):
    k = w.shape[0]
    acc = u * w[k - 1:k, :]
    for j in range(1, k):
        acc = acc + _shift_rows(u, prev_tile, j) * w[k - 1 - j:k - j, :]
    return acc


def _ffn_kernel(*refs, final_norm):
    if final_norm:
        x_ref, nw_ref, wg_ref, wu_ref, wd_ref, fnw_ref, o_ref = refs
    else:
        x_ref, nw_ref, wg_ref, wu_ref, wd_ref, o_ref = refs
    x = x_ref[...]
    xn = _rmsnorm(x, nw_ref[...]).astype(BF16)
    g = _dot(xn, wg_ref[...])
    u = _dot(xn, wu_ref[...])
    hmid = (_silu(g) * u).astype(BF16)
    y = x + 0.5 * _dot(hmid, wd_ref[...])
    if final_norm:
        y = _rmsnorm(y, fnw_ref[...])
    o_ref[...] = y


def _resident(shape):
    return pl.BlockSpec(shape, lambda *_: (0,) * len(shape), pipeline_mode=pl.Buffered(1))


def _ffn(x, nw, wg, wu, wd, fnw=None, *, tile_m):
    m, d = x.shape
    dff = wg.shape[1]
    tile_m = min(tile_m, m)
    assert m % tile_m == 0
    final_norm = fnw is not None
    in_specs = [pl.BlockSpec((tile_m, d), lambda i: (i, 0)),
                _resident((1, d)), _resident((d, dff)), _resident((d, dff)), _resident((dff, d))]
    args = [x, nw, wg, wu, wd]
    if final_norm:
        in_specs.append(_resident((1, d)))
        args.append(fnw)
    return pl.pallas_call(
        functools.partial(_ffn_kernel, final_norm=final_norm),
        out_shape=jax.ShapeDtypeStruct((m, d), F32),
        grid=(m // tile_m,),
        in_specs=in_specs,
        out_specs=pl.BlockSpec((tile_m, d), lambda i: (i, 0)),
        compiler_params=pltpu.CompilerParams(
            dimension_semantics=("arbitrary",), vmem_limit_bytes=VMEM_LIMIT_BYTES),
        name="ffn_final" if final_norm else "ffn",
    )(*args)


def _mixer_kernel(x_ref, nw_ref, wzx_ref, wdt_ref, wsc_ref, cw_ref, cb_ref, dtb_ref, alog_ref,
                  dskip_ref, snw_ref, scw_ref, wout_ref,
                  o_ref, sfin_ref, cfin_ref, scfin_ref,
                  st_scr, hist_scr, uhist_scr):
    L = CHUNK
    c = pl.program_id(1)

    @pl.when(c == 0)
    def _():
        st_scr[...] = jnp.zeros_like(st_scr)
        hist_scr[...] = jnp.zeros_like(hist_scr)
        uhist_scr[...] = jnp.zeros_like(uhist_scr)

    x = x_ref[0]
    h = _rmsnorm(x, nw_ref[...]).astype(BF16)
    dt_raw = _dot(h, wdt_ref[...])

    xbc_c_blocks = []
    for b0 in range(0, SSD_CONV_DIM, PROJ_BLOCK):
        cols = slice(b0, b0 + PROJ_BLOCK)
        xbc = _dot(h, wzx_ref[:, SSD_WIDTH + b0:SSD_WIDTH + b0 + PROJ_BLOCK])
        acc = _causal_conv(xbc, hist_scr[:, cols], cw_ref[:, cols])
        hist_scr[:, cols] = xbc[L - SUBLANES:L, :]
        xbc_c_blocks.append(_silu(acc + cb_ref[:, cols]))
    cfin_ref[0] = hist_scr[SUBLANES - (SSD_CONV_W - 1):SUBLANES, :]
    xbc_c = jnp.concatenate(xbc_c_blocks, axis=1)
    xs = xbc_c[:, 0:SSD_WIDTH]
    b_all = xbc_c[:, SSD_WIDTH:SSD_WIDTH + N_GROUPS * D_STATE]
    c_all = xbc_c[:, SSD_WIDTH + N_GROUPS * D_STATE:SSD_CONV_DIM]

    y_sc_blocks = []
    for b0 in range(0, SC_WIDTH, PROJ_BLOCK):
        cols = slice(b0, b0 + PROJ_BLOCK)
        scc = _dot(h, wsc_ref[:, SC_WIDTH + b0:SC_WIDTH + b0 + PROJ_BLOCK])
        sch = _dot(h, wsc_ref[:, 2 * SC_WIDTH + b0:2 * SC_WIDTH + b0 + PROJ_BLOCK])
        u = scc * sch
        v = _causal_conv(u, uhist_scr[:, cols], scw_ref[:, cols])
        uhist_scr[:, cols] = u[L - SUBLANES:L, :]
        scb = _dot(h, wsc_ref[:, b0:b0 + PROJ_BLOCK])
        y_sc_blocks.append((scb * v).astype(BF16))
    scfin_ref[0] = uhist_scr[SUBLANES - (SC_CONV_W - 1):SUBLANES, :]
    y_sc = jnp.concatenate(y_sc_blocks, axis=1)

    z_gate = jnp.concatenate(
        [_silu(_dot(h, wzx_ref[:, b0:b0 + PROJ_BLOCK])) for b0 in range(0, SSD_WIDTH, PROJ_BLOCK)],
        axis=1)

    dt = _softplus(dt_raw + dtb_ref[...])
    a = dt * (-jnp.exp(alog_ref[...]))
    row = lax.broadcasted_iota(jnp.int32, (L, L), 0)
    col = lax.broadcasted_iota(jnp.int32, (L, L), 1)
    causal = row >= col
    tril = jnp.where(causal, 1.0, 0.0).astype(BF16)
    a_hi, a_mid, a_lo = _split3(a)
    acum = _dot(tril, a_hi) + _dot(tril, a_mid) + _dot(tril, a_lo)
    acum_t = acum.T

    acum_e = _expand_heads(acum, L)
    dt_e = _expand_heads(dt, L)
    exp_acum_e = jnp.exp(acum_e)
    decay_end_e = jnp.exp(acum_e[L - 1:L, :] - acum_e)
    xdt = xs * dt_e
    xdt_b = xdt.astype(BF16)
    xdecay_b = (xdt * decay_end_e).astype(BF16)

    lane = lax.broadcasted_iota(jnp.int32, (L, LANES), 1)
    first_half = lane < HEAD_DIM
    y_diag_blocks = []
    y_off_blocks = []
    for g in range(N_GROUPS):
        b_g = b_all[:, g * D_STATE:(g + 1) * D_STATE]
        c_g = c_all[:, g * D_STATE:(g + 1) * D_STATE]
        b_gb = b_g.astype(BF16)
        c_gb = c_g.astype(BF16)
        cb = _dot_nt(c_gb, b_gb)
        for q in range(g * HEADS_PER_GROUP // 2, (g + 1) * HEADS_PER_GROUP // 2):
            ms = []
            for hh in (2 * q, 2 * q + 1):
                seg = acum[:, hh:hh + 1] - acum_t[hh:hh + 1, :]
                decay = jnp.exp(jnp.where(causal, seg, -jnp.inf))
                ms.append((cb * decay).astype(BF16))
            m_cat = jnp.concatenate(ms, axis=1)
            x2 = xdt_b[:, q * LANES:(q + 1) * LANES]
            zero = jnp.zeros_like(x2)
            rhs = jnp.concatenate([jnp.where(first_half, x2, zero),
                                   jnp.where(first_half, zero, x2)], axis=0)
            y_diag_blocks.append(_dot(m_cat, rhs))
        gs = slice(g * GROUP_WIDTH, (g + 1) * GROUP_WIDTH)
        s_enter = st_scr[:, gs]
        y_off_blocks.append(_dot(c_gb, s_enter.astype(BF16)))
        new_states = _dot(b_g.T.astype(BF16), xdecay_b[:, gs])
        st_scr[:, gs] = s_enter * exp_acum_e[L - 1:L, gs] + new_states
    y_diag = jnp.concatenate(y_diag_blocks, axis=1)
    y_off = jnp.concatenate(y_off_blocks, axis=1) * exp_acum_e

    y = (y_diag + y_off + dskip_ref[...] * xs) * z_gate
    y_ssd = _group_rmsnorm(y, snw_ref[...]).astype(BF16)
    mixed = jnp.concatenate([y_ssd, y_sc], axis=1)
    o_ref[0] = x + _dot(mixed, wout_ref[...])

    @pl.when(c == pl.num_programs(1) - 1)
    def _():
        sfin_ref[0] = st_scr[...].T


def _mixer_prompt(x, nw, wzx, wdt, wsc, cw, cb, dtb, alog, dskip, snw, scw, wout):
    nb, seq, d = x.shape
    assert seq % CHUNK == 0
    nc = seq // CHUNK
    bmap = lambda b, c: (b, 0, 0)
    out_shape = (
        jax.ShapeDtypeStruct((nb, seq, d), F32),
        jax.ShapeDtypeStruct((nb, SSD_WIDTH, D_STATE), F32),
        jax.ShapeDtypeStruct((nb, SSD_CONV_W - 1, SSD_CONV_DIM), F32),
        jax.ShapeDtypeStruct((nb, SC_CONV_W - 1, SC_WIDTH), F32),
    )
    return pl.pallas_call(
        _mixer_kernel,
        out_shape=out_shape,
        grid=(nb, nc),
        in_specs=[pl.BlockSpec((1, CHUNK, d), lambda b, c: (b, c, 0)),
                  _resident((1, d)), _resident((d, D_ZX)), _resident((d, LANES)), _resident((d, D_SC)),
                  _resident((SSD_CONV_W, SSD_CONV_DIM)), _resident((1, SSD_CONV_DIM)),
                  _resident((1, LANES)), _resident((1, LANES)), _resident((1, SSD_WIDTH)),
                  _resident((1, SSD_WIDTH)), _resident((SC_CONV_W, SC_WIDTH)),
                  _resident((SSD_WIDTH + SC_WIDTH, d))],
        out_specs=(pl.BlockSpec((1, CHUNK, d), lambda b, c: (b, c, 0)),
                   pl.BlockSpec((1, SSD_WIDTH, D_STATE), bmap),
                   pl.BlockSpec((1, SSD_CONV_W - 1, SSD_CONV_DIM), bmap),
                   pl.BlockSpec((1, SC_CONV_W - 1, SC_WIDTH), bmap)),
        scratch_shapes=[pltpu.VMEM((D_STATE, SSD_WIDTH), F32),
                        pltpu.VMEM((SUBLANES, SSD_CONV_DIM), F32),
                        pltpu.VMEM((SUBLANES, SC_WIDTH), F32)],
        compiler_params=pltpu.CompilerParams(
            dimension_semantics=("arbitrary", "arbitrary"), vmem_limit_bytes=VMEM_LIMIT_BYTES),
        name="mixer_prompt",
    )(x, nw, wzx, wdt, wsc, cw, cb, dtb, alog, dskip, snw, scw, wout)


def _sample_pre_kernel(x_ref, nw_ref, wzx_ref, wdt_ref, wsc_ref, cw_ref, cb_ref, dtb_ref, alog_ref,
                       scw_ref, cst_ref, scst_ref,
                       z_ref, xs_ref, b_ref, c_ref, xdt_t_ref, da_t_ref, ysc_ref, cnew_ref, scnew_ref):
    nb = x_ref.shape[0]
    x = x_ref[...]
    h = _rmsnorm(x, nw_ref[...]).astype(BF16)
    dt_raw = _dot(h, wdt_ref[...])
    xbc = _dot(h, wzx_ref[:, SSD_WIDTH:D_ZX])
    z_ref[...] = _dot(h, wzx_ref[:, 0:SSD_WIDTH])
    scb = _dot(h, wsc_ref[:, 0:SC_WIDTH])
    scc = _dot(h, wsc_ref[:, SC_WIDTH:2 * SC_WIDTH])
    sch = _dot(h, wsc_ref[:, 2 * SC_WIDTH:D_SC])

    cw = cw_ref[...]
    acc = xbc * cw[SSD_CONV_W - 1:SSD_CONV_W, :]
    for k in range(SSD_CONV_W - 1):
        acc = acc + cst_ref[k] * cw[k:k + 1, :]
    for k in range(SSD_CONV_W - 2):
        cnew_ref[k] = cst_ref[k + 1]
    cnew_ref[SSD_CONV_W - 2] = xbc
    xbc_c = _silu(acc + cb_ref[...])
    xs = xbc_c[:, 0:SSD_WIDTH]
    xs_ref[...] = xs
    b_ref[...] = xbc_c[:, SSD_WIDTH:SSD_WIDTH + N_GROUPS * D_STATE]
    c_ref[...] = xbc_c[:, SSD_WIDTH + N_GROUPS * D_STATE:SSD_CONV_DIM]

    dt = _softplus(dt_raw + dtb_ref[...])
    decay = jnp.exp(dt * (-jnp.exp(alog_ref[...])))
    xdt_t_ref[...] = (xs * _expand_heads(dt, nb)).T
    da_t_ref[...] = _expand_heads(decay, nb).T

    u = scc * sch
    scw = scw_ref[...]
    v = u * scw[SC_CONV_W - 1:SC_CONV_W, :]
    for k in range(SC_CONV_W - 1):
        v = v + scst_ref[k] * scw[k:k + 1, :]
    for k in range(SC_CONV_W - 2):
        scnew_ref[k] = scst_ref[k + 1]
    scnew_ref[SC_CONV_W - 2] = u
    ysc_ref[...] = scb * v


def _sample_pre(x, nw, wzx, wdt, wsc, cw, cb, dtb, alog, scw, cst, scst):
    nb, d = x.shape
    f = lambda *s: jax.ShapeDtypeStruct(s, F32)
    out_shape = (f(nb, SSD_WIDTH), f(nb, SSD_WIDTH), f(nb, N_GROUPS * D_STATE), f(nb, N_GROUPS * D_STATE),
                 f(SSD_WIDTH, nb), f(SSD_WIDTH, nb), f(nb, SC_WIDTH),
                 f(SSD_CONV_W - 1, nb, SSD_CONV_DIM), f(SC_CONV_W - 1, nb, SC_WIDTH))
    return pl.pallas_call(
        _sample_pre_kernel,
        out_shape=out_shape,
        compiler_params=pltpu.CompilerParams(vmem_limit_bytes=VMEM_LIMIT_BYTES),
        name="sample_pre",
    )(x, nw, wzx, wdt, wsc, cw, cb, dtb, alog, scw, cst, scst)


STATE_BATCH_BLOCK = 8


def _sample_state_kernel(s0_ref, xdt_ref, da_ref, b_ref, c_ref, snew_ref, y_ref):
    bb = STATE_BATCH_BLOCK
    xdt_t = xdt_ref[0]
    da_t = da_ref[0]
    b_rows = b_ref[...]
    c_rows = c_ref[...]
    lane = lax.broadcasted_iota(jnp.int32, (SSD_WIDTH, bb), 1)
    y_t = jnp.zeros((SSD_WIDTH, bb), F32)
    for i in range(bb):
        b_mat = jnp.concatenate(
            [jnp.broadcast_to(b_rows[i:i + 1, g * D_STATE:(g + 1) * D_STATE], (GROUP_WIDTH, D_STATE))
             for g in range(N_GROUPS)], axis=0)
        c_mat = jnp.concatenate(
            [jnp.broadcast_to(c_rows[i:i + 1, g * D_STATE:(g + 1) * D_STATE], (GROUP_WIDTH, D_STATE))
             for g in range(N_GROUPS)], axis=0)
        s_new = s0_ref[i] * da_t[:, i:i + 1] + xdt_t[:, i:i + 1] * b_mat
        snew_ref[i] = s_new
        y_col = jnp.sum(s_new * c_mat, axis=-1, keepdims=True)
        y_t = jnp.where(lane == i, y_col, y_t)
    y_ref[0] = y_t


def _sample_state(s0, xdt_t, da_t, b_rows, c_rows):
    nb = s0.shape[0]
    bb = STATE_BATCH_BLOCK
    assert nb % bb == 0
    nblk = nb // bb
    to_blocks = lambda t: t.reshape(SSD_WIDTH, nblk, bb).transpose(1, 0, 2)
    snew, y_blk = pl.pallas_call(
        _sample_state_kernel,
        out_shape=(jax.ShapeDtypeStruct((nb, SSD_WIDTH, D_STATE), F32),
                   jax.ShapeDtypeStruct((nblk, SSD_WIDTH, bb), F32)),
        grid=(nblk,),
        in_specs=[pl.BlockSpec((bb, SSD_WIDTH, D_STATE), lambda i: (i, 0, 0)),
                  pl.BlockSpec((1, SSD_WIDTH, bb), lambda i: (i, 0, 0)),
                  pl.BlockSpec((1, SSD_WIDTH, bb), lambda i: (i, 0, 0)),
                  pl.BlockSpec((bb, N_GROUPS * D_STATE), lambda i: (i, 0)),
                  pl.BlockSpec((bb, N_GROUPS * D_STATE), lambda i: (i, 0))],
        out_specs=(pl.BlockSpec((bb, SSD_WIDTH, D_STATE), lambda i: (i, 0, 0)),
                   pl.BlockSpec((1, SSD_WIDTH, bb), lambda i: (i, 0, 0))),
        compiler_params=pltpu.CompilerParams(
            dimension_semantics=("arbitrary",), vmem_limit_bytes=VMEM_LIMIT_BYTES),
        name="sample_state",
    )(s0, to_blocks(xdt_t), to_blocks(da_t), b_rows, c_rows)
    y = y_blk.transpose(0, 2, 1).reshape(nb, SSD_WIDTH)
    return snew, y


def _sample_post_kernel(x_ref, yraw_ref, xs_ref, z_ref, ysc_ref, dskip_ref, snw_ref, wout_ref, o_ref):
    y = yraw_ref[...] + dskip_ref[...] * xs_ref[...]
    y = y * _silu(z_ref[...])
    y_ssd = _group_rmsnorm(y, snw_ref[...])
    mixed = jnp.concatenate([y_ssd, ysc_ref[...]], axis=1).astype(BF16)
    o_ref[...] = x_ref[...] + _dot(mixed, wout_ref[...])


def _sample_post(x, yraw, xs, z, ysc, dskip, snw, wout):
    return pl.pallas_call(
        _sample_post_kernel,
        out_shape=jax.ShapeDtypeStruct(x.shape, F32),
        compiler_params=pltpu.CompilerParams(vmem_limit_bytes=VMEM_LIMIT_BYTES),
        name="sample_post",
    )(x, yraw, xs, z, ysc, dskip, snw, wout)


FFN_TILE_M = 512


def _layer_params(i, norm_ffn1_w, ffn1_w_gate, ffn1_w_up, ffn1_w_down, norm_mix_w, w_in,
                  ssd_conv_w, ssd_conv_b, dt_bias, a_log, d_skip, ssd_norm_w, sconv_w, w_out,
                  norm_ffn2_w, ffn2_w_gate, ffn2_w_up, ffn2_w_down):
    o_dt = SSD_WIDTH + SSD_CONV_DIM
    o_sc = o_dt + N_HEADS
    wi = w_in[i]
    wzx = wi[:, :o_dt].astype(BF16)
    wdt = jnp.pad(wi[:, o_dt:o_sc].astype(BF16), ((0, 0), (0, LANES - N_HEADS)))
    wsc = wi[:, o_sc:].astype(BF16)
    pad_heads = lambda v: jnp.pad(v, (0, LANES - N_HEADS)).reshape(1, LANES)
    row = lambda v: v.reshape(1, -1)
    return dict(
        ffn1=(row(norm_ffn1_w[i]), ffn1_w_gate[i].astype(BF16), ffn1_w_up[i].astype(BF16),
              ffn1_w_down[i].astype(BF16)),
        ffn2=(row(norm_ffn2_w[i]), ffn2_w_gate[i].astype(BF16), ffn2_w_up[i].astype(BF16),
              ffn2_w_down[i].astype(BF16)),
        nw=row(norm_mix_w[i]), win=(wzx, wdt, wsc), cw=ssd_conv_w[i], cb=row(ssd_conv_b[i]),
        dtb=pad_heads(dt_bias[i]), alog=pad_heads(a_log[i]),
        dskip=row(jnp.repeat(d_skip[i], HEAD_DIM)), snw=row(ssd_norm_w[i]), scw=sconv_w[i],
        wout=w_out[i].astype(BF16))


def kernel(x_prompt, x_sample, state_ssm, state_ssd_conv, state_sconv, norm_ffn1_w, ffn1_w_gate, ffn1_w_up, ffn1_w_down, norm_mix_w, w_in, ssd_conv_w, ssd_conv_b, dt_bias, a_log, d_skip, ssd_norm_w, sconv_w, w_out, norm_ffn2_w, ffn2_w_gate, ffn2_w_up, ffn2_w_down, final_norm_w):
    depth = w_in.shape[0]
    bp, seq, d = x_prompt.shape
    bs, dec_seq, _ = x_sample.shape
    assert dec_seq == 1, "sample group is one token per sequence"
    fnw = final_norm_w.reshape(1, d)

    xp = x_prompt.reshape(bp * seq, d)
    xs = x_sample.reshape(bs, d)
    outs = [[] for _ in range(6)]
    for i in range(depth):
        p = _layer_params(i, norm_ffn1_w, ffn1_w_gate, ffn1_w_up, ffn1_w_down, norm_mix_w, w_in,
                          ssd_conv_w, ssd_conv_b, dt_bias, a_log, d_skip, ssd_norm_w, sconv_w,
                          w_out, norm_ffn2_w, ffn2_w_gate, ffn2_w_up, ffn2_w_down)
        last = i == depth - 1
        xp = _ffn(xp, *p["ffn1"], tile_m=FFN_TILE_M)
        xp3, s_p, c_p, sc_p = _mixer_prompt(
            xp.reshape(bp, seq, d), p["nw"], *p["win"], p["cw"], p["cb"], p["dtb"], p["alog"],
            p["dskip"], p["snw"], p["scw"], p["wout"])
        xp = _ffn(xp3.reshape(bp * seq, d), *p["ffn2"], fnw if last else None, tile_m=FFN_TILE_M)
        xs = _ffn(xs, *p["ffn1"], tile_m=FFN_TILE_M)
        z, xs_conv, b_rows, c_rows, xdt_t, da_t, ysc, c_s, sc_s = _sample_pre(
            xs, p["nw"], *p["win"], p["cw"], p["cb"], p["dtb"], p["alog"], p["scw"],
            state_ssd_conv[i].transpose(1, 0, 2), state_sconv[i].transpose(1, 0, 2))
        s_s, yraw = _sample_state(state_ssm[i].reshape(bs, SSD_WIDTH, D_STATE), xdt_t, da_t, b_rows, c_rows)
        xs = _sample_post(xs, yraw, xs_conv, z, ysc, p["dskip"], p["snw"], p["wout"])
        xs = _ffn(xs, *p["ffn2"], fnw if last else None, tile_m=FFN_TILE_M)
        for lst, v in zip(outs, (s_p.reshape(bp, N_HEADS, HEAD_DIM, D_STATE), c_p, sc_p,
                                 s_s.reshape(bs, N_HEADS, HEAD_DIM, D_STATE),
                                 c_s.transpose(1, 0, 2), sc_s.transpose(1, 0, 2))):
            lst.append(v)
    return (xp.reshape(bp, seq, d), xs.reshape(bs, dec_seq, d)) + tuple(jnp.stack(l) for l in outs)
```

```python
import functools

import jax
import jax.numpy as jnp
from jax import lax
from jax.experimental import pallas as pl
from jax.experimental.pallas import tpu as pltpu

F32 = jnp.float32
BF16 = jnp.bfloat16

D_MODEL = 1024
SSD_WIDTH = 1024
SC_WIDTH = 1024
HEAD_DIM = 64
N_HEADS = SSD_WIDTH // HEAD_DIM
N_GROUPS = 2
HEADS_PER_GROUP = N_HEADS // N_GROUPS
GROUP_WIDTH = SSD_WIDTH // N_GROUPS
D_STATE = 128
SSD_CONV_W = 4
SSD_CONV_DIM = SSD_WIDTH + 2 * N_GROUPS * D_STATE
SC_CONV_W = 3
CHUNK = 256
NORM_EPS = 1e-6

LANES = 128
SUBLANES = 8

D_ZX = SSD_WIDTH + SSD_CONV_DIM
D_SC = 3 * SC_WIDTH
PROJ_BLOCK = 512

VMEM_LIMIT_BYTES = 56 * 1024 * 1024


def _rmsnorm(x, w):
    ms = jnp.mean(x * x, axis=-1, keepdims=True)
    return x * lax.rsqrt(ms + NORM_EPS) * w


def _silu(x):
    return x * (0.5 * jnp.tanh(0.5 * x) + 0.5)


def _softplus(x):
    return jnp.maximum(x, 0.0) + jnp.log1p(jnp.exp(-jnp.abs(x)))


def _dot(a, b):
    return jnp.dot(a, b, preferred_element_type=F32)


def _dot_nt(a, b):
    return lax.dot_general(a, b, (((1,), (1,)), ((), ())), preferred_element_type=F32)


def _split3(x):
    hi = x.astype(BF16)
    r1 = x - hi.astype(F32)
    mid = r1.astype(BF16)
    lo = (r1 - mid.astype(F32)).astype(BF16)
    return hi, mid, lo


def _expand_heads(v, rows):
    lane = lax.broadcasted_iota(jnp.int32, (rows, LANES), 1)
    first_half = lane < HEAD_DIM
    blocks = []
    for q in range(N_HEADS // 2):
        c0 = jnp.broadcast_to(v[:, 2 * q:2 * q + 1], (rows, LANES))
        c1 = jnp.broadcast_to(v[:, 2 * q + 1:2 * q + 2], (rows, LANES))
        blocks.append(jnp.where(first_half, c0, c1))
    return jnp.concatenate(blocks, axis=1)


def _group_rmsnorm(y, w):
    outs = []
    for g in range(N_GROUPS):
        sl = slice(g * GROUP_WIDTH, (g + 1) * GROUP_WIDTH)
        outs.append(_rmsnorm(y[:, sl], w[:, sl]))
    return jnp.concatenate(outs, axis=1)


def _shift_rows(x, prev_tile, j):
    rolled = pltpu.roll(x, j, 0)
    row = lax.broadcasted_iota(jnp.int32, prev_tile.shape, 0)
    head = jnp.where(row < j, pltpu.roll(prev_tile, j, 0), rolled[0:SUBLANES])
    return jnp.concatenate([head, rolled[SUBLANES:]], axis=0)


def _causal_conv(u, prev_tile, w):
    k = w.shape[0]
    acc = u * w[k - 1:k, :]
    for j in range(1, k):
        acc = acc + _shift_rows(u, prev_tile, j) * w[k - 1 - j:k - j, :]
    return acc


def _ffn_kernel(*refs, final_norm):
    if final_norm:
        xp_ref, xs_ref, nw_ref, wg_ref, wu_ref, wd_ref, fnw_ref, op_ref, os_ref = refs
    else:
        xp_ref, xs_ref, nw_ref, wg_ref, wu_ref, wd_ref, op_ref, os_ref = refs

    def half_step(x_ref, o_ref):
        x = x_ref[...]
        xn = _rmsnorm(x, nw_ref[...]).astype(BF16)
        g = _dot(xn, wg_ref[...])
        u = _dot(xn, wu_ref[...])
        hmid = (_silu(g) * u).astype(BF16)
        y = x + 0.5 * _dot(hmid, wd_ref[...])
        if final_norm:
            y = _rmsnorm(y, fnw_ref[...])
        o_ref[...] = y

    i = pl.program_id(0)
    n_prompt_tiles = pl.num_programs(0) - 1

    @pl.when(i < n_prompt_tiles)
    def _():
        half_step(xp_ref, op_ref)

    @pl.when(i == n_prompt_tiles)
    def _():
        half_step(xs_ref, os_ref)


def _resident(shape):
    return pl.BlockSpec(shape, lambda *_: (0,) * len(shape), pipeline_mode=pl.Buffered(1))


def _ffn(xp, xs, nw, wg, wu, wd, fnw=None, *, tile_m):
    m, d = xp.shape
    ms = xs.shape[0]
    dff = wg.shape[1]
    assert m % tile_m == 0
    n = m // tile_m
    final_norm = fnw is not None
    prompt_tile = pl.BlockSpec((tile_m, d), lambda i: (jnp.minimum(i, n - 1), 0))
    sample_tile = pl.BlockSpec((ms, d), lambda i: (0, 0))
    in_specs = [prompt_tile, sample_tile,
                _resident((1, d)), _resident((d, dff)), _resident((d, dff)), _resident((dff, d))]
    args = [xp, xs, nw, wg, wu, wd]
    if final_norm:
        in_specs.append(_resident((1, d)))
        args.append(fnw)
    return pl.pallas_call(
        functools.partial(_ffn_kernel, final_norm=final_norm),
        out_shape=(jax.ShapeDtypeStruct((m, d), F32), jax.ShapeDtypeStruct((ms, d), F32)),
        grid=(n + 1,),
        in_specs=in_specs,
        out_specs=(prompt_tile, sample_tile),
        compiler_params=pltpu.CompilerParams(
            dimension_semantics=("arbitrary",), vmem_limit_bytes=VMEM_LIMIT_BYTES),
        name="ffn_final" if final_norm else "ffn",
    )(*args)


def _mixer_kernel(x_ref, nw_ref, wzx_ref, wdt_ref, wsc_ref, cw_ref, cb_ref, dtb_ref, alog_ref,
                  dskip_ref, snw_ref, scw_ref, wout_ref,
                  o_ref, sfin_ref, cfin_ref, scfin_ref,
                  st_scr, hist_scr, uhist_scr):
    L = CHUNK
    c = pl.program_id(1)

    @pl.when(c == 0)
    def _():
        st_scr[...] = jnp.zeros_like(st_scr)
        hist_scr[...] = jnp.zeros_like(hist_scr)
        uhist_scr[...] = jnp.zeros_like(uhist_scr)

    x = x_ref[0]
    h = _rmsnorm(x, nw_ref[...]).astype(BF16)
    dt_raw = _dot(h, wdt_ref[...])

    xbc_c_blocks = []
    for b0 in range(0, SSD_CONV_DIM, PROJ_BLOCK):
        cols = slice(b0, b0 + PROJ_BLOCK)
        xbc = _dot(h, wzx_ref[:, SSD_WIDTH + b0:SSD_WIDTH + b0 + PROJ_BLOCK])
        acc = _causal_conv(xbc, hist_scr[:, cols], cw_ref[:, cols])
        hist_scr[:, cols] = xbc[L - SUBLANES:L, :]
        xbc_c_blocks.append(_silu(acc + cb_ref[:, cols]))
    cfin_ref[0] = hist_scr[SUBLANES - (SSD_CONV_W - 1):SUBLANES, :]
    xbc_c = jnp.concatenate(xbc_c_blocks, axis=1)
    xs = xbc_c[:, 0:SSD_WIDTH]
    b_all = xbc_c[:, SSD_WIDTH:SSD_WIDTH + N_GROUPS * D_STATE]
    c_all = xbc_c[:, SSD_WIDTH + N_GROUPS * D_STATE:SSD_CONV_DIM]

    y_sc_blocks = []
    for b0 in range(0, SC_WIDTH, PROJ_BLOCK):
        cols = slice(b0, b0 + PROJ_BLOCK)
        scc = _dot(h, wsc_ref[:, SC_WIDTH + b0:SC_WIDTH + b0 + PROJ_BLOCK])
        sch = _dot(h, wsc_ref[:, 2 * SC_WIDTH + b0:2 * SC_WIDTH + b0 + PROJ_BLOCK])
        u = scc * sch
        v = _causal_conv(u, uhist_scr[:, cols], scw_ref[:, cols])
        uhist_scr[:, cols] = u[L - SUBLANES:L, :]
        scb = _dot(h, wsc_ref[:, b0:b0 + PROJ_BLOCK])
        y_sc_blocks.append((scb * v).astype(BF16))
    scfin_ref[0] = uhist_scr[SUBLANES - (SC_CONV_W - 1):SUBLANES, :]
    y_sc = jnp.concatenate(y_sc_blocks, axis=1)

    z_gate = jnp.concatenate(
        [_silu(_dot(h, wzx_ref[:, b0:b0 + PROJ_BLOCK])) for b0 in range(0, SSD_WIDTH, PROJ_BLOCK)],
        axis=1)

    dt = _softplus(dt_raw + dtb_ref[...])
    a = dt * (-jnp.exp(alog_ref[...]))
    row = lax.broadcasted_iota(jnp.int32, (L, L), 0)
    col = lax.broadcasted_iota(jnp.int32, (L, L), 1)
    causal = row >= col
    tril = jnp.where(causal, 1.0, 0.0).astype(BF16)
    a_hi, a_mid, a_lo = _split3(a)
    acum = _dot(tril, a_hi) + _dot(tril, a_mid) + _dot(tril, a_lo)
    acum_t = acum.T

    acum_e = _expand_heads(acum, L)
    dt_e = _expand_heads(dt, L)
    exp_acum_e = jnp.exp(acum_e)
    decay_end_e = jnp.exp(acum_e[L - 1:L, :] - acum_e)
    xdt = xs * dt_e
    xdt_b = xdt.astype(BF16)
    xdecay_b = (xdt * decay_end_e).astype(BF16)

    lane = lax.broadcasted_iota(jnp.int32, (L, LANES), 1)
    first_half = lane < HEAD_DIM
    y_diag_blocks = []
    y_off_blocks = []
    for g in range(N_GROUPS):
        b_g = b_all[:, g * D_STATE:(g + 1) * D_STATE]
        c_g = c_all[:, g * D_STATE:(g + 1) * D_STATE]
        b_gb = b_g.astype(BF16)
        c_gb = c_g.astype(BF16)
        cb = _dot_nt(c_gb, b_gb)
        for q in range(g * HEADS_PER_GROUP // 2, (g + 1) * HEADS_PER_GROUP // 2):
            ms = []
            for hh in (2 * q, 2 * q + 1):
                seg = acum[:, hh:hh + 1] - acum_t[hh:hh + 1, :]
                decay = jnp.exp(jnp.where(causal, seg, -jnp.inf))
                ms.append((cb * decay).astype(BF16))
            m_cat = jnp.concatenate(ms, axis=1)
            x2 = xdt_b[:, q * LANES:(q + 1) * LANES]
            zero = jnp.zeros_like(x2)
            rhs = jnp.concatenate([jnp.where(first_half, x2, zero),
                                   jnp.where(first_half, zero, x2)], axis=0)
            y_diag_blocks.append(_dot(m_cat, rhs))
        gs = slice(g * GROUP_WIDTH, (g + 1) * GROUP_WIDTH)
        s_enter = st_scr[:, gs]
        y_off_blocks.append(_dot(c_gb, s_enter.astype(BF16)))
        new_states = _dot(b_g.T.astype(BF16), xdecay_b[:, gs])
        st_scr[:, gs] = s_enter * exp_acum_e[L - 1:L, gs] + new_states
    y_diag = jnp.concatenate(y_diag_blocks, axis=1)
    y_off = jnp.concatenate(y_off_blocks, axis=1) * exp_acum_e

    y = (y_diag + y_off + dskip_ref[...] * xs) * z_gate
    y_ssd = _group_rmsnorm(y, snw_ref[...]).astype(BF16)
    mixed = jnp.concatenate([y_ssd, y_sc], axis=1)
    o_ref[0] = x + _dot(mixed, wout_ref[...])

    @pl.when(c == pl.num_programs(1) - 1)
    def _():
        sfin_ref[0] = st_scr[...].T


def _mixer_prompt(x, nw, wzx, wdt, wsc, cw, cb, dtb, alog, dskip, snw, scw, wout):
    nb, seq, d = x.shape
    assert seq % CHUNK == 0
    nc = seq // CHUNK
    bmap = lambda b, c: (b, 0, 0)
    out_shape = (
        jax.ShapeDtypeStruct((nb, seq, d), F32),
        jax.ShapeDtypeStruct((nb, SSD_WIDTH, D_STATE), F32),
        jax.ShapeDtypeStruct((nb, SSD_CONV_W - 1, SSD_CONV_DIM), F32),
        jax.ShapeDtypeStruct((nb, SC_CONV_W - 1, SC_WIDTH), F32),
    )
    return pl.pallas_call(
        _mixer_kernel,
        out_shape=out_shape,
        grid=(nb, nc),
        in_specs=[pl.BlockSpec((1, CHUNK, d), lambda b, c: (b, c, 0)),
                  _resident((1, d)), _resident((d, D_ZX)), _resident((d, LANES)), _resident((d, D_SC)),
                  _resident((SSD_CONV_W, SSD_CONV_DIM)), _resident((1, SSD_CONV_DIM)),
                  _resident((1, LANES)), _resident((1, LANES)), _resident((1, SSD_WIDTH)),
                  _resident((1, SSD_WIDTH)), _resident((SC_CONV_W, SC_WIDTH)),
                  _resident((SSD_WIDTH + SC_WIDTH, d))],
        out_specs=(pl.BlockSpec((1, CHUNK, d), lambda b, c: (b, c, 0)),
                   pl.BlockSpec((1, SSD_WIDTH, D_STATE), bmap),
                   pl.BlockSpec((1, SSD_CONV_W - 1, SSD_CONV_DIM), bmap),
                   pl.BlockSpec((1, SC_CONV_W - 1, SC_WIDTH), bmap)),
        scratch_shapes=[pltpu.VMEM((D_STATE, SSD_WIDTH), F32),
                        pltpu.VMEM((SUBLANES, SSD_CONV_DIM), F32),
                        pltpu.VMEM((SUBLANES, SC_WIDTH), F32)],
        compiler_params=pltpu.CompilerParams(
            dimension_semantics=("arbitrary", "arbitrary"), vmem_limit_bytes=VMEM_LIMIT_BYTES),
        name="mixer_prompt",
    )(x, nw, wzx, wdt, wsc, cw, cb, dtb, alog, dskip, snw, scw, wout)


def _sample_pre_kernel(x_ref, nw_ref, wzx_ref, wdt_ref, wsc_ref, cw_ref, cb_ref, dtb_ref, alog_ref,
                       scw_ref, cst_ref, scst_ref,
                       z_ref, xs_ref, b_ref, c_ref, xdt_t_ref, da_ref, ysc_ref, cnew_ref, scnew_ref):
    nb = x_ref.shape[0]
    x = x_ref[...]
    h = _rmsnorm(x, nw_ref[...]).astype(BF16)
    dt_raw = _dot(h, wdt_ref[...])
    xbc = _dot(h, wzx_ref[:, SSD_WIDTH:D_ZX])
    z_ref[...] = _dot(h, wzx_ref[:, 0:SSD_WIDTH])
    scb = _dot(h, wsc_ref[:, 0:SC_WIDTH])
    scc = _dot(h, wsc_ref[:, SC_WIDTH:2 * SC_WIDTH])
    sch = _dot(h, wsc_ref[:, 2 * SC_WIDTH:D_SC])

    cw = cw_ref[...]
    acc = xbc * cw[SSD_CONV_W - 1:SSD_CONV_W, :]
    for k in range(SSD_CONV_W - 1):
        acc = acc + cst_ref[k] * cw[k:k + 1, :]
    for k in range(SSD_CONV_W - 2):
        cnew_ref[k] = cst_ref[k + 1]
    cnew_ref[SSD_CONV_W - 2] = xbc
    xbc_c = _silu(acc + cb_ref[...])
    xs = xbc_c[:, 0:SSD_WIDTH]
    xs_ref[...] = xs
    b_ref[...] = xbc_c[:, SSD_WIDTH:SSD_WIDTH + N_GROUPS * D_STATE]
    c_ref[...] = xbc_c[:, SSD_WIDTH + N_GROUPS * D_STATE:SSD_CONV_DIM]

    dt = _softplus(dt_raw + dtb_ref[...])
    da_ref[...] = jnp.exp(dt * (-jnp.exp(alog_ref[...])))
    xdt_t_ref[...] = (xs * _expand_heads(dt, nb)).T

    u = scc * sch
    scw = scw_ref[...]
    v = u * scw[SC_CONV_W - 1:SC_CONV_W, :]
    for k in range(SC_CONV_W - 1):
        v = v + scst_ref[k] * scw[k:k + 1, :]
    for k in range(SC_CONV_W - 2):
        scnew_ref[k] = scst_ref[k + 1]
    scnew_ref[SC_CONV_W - 2] = u
    ysc_ref[...] = scb * v


def _sample_pre(x, nw, wzx, wdt, wsc, cw, cb, dtb, alog, scw, cst, scst):
    nb, d = x.shape
    f = lambda *s: jax.ShapeDtypeStruct(s, F32)
    out_shape = (f(nb, SSD_WIDTH), f(nb, SSD_WIDTH), f(nb, N_GROUPS * D_STATE), f(nb, N_GROUPS * D_STATE),
                 f(SSD_WIDTH, nb), f(nb, LANES), f(nb, SC_WIDTH),
                 f(SSD_CONV_W - 1, nb, SSD_CONV_DIM), f(SC_CONV_W - 1, nb, SC_WIDTH))
    return pl.pallas_call(
        _sample_pre_kernel,
        out_shape=out_shape,
        compiler_params=pltpu.CompilerParams(vmem_limit_bytes=VMEM_LIMIT_BYTES),
        name="sample_pre",
    )(x, nw, wzx, wdt, wsc, cw, cb, dtb, alog, scw, cst, scst)


STATE_BATCH_BLOCK = 8


def _split2(x):
    hi = x.astype(BF16)
    return hi, (x - hi.astype(F32)).astype(BF16)


def _sample_state_kernel(da_ref, s0_ref, xdt_ref, b_ref, c_ref, snew_ref, y_ref):
    bb = STATE_BATCH_BLOCK
    blk = pl.program_id(0)
    xdt_t = xdt_ref[0]
    b_rows = b_ref[...]
    c_hi, c_lo = _split2(c_ref[...])
    row = lax.broadcasted_iota(jnp.int32, (bb, SSD_WIDTH), 0)
    y_blk = jnp.zeros((bb, SSD_WIDTH), F32)
    for i in range(bb):
        xdt_col = xdt_t[:, i:i + 1]
        y_parts = []
        for g in range(N_GROUPS):
            ns = slice(g * D_STATE, (g + 1) * D_STATE)
            b_row = b_rows[i:i + 1, ns]
            heads = []
            for hh in range(g * HEADS_PER_GROUP, (g + 1) * HEADS_PER_GROUP):
                rs = slice(hh * HEAD_DIM, (hh + 1) * HEAD_DIM)
                decay = da_ref[blk * bb + i, hh]
                heads.append(s0_ref[i, rs, :] * decay + xdt_col[rs] * b_row)
            s_new = jnp.concatenate(heads, axis=0)
            snew_ref[i, g * GROUP_WIDTH:(g + 1) * GROUP_WIDTH, :] = s_new
            s_hi, s_lo = _split2(s_new)
            lhs = jnp.concatenate([c_hi[:, ns], c_lo[:, ns]], axis=0)
            r_hi = _dot_nt(lhs, s_hi)
            r_lo = _dot_nt(c_hi[:, ns], s_lo)
            y_parts.append(r_hi[i:i + 1] + r_hi[bb + i:bb + i + 1] + r_lo[i:i + 1])
        y_row = jnp.concatenate(y_parts, axis=1)
        y_blk = jnp.where(row == i, y_row, y_blk)
    y_ref[...] = y_blk


def _sample_state(s0, xdt_t, decay, b_rows, c_rows):
    nb = s0.shape[0]
    bb = STATE_BATCH_BLOCK
    assert nb % bb == 0
    nblk = nb // bb
    xdt_blocks = xdt_t.reshape(SSD_WIDTH, nblk, bb).transpose(1, 0, 2)
    return pl.pallas_call(
        _sample_state_kernel,
        out_shape=(jax.ShapeDtypeStruct((nb, SSD_WIDTH, D_STATE), F32),
                   jax.ShapeDtypeStruct((nb, SSD_WIDTH), F32)),
        grid=(nblk,),
        in_specs=[pl.BlockSpec(memory_space=pltpu.SMEM),
                  pl.BlockSpec((bb, SSD_WIDTH, D_STATE), lambda i: (i, 0, 0)),
                  pl.BlockSpec((1, SSD_WIDTH, bb), lambda i: (i, 0, 0)),
                  pl.BlockSpec((bb, N_GROUPS * D_STATE), lambda i: (i, 0)),
                  pl.BlockSpec((bb, N_GROUPS * D_STATE), lambda i: (i, 0))],
        out_specs=(pl.BlockSpec((bb, SSD_WIDTH, D_STATE), lambda i: (i, 0, 0)),
                   pl.BlockSpec((bb, SSD_WIDTH), lambda i: (i, 0))),
        compiler_params=pltpu.CompilerParams(
            dimension_semantics=("arbitrary",), vmem_limit_bytes=VMEM_LIMIT_BYTES),
        name="sample_state",
    )(decay, s0, xdt_blocks, b_rows, c_rows)


def _sample_post_kernel(x_ref, yraw_ref, xs_ref, z_ref, ysc_ref, dskip_ref, snw_ref, wout_ref, o_ref):
    y = yraw_ref[...] + dskip_ref[...] * xs_ref[...]
    y = y * _silu(z_ref[...])
    y_ssd = _group_rmsnorm(y, snw_ref[...])
    mixed = jnp.concatenate([y_ssd, ysc_ref[...]], axis=1).astype(BF16)
    o_ref[...] = x_ref[...] + _dot(mixed, wout_ref[...])


def _sample_post(x, yraw, xs, z, ysc, dskip, snw, wout):
    return pl.pallas_call(
        _sample_post_kernel,
        out_shape=jax.ShapeDtypeStruct(x.shape, F32),
        compiler_params=pltpu.CompilerParams(vmem_limit_bytes=VMEM_LIMIT_BYTES),
        name="sample_post",
    )(x, yraw, xs, z, ysc, dskip, snw, wout)


FFN_TILE_M = 512


def _layer_params(i, norm_ffn1_w, ffn1_w_gate, ffn1_w_up, ffn1_w_down, norm_mix_w, w_in,
                  ssd_conv_w, ssd_conv_b, dt_bias, a_log, d_skip, ssd_norm_w, sconv_w, w_out,
                  norm_ffn2_w, ffn2_w_gate, ffn2_w_up, ffn2_w_down):
    o_dt = SSD_WIDTH + SSD_CONV_DIM
    o_sc = o_dt + N_HEADS
    wi = w_in[i]
    wzx = wi[:, :o_dt].astype(BF16)
    wdt = jnp.pad(wi[:, o_dt:o_sc].astype(BF16), ((0, 0), (0, LANES - N_HEADS)))
    wsc = wi[:, o_sc:].astype(BF16)
    pad_heads = lambda v: jnp.pad(v, (0, LANES - N_HEADS)).reshape(1, LANES)
    row = lambda v: v.reshape(1, -1)
    return dict(
        ffn1=(row(norm_ffn1_w[i]), ffn1_w_gate[i].astype(BF16), ffn1_w_up[i].astype(BF16),
              ffn1_w_down[i].astype(BF16)),
        ffn2=(row(norm_ffn2_w[i]), ffn2_w_gate[i].astype(BF16), ffn2_w_up[i].astype(BF16),
              ffn2_w_down[i].astype(BF16)),
        nw=row(norm_mix_w[i]), win=(wzx, wdt, wsc), cw=ssd_conv_w[i], cb=row(ssd_conv_b[i]),
        dtb=pad_heads(dt_bias[i]), alog=pad_heads(a_log[i]),
        dskip=row(jnp.repeat(d_skip[i], HEAD_DIM)), snw=row(ssd_norm_w[i]), scw=sconv_w[i],
        wout=w_out[i].astype(BF16))


def kernel(x_prompt, x_sample, state_ssm, state_ssd_conv, state_sconv, norm_ffn1_w, ffn1_w_gate, ffn1_w_up, ffn1_w_down, norm_mix_w, w_in, ssd_conv_w, ssd_conv_b, dt_bias, a_log, d_skip, ssd_norm_w, sconv_w, w_out, norm_ffn2_w, ffn2_w_gate, ffn2_w_up, ffn2_w_down, final_norm_w):
    depth = w_in.shape[0]
    bp, seq, d = x_prompt.shape
    bs, dec_seq, _ = x_sample.shape
    assert dec_seq == 1, "sample group is one token per sequence"
    fnw = final_norm_w.reshape(1, d)

    xp = x_prompt.reshape(bp * seq, d)
    xs = x_sample.reshape(bs, d)
    outs = [[] for _ in range(6)]
    for i in range(depth):
        p = _layer_params(i, norm_ffn1_w, ffn1_w_gate, ffn1_w_up, ffn1_w_down, norm_mix_w, w_in,
                          ssd_conv_w, ssd_conv_b, dt_bias, a_log, d_skip, ssd_norm_w, sconv_w,
                          w_out, norm_ffn2_w, ffn2_w_gate, ffn2_w_up, ffn2_w_down)
        last = i == depth - 1
        xp, xs = _ffn(xp, xs, *p["ffn1"], tile_m=FFN_TILE_M)
        xp3, s_p, c_p, sc_p = _mixer_prompt(
            xp.reshape(bp, seq, d), p["nw"], *p["win"], p["cw"], p["cb"], p["dtb"], p["alog"],
            p["dskip"], p["snw"], p["scw"], p["wout"])
        z, xs_conv, b_rows, c_rows, xdt_t, decay, ysc, c_s, sc_s = _sample_pre(
            xs, p["nw"], *p["win"], p["cw"], p["cb"], p["dtb"], p["alog"], p["scw"],
            state_ssd_conv[i].transpose(1, 0, 2), state_sconv[i].transpose(1, 0, 2))
        s_s, yraw = _sample_state(state_ssm[i].reshape(bs, SSD_WIDTH, D_STATE), xdt_t, decay, b_rows, c_rows)
        xs = _sample_post(xs, yraw, xs_conv, z, ysc, p["dskip"], p["snw"], p["wout"])
        xp, xs = _ffn(xp3.reshape(bp * seq, d), xs, *p["ffn2"], fnw if last else None, tile_m=FFN_TILE_M)
        for lst, v in zip(outs, (s_p.reshape(bp, N_HEADS, HEAD_DIM, D_STATE), c_p, sc_p,
                                 s_s.reshape(bs, N_HEADS, HEAD_DIM, D_STATE),
                                 c_s.transpose(1, 0, 2), sc_s.transpose(1, 0, 2))):
            lst.append(v)
    return (xp.reshape(bp, seq, d), xs.reshape(bs, dec_seq, d)) + tuple(jnp.stack(l) for l in outs)
```

```python
import functools

import jax
import jax.numpy as jnp
from jax import lax
from jax.experimental import pallas as pl
from jax.experimental.pallas import tpu as pltpu

F32 = jnp.float32
BF16 = jnp.bfloat16

D_MODEL = 1024
SSD_WIDTH = 1024
SC_WIDTH = 1024
HEAD_DIM = 64
N_HEADS = SSD_WIDTH // HEAD_DIM
N_GROUPS = 2
HEADS_PER_GROUP = N_HEADS // N_GROUPS
GROUP_WIDTH = SSD_WIDTH // N_GROUPS
D_STATE = 128
SSD_CONV_W = 4
SSD_CONV_DIM = SSD_WIDTH + 2 * N_GROUPS * D_STATE
SC_CONV_W = 3
CHUNK = 256
NORM_EPS = 1e-6

LANES = 128
SUBLANES = 8

D_ZX = SSD_WIDTH + SSD_CONV_DIM
D_SC = 3 * SC_WIDTH
PROJ_BLOCK = 512

VMEM_LIMIT_BYTES = 56 * 1024 * 1024


def _rmsnorm(x, w):
    ms = jnp.mean(x * x, axis=-1, keepdims=True)
    return x * lax.rsqrt(ms + NORM_EPS) * w


def _silu(x):
    return x * (0.5 * jnp.tanh(0.5 * x) + 0.5)


def _softplus(x):
    return jnp.maximum(x, 0.0) + jnp.log1p(jnp.exp(-jnp.abs(x)))


def _dot(a, b):
    return jnp.dot(a, b, preferred_element_type=F32)


def _dot_nt(a, b):
    return lax.dot_general(a, b, (((1,), (1,)), ((), ())), preferred_element_type=F32)


def _split3(x):
    hi = x.astype(BF16)
    r1 = x - hi.astype(F32)
    mid = r1.astype(BF16)
    lo = (r1 - mid.astype(F32)).astype(BF16)
    return hi, mid, lo


def _expand_heads(v, rows):
    lane = lax.broadcasted_iota(jnp.int32, (rows, LANES), 1)
    first_half = lane < HEAD_DIM
    blocks = []
    for q in range(N_HEADS // 2):
        c0 = jnp.broadcast_to(v[:, 2 * q:2 * q + 1], (rows, LANES))
        c1 = jnp.broadcast_to(v[:, 2 * q + 1:2 * q + 2], (rows, LANES))
        blocks.append(jnp.where(first_half, c0, c1))
    return jnp.concatenate(blocks, axis=1)


def _group_rmsnorm(y, w):
    outs = []
    for g in range(N_GROUPS):
        sl = slice(g * GROUP_WIDTH, (g + 1) * GROUP_WIDTH)
        outs.append(_rmsnorm(y[:, sl], w[:, sl]))
    return jnp.concatenate(outs, axis=1)


def _shift_rows(x, prev_tile, j):
    rolled = pltpu.roll(x, j, 0)
    row = lax.broadcasted_iota(jnp.int32, prev_tile.shape, 0)
    head = jnp.where(row < j, pltpu.roll(prev_tile, j, 0), rolled[0:SUBLANES])
    return jnp.concatenate([head, rolled[SUBLANES:]], axis=0)


def _causal_conv(u, prev_tile, w):
    k = w.shape[0]
    acc = u * w[k - 1:k, :]
    for j in range(1, k):
        acc = acc + _shift_rows(u, prev_tile, j) * w[k - 1 - j:k - j, :]
    return acc


WEIGHT_LOAD_STEPS = 8


def _load_weight_rows(step, w_ref, w_bf_ref):
    rows = w_ref.shape[0]
    r0 = pl.multiple_of(step * rows, rows)
    w_bf_ref[pl.ds(r0, rows), :] = w_ref[...].astype(BF16)


def _weight_rows_spec(w, steps):
    rows, cols = w.shape
    assert rows % (steps * 2 * SUBLANES) == 0
    return pl.BlockSpec((rows // steps, cols), lambda i: (jnp.minimum(i, steps - 1), 0))


def _ffn_kernel(*refs, final_norm):
    if final_norm:
        (xp_ref, xs_ref, nw_ref, wg_ref, wu_ref, wd_ref, fnw_ref, op_ref, os_ref,
         wg_bf, wu_bf, wd_bf) = refs
    else:
        xp_ref, xs_ref, nw_ref, wg_ref, wu_ref, wd_ref, op_ref, os_ref, wg_bf, wu_bf, wd_bf = refs

    def half_step(x_ref, o_ref):
        x = x_ref[...]
        xn = _rmsnorm(x, nw_ref[...]).astype(BF16)
        g = _dot(xn, wg_bf[...])
        u = _dot(xn, wu_bf[...])
        hmid = (_silu(g) * u).astype(BF16)
        y = x + 0.5 * _dot(hmid, wd_bf[...])
        if final_norm:
            y = _rmsnorm(y, fnw_ref[...])
        o_ref[...] = y

    i = pl.program_id(0)
    sample_step = pl.num_programs(0) - 1

    @pl.when(i < WEIGHT_LOAD_STEPS)
    def _():
        for w_ref, w_bf in ((wg_ref, wg_bf), (wu_ref, wu_bf), (wd_ref, wd_bf)):
            _load_weight_rows(i, w_ref, w_bf)

    @pl.when((i >= WEIGHT_LOAD_STEPS) & (i < sample_step))
    def _():
        half_step(xp_ref, op_ref)

    @pl.when(i == sample_step)
    def _():
        half_step(xs_ref, os_ref)


def _resident(shape):
    return pl.BlockSpec(shape, lambda *_: (0,) * len(shape), pipeline_mode=pl.Buffered(1))


def _ffn(xp, xs, nw, wg, wu, wd, fnw=None, *, tile_m):
    m, d = xp.shape
    ms = xs.shape[0]
    dff = wg.shape[1]
    assert m % tile_m == 0
    n = m // tile_m
    final_norm = fnw is not None
    ws = WEIGHT_LOAD_STEPS
    prompt_tile = pl.BlockSpec((tile_m, d), lambda i: (jnp.clip(i - ws, 0, n - 1), 0))
    sample_tile = pl.BlockSpec((ms, d), lambda i: (0, 0))
    in_specs = [prompt_tile, sample_tile, _resident((1, d)),
                _weight_rows_spec(wg, ws), _weight_rows_spec(wu, ws), _weight_rows_spec(wd, ws)]
    args = [xp, xs, nw, wg, wu, wd]
    if final_norm:
        in_specs.append(_resident((1, d)))
        args.append(fnw)
    return pl.pallas_call(
        functools.partial(_ffn_kernel, final_norm=final_norm),
        out_shape=(jax.ShapeDtypeStruct((m, d), F32), jax.ShapeDtypeStruct((ms, d), F32)),
        grid=(ws + n + 1,),
        in_specs=in_specs,
        out_specs=(prompt_tile, sample_tile),
        scratch_shapes=[pltpu.VMEM((d, dff), BF16), pltpu.VMEM((d, dff), BF16), pltpu.VMEM((dff, d), BF16)],
        compiler_params=pltpu.CompilerParams(
            dimension_semantics=("arbitrary",), vmem_limit_bytes=VMEM_LIMIT_BYTES),
        name="ffn_final" if final_norm else "ffn",
    )(*args)


def _load_w_in_rows(step, win_ref, wzx_bf, wdt_bf, wsc_bf):
    rows = win_ref.shape[0]
    dst = pl.ds(pl.multiple_of(step * rows, rows), rows)
    o_dt = D_ZX
    o_sc = D_ZX + N_HEADS
    wzx_bf[dst, :] = win_ref[:, 0:o_dt].astype(BF16)
    lane = lax.broadcasted_iota(jnp.int32, (rows, LANES), 1)
    wdt_bf[dst, :] = jnp.where(lane < N_HEADS, win_ref[:, o_dt:o_dt + LANES], 0.0).astype(BF16)
    wsc_bf[dst, :] = win_ref[:, o_sc:o_sc + D_SC].astype(BF16)


def _mixer_kernel(chunks_per_seq, x_ref, nw_ref, win_ref, cw_ref, cb_ref, dtb_ref, alog_ref,
                  dskip_ref, snw_ref, scw_ref, wout_ref,
                  o_ref, sfin_ref, cfin_ref, scfin_ref,
                  st_scr, hist_scr, uhist_scr, wzx_bf, wdt_bf, wsc_bf, wout_bf):
    i = pl.program_id(0)

    @pl.when(i < WEIGHT_LOAD_STEPS)
    def _():
        _load_w_in_rows(i, win_ref, wzx_bf, wdt_bf, wsc_bf)
        _load_weight_rows(i, wout_ref, wout_bf)

    @pl.when(i >= WEIGHT_LOAD_STEPS)
    def _():
        c = lax.rem(i - WEIGHT_LOAD_STEPS, chunks_per_seq)
        _mixer_chunk(c == 0, c == chunks_per_seq - 1, x_ref, nw_ref, wzx_bf, wdt_bf, wsc_bf, cw_ref,
                     cb_ref, dtb_ref, alog_ref, dskip_ref, snw_ref, scw_ref, wout_bf,
                     o_ref, sfin_ref, cfin_ref, scfin_ref, st_scr, hist_scr, uhist_scr)


def _mixer_chunk(first, last, x_ref, nw_ref, wzx_ref, wdt_ref, wsc_ref, cw_ref, cb_ref, dtb_ref, alog_ref,
                 dskip_ref, snw_ref, scw_ref, wout_ref,
                 o_ref, sfin_ref, cfin_ref, scfin_ref,
                 st_scr, hist_scr, uhist_scr):
    L = CHUNK

    @pl.when(first)
    def _():
        st_scr[...] = jnp.zeros_like(st_scr)
        hist_scr[...] = jnp.zeros_like(hist_scr)
        uhist_scr[...] = jnp.zeros_like(uhist_scr)

    x = x_ref[0]
    h = _rmsnorm(x, nw_ref[...]).astype(BF16)
    dt_raw = _dot(h, wdt_ref[...])

    xbc_c_blocks = []
    for b0 in range(0, SSD_CONV_DIM, PROJ_BLOCK):
        cols = slice(b0, b0 + PROJ_BLOCK)
        xbc = _dot(h, wzx_ref[:, SSD_WIDTH + b0:SSD_WIDTH + b0 + PROJ_BLOCK])
        acc = _causal_conv(xbc, hist_scr[:, cols], cw_ref[:, cols])
        hist_scr[:, cols] = xbc[L - SUBLANES:L, :]
        xbc_c_blocks.append(_silu(acc + cb_ref[:, cols]))
    cfin_ref[0] = hist_scr[SUBLANES - (SSD_CONV_W - 1):SUBLANES, :]
    xbc_c = jnp.concatenate(xbc_c_blocks, axis=1)
    xs = xbc_c[:, 0:SSD_WIDTH]
    b_all = xbc_c[:, SSD_WIDTH:SSD_WIDTH + N_GROUPS * D_STATE]
    c_all = xbc_c[:, SSD_WIDTH + N_GROUPS * D_STATE:SSD_CONV_DIM]

    y_sc_blocks = []
    for b0 in range(0, SC_WIDTH, PROJ_BLOCK):
        cols = slice(b0, b0 + PROJ_BLOCK)
        scc = _dot(h, wsc_ref[:, SC_WIDTH + b0:SC_WIDTH + b0 + PROJ_BLOCK])
        sch = _dot(h, wsc_ref[:, 2 * SC_WIDTH + b0:2 * SC_WIDTH + b0 + PROJ_BLOCK])
        u = scc * sch
        v = _causal_conv(u, uhist_scr[:, cols], scw_ref[:, cols])
        uhist_scr[:, cols] = u[L - SUBLANES:L, :]
        scb = _dot(h, wsc_ref[:, b0:b0 + PROJ_BLOCK])
        y_sc_blocks.append((scb * v).astype(BF16))
    scfin_ref[0] = uhist_scr[SUBLANES - (SC_CONV_W - 1):SUBLANES, :]
    y_sc = jnp.concatenate(y_sc_blocks, axis=1)

    z_gate = jnp.concatenate(
        [_silu(_dot(h, wzx_ref[:, b0:b0 + PROJ_BLOCK])) for b0 in range(0, SSD_WIDTH, PROJ_BLOCK)],
        axis=1)

    dt = _softplus(dt_raw + dtb_ref[...])
    a = dt * (-jnp.exp(alog_ref[...]))
    row = lax.broadcasted_iota(jnp.int32, (L, L), 0)
    col = lax.broadcasted_iota(jnp.int32, (L, L), 1)
    causal = row >= col
    tril = jnp.where(causal, 1.0, 0.0).astype(BF16)
    a_hi, a_mid, a_lo = _split3(a)
    acum = _dot(tril, a_hi) + _dot(tril, a_mid) + _dot(tril, a_lo)
    acum_t = acum.T

    acum_e = _expand_heads(acum, L)
    dt_e = _expand_heads(dt, L)
    exp_acum_e = jnp.exp(acum_e)
    decay_end_e = jnp.exp(acum_e[L - 1:L, :] - acum_e)
    xdt = xs * dt_e
    xdt_b = xdt.astype(BF16)
    xdecay_b = (xdt * decay_end_e).astype(BF16)

    lane = lax.broadcasted_iota(jnp.int32, (L, LANES), 1)
    first_half = lane < HEAD_DIM
    y_diag_blocks = []
    y_off_blocks = []
    for g in range(N_GROUPS):
        b_g = b_all[:, g * D_STATE:(g + 1) * D_STATE]
        c_g = c_all[:, g * D_STATE:(g + 1) * D_STATE]
        b_gb = b_g.astype(BF16)
        c_gb = c_g.astype(BF16)
        cb = _dot_nt(c_gb, b_gb)
        for q in range(g * HEADS_PER_GROUP // 2, (g + 1) * HEADS_PER_GROUP // 2):
            ms = []
            for hh in (2 * q, 2 * q + 1):
                seg = acum[:, hh:hh + 1] - acum_t[hh:hh + 1, :]
                decay = jnp.exp(jnp.where(causal, seg, -jnp.inf))
                ms.append((cb * decay).astype(BF16))
            m_cat = jnp.concatenate(ms, axis=1)
            x2 = xdt_b[:, q * LANES:(q + 1) * LANES]
            zero = jnp.zeros_like(x2)
            rhs = jnp.concatenate([jnp.where(first_half, x2, zero),
                                   jnp.where(first_half, zero, x2)], axis=0)
            y_diag_blocks.append(_dot(m_cat, rhs))
        gs = slice(g * GROUP_WIDTH, (g + 1) * GROUP_WIDTH)
        s_enter = st_scr[:, gs]
        y_off_blocks.append(_dot(c_gb, s_enter.astype(BF16)))
        new_states = _dot(b_g.T.astype(BF16), xdecay_b[:, gs])
        st_scr[:, gs] = s_enter * exp_acum_e[L - 1:L, gs] + new_states
    y_diag = jnp.concatenate(y_diag_blocks, axis=1)
    y_off = jnp.concatenate(y_off_blocks, axis=1) * exp_acum_e

    y = (y_diag + y_off + dskip_ref[...] * xs) * z_gate
    y_ssd = _group_rmsnorm(y, snw_ref[...]).astype(BF16)
    mixed = jnp.concatenate([y_ssd, y_sc], axis=1)
    o_ref[0] = x + _dot(mixed, wout_ref[...])

    @pl.when(last)
    def _():
        sfin_ref[0] = st_scr[...].T


def _mixer_prompt(x, nw, win, cw, cb, dtb, alog, dskip, snw, scw, wout):
    nb, seq, d = x.shape
    assert seq % CHUNK == 0
    nc = seq // CHUNK
    ws = WEIGHT_LOAD_STEPS

    def chunk_map(i):
        t = jnp.maximum(i - ws, 0)
        return (t // nc, t % nc, 0)

    def seq_map(i):
        return (jnp.maximum(i - ws, 0) // nc, 0, 0)

    out_shape = (
        jax.ShapeDtypeStruct((nb, seq, d), F32),
        jax.ShapeDtypeStruct((nb, SSD_WIDTH, D_STATE), F32),
        jax.ShapeDtypeStruct((nb, SSD_CONV_W - 1, SSD_CONV_DIM), F32),
        jax.ShapeDtypeStruct((nb, SC_CONV_W - 1, SC_WIDTH), F32),
    )
    return pl.pallas_call(
        functools.partial(_mixer_kernel, nc),
        out_shape=out_shape,
        grid=(ws + nb * nc,),
        in_specs=[pl.BlockSpec((1, CHUNK, d), chunk_map),
                  _resident((1, d)), _weight_rows_spec(win, ws),
                  _resident((SSD_CONV_W, SSD_CONV_DIM)), _resident((1, SSD_CONV_DIM)),
                  _resident((1, LANES)), _resident((1, LANES)), _resident((1, SSD_WIDTH)),
                  _resident((1, SSD_WIDTH)), _resident((SC_CONV_W, SC_WIDTH)),
                  _weight_rows_spec(wout, ws)],
        out_specs=(pl.BlockSpec((1, CHUNK, d), chunk_map),
                   pl.BlockSpec((1, SSD_WIDTH, D_STATE), seq_map),
                   pl.BlockSpec((1, SSD_CONV_W - 1, SSD_CONV_DIM), seq_map),
                   pl.BlockSpec((1, SC_CONV_W - 1, SC_WIDTH), seq_map)),
        scratch_shapes=[pltpu.VMEM((D_STATE, SSD_WIDTH), F32),
                        pltpu.VMEM((SUBLANES, SSD_CONV_DIM), F32),
                        pltpu.VMEM((SUBLANES, SC_WIDTH), F32),
                        pltpu.VMEM((d, D_ZX), BF16), pltpu.VMEM((d, LANES), BF16),
                        pltpu.VMEM((d, D_SC), BF16), pltpu.VMEM((SSD_WIDTH + SC_WIDTH, d), BF16)],
        compiler_params=pltpu.CompilerParams(
            dimension_semantics=("arbitrary",), vmem_limit_bytes=VMEM_LIMIT_BYTES),
        name="mixer_prompt",
    )(x, nw, win, cw, cb, dtb, alog, dskip, snw, scw, wout)


def _sample_pre_kernel(x_ref, nw_ref, win_ref, cw_ref, cb_ref, dtb_ref, alog_ref,
                       scw_ref, cst_ref, scst_ref,
                       z_ref, xs_ref, b_ref, c_ref, xdt_t_ref, da_ref, ysc_ref, cnew_ref, scnew_ref):
    nb = x_ref.shape[0]
    x = x_ref[...]
    h = _rmsnorm(x, nw_ref[...]).astype(BF16)
    o_dt = D_ZX
    o_sc = D_ZX + N_HEADS
    proj = lambda c0, width: _dot(h, win_ref[:, c0:c0 + width].astype(BF16))
    lane = lax.broadcasted_iota(jnp.int32, (win_ref.shape[0], LANES), 1)
    w_dt = jnp.where(lane < N_HEADS, win_ref[:, o_dt:o_dt + LANES], 0.0).astype(BF16)
    dt_raw = _dot(h, w_dt)
    xbc = proj(SSD_WIDTH, SSD_CONV_DIM)
    z_ref[...] = proj(0, SSD_WIDTH)
    scb = proj(o_sc, SC_WIDTH)
    scc = proj(o_sc + SC_WIDTH, SC_WIDTH)
    sch = proj(o_sc + 2 * SC_WIDTH, SC_WIDTH)

    cw = cw_ref[...]
    acc = xbc * cw[SSD_CONV_W - 1:SSD_CONV_W, :]
    for k in range(SSD_CONV_W - 1):
        acc = acc + cst_ref[k] * cw[k:k + 1, :]
    for k in range(SSD_CONV_W - 2):
        cnew_ref[k] = cst_ref[k + 1]
    cnew_ref[SSD_CONV_W - 2] = xbc
    xbc_c = _silu(acc + cb_ref[...])
    xs = xbc_c[:, 0:SSD_WIDTH]
    xs_ref[...] = xs
    b_ref[...] = xbc_c[:, SSD_WIDTH:SSD_WIDTH + N_GROUPS * D_STATE]
    c_ref[...] = xbc_c[:, SSD_WIDTH + N_GROUPS * D_STATE:SSD_CONV_DIM]

    dt = _softplus(dt_raw + dtb_ref[...])
    da_ref[...] = jnp.exp(dt * (-jnp.exp(alog_ref[...])))
    xdt_t_ref[...] = (xs * _expand_heads(dt, nb)).T

    u = scc * sch
    scw = scw_ref[...]
    v = u * scw[SC_CONV_W - 1:SC_CONV_W, :]
    for k in range(SC_CONV_W - 1):
        v = v + scst_ref[k] * scw[k:k + 1, :]
    for k in range(SC_CONV_W - 2):
        scnew_ref[k] = scst_ref[k + 1]
    scnew_ref[SC_CONV_W - 2] = u
    ysc_ref[...] = scb * v


def _sample_pre(x, nw, win, cw, cb, dtb, alog, scw, cst, scst):
    nb, d = x.shape
    f = lambda *s: jax.ShapeDtypeStruct(s, F32)
    out_shape = (f(nb, SSD_WIDTH), f(nb, SSD_WIDTH), f(nb, N_GROUPS * D_STATE), f(nb, N_GROUPS * D_STATE),
                 f(SSD_WIDTH, nb), f(nb, LANES), f(nb, SC_WIDTH),
                 f(SSD_CONV_W - 1, nb, SSD_CONV_DIM), f(SC_CONV_W - 1, nb, SC_WIDTH))
    return pl.pallas_call(
        _sample_pre_kernel,
        out_shape=out_shape,
        compiler_params=pltpu.CompilerParams(vmem_limit_bytes=VMEM_LIMIT_BYTES),
        name="sample_pre",
    )(x, nw, win, cw, cb, dtb, alog, scw, cst, scst)


STATE_BATCH_BLOCK = 8


def _split2(x):
    hi = x.astype(BF16)
    return hi, (x - hi.astype(F32)).astype(BF16)


def _sample_state_kernel(da_ref, s0_ref, xdt_ref, b_ref, c_ref, snew_ref, y_ref):
    bb = STATE_BATCH_BLOCK
    blk = pl.program_id(0)
    xdt_t = xdt_ref[0]
    b_rows = b_ref[...]
    c_hi, c_lo = _split2(c_ref[...])
    row = lax.broadcasted_iota(jnp.int32, (bb, SSD_WIDTH), 0)
    y_blk = jnp.zeros((bb, SSD_WIDTH), F32)
    for i in range(bb):
        xdt_col = xdt_t[:, i:i + 1]
        y_parts = []
        for g in range(N_GROUPS):
            ns = slice(g * D_STATE, (g + 1) * D_STATE)
            b_row = b_rows[i:i + 1, ns]
            heads = []
            for hh in range(g * HEADS_PER_GROUP, (g + 1) * HEADS_PER_GROUP):
                rs = slice(hh * HEAD_DIM, (hh + 1) * HEAD_DIM)
                decay = da_ref[blk * bb + i, hh]
                heads.append(s0_ref[i, rs, :] * decay + xdt_col[rs] * b_row)
            s_new = jnp.concatenate(heads, axis=0)
            snew_ref[i, g * GROUP_WIDTH:(g + 1) * GROUP_WIDTH, :] = s_new
            s_hi, s_lo = _split2(s_new)
            lhs = jnp.concatenate([c_hi[:, ns], c_lo[:, ns]], axis=0)
            r_hi = _dot_nt(lhs, s_hi)
            r_lo = _dot_nt(c_hi[:, ns], s_lo)
            y_parts.append(r_hi[i:i + 1] + r_hi[bb + i:bb + i + 1] + r_lo[i:i + 1])
        y_row = jnp.concatenate(y_parts, axis=1)
        y_blk = jnp.where(row == i, y_row, y_blk)
    y_ref[...] = y_blk


def _sample_state(s0, xdt_t, decay, b_rows, c_rows):
    nb = s0.shape[0]
    bb = STATE_BATCH_BLOCK
    assert nb % bb == 0
    nblk = nb // bb
    xdt_blocks = xdt_t.reshape(SSD_WIDTH, nblk, bb).transpose(1, 0, 2)
    return pl.pallas_call(
        _sample_state_kernel,
        out_shape=(jax.ShapeDtypeStruct((nb, SSD_WIDTH, D_STATE), F32),
                   jax.ShapeDtypeStruct((nb, SSD_WIDTH), F32)),
        grid=(nblk,),
        in_specs=[pl.BlockSpec(memory_space=pltpu.SMEM),
                  pl.BlockSpec((bb, SSD_WIDTH, D_STATE), lambda i: (i, 0, 0)),
                  pl.BlockSpec((1, SSD_WIDTH, bb), lambda i: (i, 0, 0)),
                  pl.BlockSpec((bb, N_GROUPS * D_STATE), lambda i: (i, 0)),
                  pl.BlockSpec((bb, N_GROUPS * D_STATE), lambda i: (i, 0))],
        out_specs=(pl.BlockSpec((bb, SSD_WIDTH, D_STATE), lambda i: (i, 0, 0)),
                   pl.BlockSpec((bb, SSD_WIDTH), lambda i: (i, 0))),
        compiler_params=pltpu.CompilerParams(
            dimension_semantics=("arbitrary",), vmem_limit_bytes=VMEM_LIMIT_BYTES),
        name="sample_state",
    )(decay, s0, xdt_blocks, b_rows, c_rows)


def _sample_post_kernel(x_ref, yraw_ref, xs_ref, z_ref, ysc_ref, dskip_ref, snw_ref, wout_ref, o_ref):
    y = yraw_ref[...] + dskip_ref[...] * xs_ref[...]
    y = y * _silu(z_ref[...])
    y_ssd = _group_rmsnorm(y, snw_ref[...])
    mixed = jnp.concatenate([y_ssd, ysc_ref[...]], axis=1).astype(BF16)
    o_ref[...] = x_ref[...] + _dot(mixed, wout_ref[...].astype(BF16))


def _sample_post(x, yraw, xs, z, ysc, dskip, snw, wout):
    return pl.pallas_call(
        _sample_post_kernel,
        out_shape=jax.ShapeDtypeStruct(x.shape, F32),
        compiler_params=pltpu.CompilerParams(vmem_limit_bytes=VMEM_LIMIT_BYTES),
        name="sample_post",
    )(x, yraw, xs, z, ysc, dskip, snw, wout)


FFN_TILE_M = 512


def _layer_params(i, norm_ffn1_w, ffn1_w_gate, ffn1_w_up, ffn1_w_down, norm_mix_w, w_in,
                  ssd_conv_w, ssd_conv_b, dt_bias, a_log, d_skip, ssd_norm_w, sconv_w, w_out,
                  norm_ffn2_w, ffn2_w_gate, ffn2_w_up, ffn2_w_down):
    pad_heads = lambda v: jnp.pad(v, (0, LANES - N_HEADS)).reshape(1, LANES)
    row = lambda v: v.reshape(1, -1)
    return dict(
        ffn1=(row(norm_ffn1_w[i]), ffn1_w_gate[i], ffn1_w_up[i], ffn1_w_down[i]),
        ffn2=(row(norm_ffn2_w[i]), ffn2_w_gate[i], ffn2_w_up[i], ffn2_w_down[i]),
        nw=row(norm_mix_w[i]), win=w_in[i], cw=ssd_conv_w[i], cb=row(ssd_conv_b[i]),
        dtb=pad_heads(dt_bias[i]), alog=pad_heads(a_log[i]),
        dskip=row(jnp.repeat(d_skip[i], HEAD_DIM)), snw=row(ssd_norm_w[i]), scw=sconv_w[i],
        wout=w_out[i])


def kernel(x_prompt, x_sample, state_ssm, state_ssd_conv, state_sconv, norm_ffn1_w, ffn1_w_gate, ffn1_w_up, ffn1_w_down, norm_mix_w, w_in, ssd_conv_w, ssd_conv_b, dt_bias, a_log, d_skip, ssd_norm_w, sconv_w, w_out, norm_ffn2_w, ffn2_w_gate, ffn2_w_up, ffn2_w_down, final_norm_w):
    depth = w_in.shape[0]
    bp, seq, d = x_prompt.shape
    bs, dec_seq, _ = x_sample.shape
    assert dec_seq == 1, "sample group is one token per sequence"
    fnw = final_norm_w.reshape(1, d)

    xp = x_prompt.reshape(bp * seq, d)
    xs = x_sample.reshape(bs, d)
    outs = [[] for _ in range(6)]
    for i in range(depth):
        p = _layer_params(i, norm_ffn1_w, ffn1_w_gate, ffn1_w_up, ffn1_w_down, norm_mix_w, w_in,
                          ssd_conv_w, ssd_conv_b, dt_bias, a_log, d_skip, ssd_norm_w, sconv_w,
                          w_out, norm_ffn2_w, ffn2_w_gate, ffn2_w_up, ffn2_w_down)
        last = i == depth - 1
        xp, xs = _ffn(xp, xs, *p["ffn1"], tile_m=FFN_TILE_M)
        xp3, s_p, c_p, sc_p = _mixer_prompt(
            xp.reshape(bp, seq, d), p["nw"], p["win"], p["cw"], p["cb"], p["dtb"], p["alog"],
            p["dskip"], p["snw"], p["scw"], p["wout"])
        z, xs_conv, b_rows, c_rows, xdt_t, decay, ysc, c_s, sc_s = _sample_pre(
            xs, p["nw"], p["win"], p["cw"], p["cb"], p["dtb"], p["alog"], p["scw"],
            state_ssd_conv[i].transpose(1, 0, 2), state_sconv[i].transpose(1, 0, 2))
        s_s, yraw = _sample_state(state_ssm[i].reshape(bs, SSD_WIDTH, D_STATE), xdt_t, decay, b_rows, c_rows)
        xs = _sample_post(xs, yraw, xs_conv, z, ysc, p["dskip"], p["snw"], p["wout"])
        xp, xs = _ffn(xp3.reshape(bp * seq, d), xs, *p["ffn2"], fnw if last else None, tile_m=FFN_TILE_M)
        for lst, v in zip(outs, (s_p.reshape(bp, N_HEADS, HEAD_DIM, D_STATE), c_p, sc_p,
                                 s_s.reshape(bs, N_HEADS, HEAD_DIM, D_STATE),
                                 c_s.transpose(1, 0, 2), sc_s.transpose(1, 0, 2))):
            lst.append(v)
    return (xp.reshape(bp, seq, d), xs.reshape(bs, dec_seq, d)) + tuple(jnp.stack(l) for l in outs)
```

```python
import functools

import jax
import jax.numpy as jnp
from jax import lax
from jax.experimental import pallas as pl
from jax.experimental.pallas import tpu as pltpu

F32 = jnp.float32
BF16 = jnp.bfloat16

D_MODEL = 1024
SSD_WIDTH = 1024
SC_WIDTH = 1024
HEAD_DIM = 64
N_HEADS = SSD_WIDTH // HEAD_DIM
N_GROUPS = 2
HEADS_PER_GROUP = N_HEADS // N_GROUPS
GROUP_WIDTH = SSD_WIDTH // N_GROUPS
D_STATE = 128
SSD_CONV_W = 4
SSD_CONV_DIM = SSD_WIDTH + 2 * N_GROUPS * D_STATE
SC_CONV_W = 3
CHUNK = 256
NORM_EPS = 1e-6

LANES = 128
SUBLANES = 8

OFF_Z = 0
OFF_XBC = OFF_Z + SSD_WIDTH
OFF_DT = OFF_XBC + SSD_CONV_DIM
OFF_SCB = OFF_DT + N_HEADS
OFF_SCC = OFF_SCB + SC_WIDTH
OFF_SCH = OFF_SCC + SC_WIDTH
D_IN_PROJ = OFF_SCH + SC_WIDTH
PROJ_BLOCK = 512

VMEM_LIMIT_BYTES = 56 * 1024 * 1024


def _rmsnorm(x, w):
    ms = jnp.mean(x * x, axis=-1, keepdims=True)
    return x * lax.rsqrt(ms + NORM_EPS) * w


def _silu(x):
    return x * (0.5 * jnp.tanh(0.5 * x) + 0.5)


def _softplus(x):
    return jnp.maximum(x, 0.0) + jnp.log1p(jnp.exp(-jnp.abs(x)))


def _dot(a, b):
    return jnp.dot(a, b, preferred_element_type=F32)


def _dot_nt(a, b):
    return lax.dot_general(a, b, (((1,), (1,)), ((), ())), preferred_element_type=F32)


def _split3(x):
    hi = x.astype(BF16)
    r1 = x - hi.astype(F32)
    mid = r1.astype(BF16)
    lo = (r1 - mid.astype(F32)).astype(BF16)
    return hi, mid, lo


def _expand_heads(v, rows):
    lane = lax.broadcasted_iota(jnp.int32, (rows, LANES), 1)
    first_half = lane < HEAD_DIM
    blocks = []
    for q in range(N_HEADS // 2):
        c0 = jnp.broadcast_to(v[:, 2 * q:2 * q + 1], (rows, LANES))
        c1 = jnp.broadcast_to(v[:, 2 * q + 1:2 * q + 2], (rows, LANES))
        blocks.append(jnp.where(first_half, c0, c1))
    return jnp.concatenate(blocks, axis=1)


def _group_rmsnorm(y, w):
    outs = []
    for g in range(N_GROUPS):
        sl = slice(g * GROUP_WIDTH, (g + 1) * GROUP_WIDTH)
        outs.append(_rmsnorm(y[:, sl], w[:, sl]))
    return jnp.concatenate(outs, axis=1)


def _shift_rows(x, prev_tile, j):
    rolled = pltpu.roll(x, j, 0)
    row = lax.broadcasted_iota(jnp.int32, prev_tile.shape, 0)
    head = jnp.where(row < j, pltpu.roll(prev_tile, j, 0), rolled[0:SUBLANES])
    return jnp.concatenate([head, rolled[SUBLANES:]], axis=0)


def _causal_conv(u, prev_tile, w):
    k = w.shape[0]
    acc = u * w[k - 1:k, :]
    for j in range(1, k):
        acc = acc + _shift_rows(u, prev_tile, j) * w[k - 1 - j:k - j, :]
    return acc


WEIGHT_LOAD_STEPS = 8


def _load_weight_rows(step, w_ref, w_bf_ref):
    rows = w_ref.shape[0]
    r0 = pl.multiple_of(step * rows, rows)
    w_bf_ref[pl.ds(r0, rows), :] = w_ref[...].astype(BF16)


def _weight_block_rows(rows, steps):
    tile = 2 * SUBLANES
    return -(-rows // (steps * tile)) * tile


def _weight_rows_spec(w, steps):
    _, rows, cols = w.stacked.shape
    return pl.BlockSpec((None, _weight_block_rows(rows, steps), cols),
                        lambda i: (w.layer, jnp.minimum(i, steps - 1), 0))


def _weight_scratch(w, steps):
    rows, cols = w.shape
    return pltpu.VMEM((steps * _weight_block_rows(rows, steps), cols), BF16)


class _LayerWeight:
    def __init__(self, stacked, layer):
        self.stacked, self.layer = stacked, layer
        self.shape = stacked.shape[1:]


def _ffn_kernel(*refs, final_norm):
    if final_norm:
        (xp_ref, xs_ref, nw_ref, wg_ref, wu_ref, wd_ref, fnw_ref, op_ref, os_ref,
         wg_bf, wu_bf, wd_bf) = refs
    else:
        xp_ref, xs_ref, nw_ref, wg_ref, wu_ref, wd_ref, op_ref, os_ref, wg_bf, wu_bf, wd_bf = refs

    def half_step(x_ref, o_ref):
        x = x_ref[...]
        xn = _rmsnorm(x, nw_ref[...]).astype(BF16)
        g = _dot(xn, wg_bf[...])
        u = _dot(xn, wu_bf[...])
        hmid = (_silu(g) * u).astype(BF16)
        y = x + 0.5 * _dot(hmid, wd_bf[...])
        if final_norm:
            y = _rmsnorm(y, fnw_ref[...])
        o_ref[...] = y

    i = pl.program_id(0)
    sample_step = pl.num_programs(0) - 1

    @pl.when(i < WEIGHT_LOAD_STEPS)
    def _():
        for w_ref, w_bf in ((wg_ref, wg_bf), (wu_ref, wu_bf), (wd_ref, wd_bf)):
            _load_weight_rows(i, w_ref, w_bf)

    @pl.when((i >= WEIGHT_LOAD_STEPS) & (i < sample_step))
    def _():
        half_step(xp_ref, op_ref)

    @pl.when(i == sample_step)
    def _():
        half_step(xs_ref, os_ref)


def _resident(shape):
    return pl.BlockSpec(shape, lambda *_: (0,) * len(shape), pipeline_mode=pl.Buffered(1))


def _ffn(xp, xs, nw, wg, wu, wd, fnw=None, *, tile_m):
    m, d = xp.shape
    ms = xs.shape[0]
    dff = wg.shape[1]
    assert m % tile_m == 0
    n = m // tile_m
    final_norm = fnw is not None
    ws = WEIGHT_LOAD_STEPS
    prompt_tile = pl.BlockSpec((tile_m, d), lambda i: (jnp.clip(i - ws, 0, n - 1), 0))
    sample_tile = pl.BlockSpec((ms, d), lambda i: (0, 0))
    in_specs = [prompt_tile, sample_tile, _resident((1, d)),
                _weight_rows_spec(wg, ws), _weight_rows_spec(wu, ws), _weight_rows_spec(wd, ws)]
    args = [xp, xs, nw, wg.stacked, wu.stacked, wd.stacked]
    if final_norm:
        in_specs.append(_resident((1, d)))
        args.append(fnw)
    return pl.pallas_call(
        functools.partial(_ffn_kernel, final_norm=final_norm),
        out_shape=(jax.ShapeDtypeStruct((m, d), F32), jax.ShapeDtypeStruct((ms, d), F32)),
        grid=(ws + n + 1,),
        in_specs=in_specs,
        out_specs=(prompt_tile, sample_tile),
        scratch_shapes=[_weight_scratch(wg, ws), _weight_scratch(wu, ws), _weight_scratch(wd, ws)],
        compiler_params=pltpu.CompilerParams(
            dimension_semantics=("arbitrary",), vmem_limit_bytes=VMEM_LIMIT_BYTES),
        name="ffn_final" if final_norm else "ffn",
    )(*args)


def _in_proj(h, wint_ref, col0, width):
    return _dot_nt(h, wint_ref[col0:col0 + width, :])


def _dt_proj(h, wint_ref):
    raw = _in_proj(h, wint_ref, OFF_DT, LANES)
    lane = lax.broadcasted_iota(jnp.int32, raw.shape, 1)
    return jnp.where(lane < N_HEADS, raw, 0.0)


def _mixer_kernel(chunks_per_seq, x_ref, nw_ref, wint_ref, cw_ref, cb_ref, dtb_ref, alog_ref,
                  dskip_ref, snw_ref, scw_ref, wout_ref,
                  o_ref, sfin_ref, cfin_ref, scfin_ref,
                  st_scr, hist_scr, uhist_scr, wint_bf, wout_bf):
    i = pl.program_id(0)

    @pl.when(i < WEIGHT_LOAD_STEPS)
    def _():
        _load_weight_rows(i, wint_ref, wint_bf)
        _load_weight_rows(i, wout_ref, wout_bf)

    @pl.when(i >= WEIGHT_LOAD_STEPS)
    def _():
        c = lax.rem(i - WEIGHT_LOAD_STEPS, chunks_per_seq)
        _mixer_chunk(c == 0, c == chunks_per_seq - 1, x_ref, nw_ref, wint_bf, cw_ref,
                     cb_ref, dtb_ref, alog_ref, dskip_ref, snw_ref, scw_ref, wout_bf,
                     o_ref, sfin_ref, cfin_ref, scfin_ref, st_scr, hist_scr, uhist_scr)


def _mixer_chunk(first, last, x_ref, nw_ref, wint_ref, cw_ref, cb_ref, dtb_ref, alog_ref,
                 dskip_ref, snw_ref, scw_ref, wout_ref,
                 o_ref, sfin_ref, cfin_ref, scfin_ref,
                 st_scr, hist_scr, uhist_scr):
    L = CHUNK

    @pl.when(first)
    def _():
        st_scr[...] = jnp.zeros_like(st_scr)
        hist_scr[...] = jnp.zeros_like(hist_scr)
        uhist_scr[...] = jnp.zeros_like(uhist_scr)

    x = x_ref[0]
    h = _rmsnorm(x, nw_ref[...]).astype(BF16)
    dt_raw = _dt_proj(h, wint_ref)

    xbc_c_blocks = []
    for b0 in range(0, SSD_CONV_DIM, PROJ_BLOCK):
        cols = slice(b0, b0 + PROJ_BLOCK)
        xbc = _in_proj(h, wint_ref, OFF_XBC + b0, PROJ_BLOCK)
        acc = _causal_conv(xbc, hist_scr[:, cols], cw_ref[:, cols])
        hist_scr[:, cols] = xbc[L - SUBLANES:L, :]
        xbc_c_blocks.append(_silu(acc + cb_ref[:, cols]))
    cfin_ref[0] = hist_scr[SUBLANES - (SSD_CONV_W - 1):SUBLANES, :]
    xbc_c = jnp.concatenate(xbc_c_blocks, axis=1)
    xs = xbc_c[:, 0:SSD_WIDTH]
    b_all = xbc_c[:, SSD_WIDTH:SSD_WIDTH + N_GROUPS * D_STATE]
    c_all = xbc_c[:, SSD_WIDTH + N_GROUPS * D_STATE:SSD_CONV_DIM]

    y_sc_blocks = []
    for b0 in range(0, SC_WIDTH, PROJ_BLOCK):
        cols = slice(b0, b0 + PROJ_BLOCK)
        scc = _in_proj(h, wint_ref, OFF_SCC + b0, PROJ_BLOCK)
        sch = _in_proj(h, wint_ref, OFF_SCH + b0, PROJ_BLOCK)
        u = scc * sch
        v = _causal_conv(u, uhist_scr[:, cols], scw_ref[:, cols])
        uhist_scr[:, cols] = u[L - SUBLANES:L, :]
        scb = _in_proj(h, wint_ref, OFF_SCB + b0, PROJ_BLOCK)
        y_sc_blocks.append((scb * v).astype(BF16))
    scfin_ref[0] = uhist_scr[SUBLANES - (SC_CONV_W - 1):SUBLANES, :]
    y_sc = jnp.concatenate(y_sc_blocks, axis=1)

    z_gate = jnp.concatenate(
        [_silu(_in_proj(h, wint_ref, OFF_Z + b0, PROJ_BLOCK)) for b0 in range(0, SSD_WIDTH, PROJ_BLOCK)],
        axis=1)

    dt = _softplus(dt_raw + dtb_ref[...])
    a = dt * (-jnp.exp(alog_ref[...]))
    row = lax.broadcasted_iota(jnp.int32, (L, L), 0)
    col = lax.broadcasted_iota(jnp.int32, (L, L), 1)
    causal = row >= col
    tril = jnp.where(causal, 1.0, 0.0).astype(BF16)
    a_hi, a_mid, a_lo = _split3(a)
    acum = _dot(tril, a_hi) + _dot(tril, a_mid) + _dot(tril, a_lo)
    acum_t = acum.T

    acum_e = _expand_heads(acum, L)
    dt_e = _expand_heads(dt, L)
    exp_acum_e = jnp.exp(acum_e)
    decay_end_e = jnp.exp(acum_e[L - 1:L, :] - acum_e)
    xdt = xs * dt_e
    xdt_b = xdt.astype(BF16)
    xdecay_b = (xdt * decay_end_e).astype(BF16)

    lane = lax.broadcasted_iota(jnp.int32, (L, LANES), 1)
    first_half = lane < HEAD_DIM
    y_diag_blocks = []
    y_off_blocks = []
    for g in range(N_GROUPS):
        b_g = b_all[:, g * D_STATE:(g + 1) * D_STATE]
        c_g = c_all[:, g * D_STATE:(g + 1) * D_STATE]
        b_gb = b_g.astype(BF16)
        c_gb = c_g.astype(BF16)
        cb = _dot_nt(c_gb, b_gb)
        for q in range(g * HEADS_PER_GROUP // 2, (g + 1) * HEADS_PER_GROUP // 2):
            ms = []
            for hh in (2 * q, 2 * q + 1):
                seg = acum[:, hh:hh + 1] - acum_t[hh:hh + 1, :]
                decay = jnp.exp(jnp.where(causal, seg, -jnp.inf))
                ms.append((cb * decay).astype(BF16))
            m_cat = jnp.concatenate(ms, axis=1)
            x2 = xdt_b[:, q * LANES:(q + 1) * LANES]
            zero = jnp.zeros_like(x2)
            rhs = jnp.concatenate([jnp.where(first_half, x2, zero),
                                   jnp.where(first_half, zero, x2)], axis=0)
            y_diag_blocks.append(_dot(m_cat, rhs))
        gs = slice(g * GROUP_WIDTH, (g + 1) * GROUP_WIDTH)
        s_enter = st_scr[:, gs]
        y_off_blocks.append(_dot(c_gb, s_enter.astype(BF16)))
        new_states = _dot(b_g.T.astype(BF16), xdecay_b[:, gs])
        st_scr[:, gs] = s_enter * exp_acum_e[L - 1:L, gs] + new_states
    y_diag = jnp.concatenate(y_diag_blocks, axis=1)
    y_off = jnp.concatenate(y_off_blocks, axis=1) * exp_acum_e

    y = (y_diag + y_off + dskip_ref[...] * xs) * z_gate
    y_ssd = _group_rmsnorm(y, snw_ref[...]).astype(BF16)
    mixed = jnp.concatenate([y_ssd, y_sc], axis=1)
    o_ref[0] = x + _dot(mixed, wout_ref[...])

    @pl.when(last)
    def _():
        sfin_ref[0] = st_scr[...].T


def _mixer_prompt(x, nw, wint, cw, cb, dtb, alog, dskip, snw, scw, wout):
    nb, seq, d = x.shape
    assert seq % CHUNK == 0
    nc = seq // CHUNK
    ws = WEIGHT_LOAD_STEPS

    def chunk_map(i):
        t = jnp.maximum(i - ws, 0)
        return (t // nc, t % nc, 0)

    def seq_map(i):
        return (jnp.maximum(i - ws, 0) // nc, 0, 0)

    out_shape = (
        jax.ShapeDtypeStruct((nb, seq, d), F32),
        jax.ShapeDtypeStruct((nb, SSD_WIDTH, D_STATE), F32),
        jax.ShapeDtypeStruct((nb, SSD_CONV_W - 1, SSD_CONV_DIM), F32),
        jax.ShapeDtypeStruct((nb, SC_CONV_W - 1, SC_WIDTH), F32),
    )
    return pl.pallas_call(
        functools.partial(_mixer_kernel, nc),
        out_shape=out_shape,
        grid=(ws + nb * nc,),
        in_specs=[pl.BlockSpec((1, CHUNK, d), chunk_map),
                  _resident((1, d)), _weight_rows_spec(wint, ws),
                  _resident((SSD_CONV_W, SSD_CONV_DIM)), _resident((1, SSD_CONV_DIM)),
                  _resident((1, LANES)), _resident((1, LANES)), _resident((1, SSD_WIDTH)),
                  _resident((1, SSD_WIDTH)), _resident((SC_CONV_W, SC_WIDTH)),
                  _weight_rows_spec(wout, ws)],
        out_specs=(pl.BlockSpec((1, CHUNK, d), chunk_map),
                   pl.BlockSpec((1, SSD_WIDTH, D_STATE), seq_map),
                   pl.BlockSpec((1, SSD_CONV_W - 1, SSD_CONV_DIM), seq_map),
                   pl.BlockSpec((1, SC_CONV_W - 1, SC_WIDTH), seq_map)),
        scratch_shapes=[pltpu.VMEM((D_STATE, SSD_WIDTH), F32),
                        pltpu.VMEM((SUBLANES, SSD_CONV_DIM), F32),
                        pltpu.VMEM((SUBLANES, SC_WIDTH), F32),
                        _weight_scratch(wint, ws), _weight_scratch(wout, ws)],
        compiler_params=pltpu.CompilerParams(
            dimension_semantics=("arbitrary",), vmem_limit_bytes=VMEM_LIMIT_BYTES),
        name="mixer_prompt",
    )(x, nw, wint.stacked, cw, cb, dtb, alog, dskip, snw, scw, wout.stacked)


def _sample_pre_kernel(x_ref, nw_ref, wint_ref, cw_ref, cb_ref, dtb_ref, alog_ref,
                       scw_ref, cst_ref, scst_ref,
                       z_ref, xs_ref, b_ref, c_ref, xdt_t_ref, da_ref, ysc_ref, cnew_ref, scnew_ref,
                       wint_bf):
    nb = x_ref.shape[0]
    x = x_ref[...]
    h = _rmsnorm(x, nw_ref[...]).astype(BF16)
    wint_bf[...] = wint_ref[...].astype(BF16)
    dt_raw = _dt_proj(h, wint_bf)
    xbc = _in_proj(h, wint_bf, OFF_XBC, SSD_CONV_DIM)
    z_ref[...] = _in_proj(h, wint_bf, OFF_Z, SSD_WIDTH)
    scb = _in_proj(h, wint_bf, OFF_SCB, SC_WIDTH)
    scc = _in_proj(h, wint_bf, OFF_SCC, SC_WIDTH)
    sch = _in_proj(h, wint_bf, OFF_SCH, SC_WIDTH)

    cw = cw_ref[...]
    acc = xbc * cw[SSD_CONV_W - 1:SSD_CONV_W, :]
    for k in range(SSD_CONV_W - 1):
        acc = acc + cst_ref[k] * cw[k:k + 1, :]
    for k in range(SSD_CONV_W - 2):
        cnew_ref[k] = cst_ref[k + 1]
    cnew_ref[SSD_CONV_W - 2] = xbc
    xbc_c = _silu(acc + cb_ref[...])
    xs = xbc_c[:, 0:SSD_WIDTH]
    xs_ref[...] = xs
    b_ref[...] = xbc_c[:, SSD_WIDTH:SSD_WIDTH + N_GROUPS * D_STATE]
    c_ref[...] = xbc_c[:, SSD_WIDTH + N_GROUPS * D_STATE:SSD_CONV_DIM]

    dt = _softplus(dt_raw + dtb_ref[...])
    da_ref[...] = jnp.exp(dt * (-jnp.exp(alog_ref[...])))
    xdt_t_ref[...] = (xs * _expand_heads(dt, nb)).T

    u = scc * sch
    scw = scw_ref[...]
    v = u * scw[SC_CONV_W - 1:SC_CONV_W, :]
    for k in range(SC_CONV_W - 1):
        v = v + scst_ref[k] * scw[k:k + 1, :]
    for k in range(SC_CONV_W - 2):
        scnew_ref[k] = scst_ref[k + 1]
    scnew_ref[SC_CONV_W - 2] = u
    ysc_ref[...] = scb * v


def _sample_pre(x, nw, wint, cw, cb, dtb, alog, scw, cst, scst):
    nb, d = x.shape
    f = lambda *s: jax.ShapeDtypeStruct(s, F32)
    out_shape = (f(nb, SSD_WIDTH), f(nb, SSD_WIDTH), f(nb, N_GROUPS * D_STATE), f(nb, N_GROUPS * D_STATE),
                 f(SSD_WIDTH, nb), f(nb, LANES), f(nb, SC_WIDTH),
                 f(SSD_CONV_W - 1, nb, SSD_CONV_DIM), f(SC_CONV_W - 1, nb, SC_WIDTH))
    return pl.pallas_call(
        _sample_pre_kernel,
        out_shape=out_shape,
        scratch_shapes=[pltpu.VMEM(wint.shape, BF16)],
        compiler_params=pltpu.CompilerParams(vmem_limit_bytes=VMEM_LIMIT_BYTES),
        name="sample_pre",
    )(x, nw, wint, cw, cb, dtb, alog, scw, cst, scst)


STATE_BATCH_BLOCK = 8


def _split2(x):
    hi = x.astype(BF16)
    return hi, (x - hi.astype(F32)).astype(BF16)


def _sample_state_kernel(da_ref, s0_ref, xdt_ref, b_ref, c_ref, snew_ref, y_ref):
    bb = STATE_BATCH_BLOCK
    blk = pl.program_id(0)
    xdt_t = xdt_ref[0]
    b_rows = b_ref[...]
    c_hi, c_lo = _split2(c_ref[...])
    row = lax.broadcasted_iota(jnp.int32, (bb, SSD_WIDTH), 0)
    y_blk = jnp.zeros((bb, SSD_WIDTH), F32)
    for i in range(bb):
        xdt_col = xdt_t[:, i:i + 1]
        y_parts = []
        for g in range(N_GROUPS):
            ns = slice(g * D_STATE, (g + 1) * D_STATE)
            b_row = b_rows[i:i + 1, ns]
            heads = []
            for hh in range(g * HEADS_PER_GROUP, (g + 1) * HEADS_PER_GROUP):
                rs = slice(hh * HEAD_DIM, (hh + 1) * HEAD_DIM)
                decay = da_ref[blk * bb + i, hh]
                heads.append(s0_ref[i, rs, :] * decay + xdt_col[rs] * b_row)
            s_new = jnp.concatenate(heads, axis=0)
            snew_ref[i, g * GROUP_WIDTH:(g + 1) * GROUP_WIDTH, :] = s_new
            s_hi, s_lo = _split2(s_new)
            lhs = jnp.concatenate([c_hi[:, ns], c_lo[:, ns]], axis=0)
            r_hi = _dot_nt(lhs, s_hi)
            r_lo = _dot_nt(c_hi[:, ns], s_lo)
            y_parts.append(r_hi[i:i + 1] + r_hi[bb + i:bb + i + 1] + r_lo[i:i + 1])
        y_row = jnp.concatenate(y_parts, axis=1)
        y_blk = jnp.where(row == i, y_row, y_blk)
    y_ref[...] = y_blk


def _sample_state(s0, xdt_t, decay, b_rows, c_rows):
    nb = s0.shape[0]
    bb = STATE_BATCH_BLOCK
    assert nb % bb == 0
    nblk = nb // bb
    xdt_blocks = xdt_t.reshape(SSD_WIDTH, nblk, bb).transpose(1, 0, 2)
    return pl.pallas_call(
        _sample_state_kernel,
        out_shape=(jax.ShapeDtypeStruct((nb, SSD_WIDTH, D_STATE), F32),
                   jax.ShapeDtypeStruct((nb, SSD_WIDTH), F32)),
        grid=(nblk,),
        in_specs=[pl.BlockSpec(memory_space=pltpu.SMEM),
                  pl.BlockSpec((bb, SSD_WIDTH, D_STATE), lambda i: (i, 0, 0)),
                  pl.BlockSpec((1, SSD_WIDTH, bb), lambda i: (i, 0, 0)),
                  pl.BlockSpec((bb, N_GROUPS * D_STATE), lambda i: (i, 0)),
                  pl.BlockSpec((bb, N_GROUPS * D_STATE), lambda i: (i, 0))],
        out_specs=(pl.BlockSpec((bb, SSD_WIDTH, D_STATE), lambda i: (i, 0, 0)),
                   pl.BlockSpec((bb, SSD_WIDTH), lambda i: (i, 0))),
        compiler_params=pltpu.CompilerParams(
            dimension_semantics=("arbitrary",), vmem_limit_bytes=VMEM_LIMIT_BYTES),
        name="sample_state",
    )(decay, s0, xdt_blocks, b_rows, c_rows)


def _sample_post_kernel(x_ref, yraw_ref, xs_ref, z_ref, ysc_ref, dskip_ref, snw_ref, wout_ref, o_ref):
    y = yraw_ref[...] + dskip_ref[...] * xs_ref[...]
    y = y * _silu(z_ref[...])
    y_ssd = _group_rmsnorm(y, snw_ref[...])
    mixed = jnp.concatenate([y_ssd, ysc_ref[...]], axis=1).astype(BF16)
    o_ref[...] = x_ref[...] + _dot(mixed, wout_ref[...].astype(BF16))


def _sample_post(x, yraw, xs, z, ysc, dskip, snw, wout):
    return pl.pallas_call(
        _sample_post_kernel,
        out_shape=jax.ShapeDtypeStruct(x.shape, F32),
        compiler_params=pltpu.CompilerParams(vmem_limit_bytes=VMEM_LIMIT_BYTES),
        name="sample_post",
    )(x, yraw, xs, z, ysc, dskip, snw, wout)


FFN_TILE_M = 512


def _layer_params(i, norm_ffn1_w, ffn1_w_gate, ffn1_w_up, ffn1_w_down, norm_mix_w, w_in_t,
                  ssd_conv_w, ssd_conv_b, dt_bias, a_log, d_skip, ssd_norm_w, sconv_w, w_out,
                  norm_ffn2_w, ffn2_w_gate, ffn2_w_up, ffn2_w_down):
    pad_heads = lambda v: jnp.pad(v, (0, LANES - N_HEADS)).reshape(1, LANES)
    row = lambda v: v.reshape(1, -1)
    return dict(
        ffn1=(row(norm_ffn1_w[i]), _LayerWeight(ffn1_w_gate, i), _LayerWeight(ffn1_w_up, i),
              _LayerWeight(ffn1_w_down, i)),
        ffn2=(row(norm_ffn2_w[i]), _LayerWeight(ffn2_w_gate, i), _LayerWeight(ffn2_w_up, i),
              _LayerWeight(ffn2_w_down, i)),
        nw=row(norm_mix_w[i]), wint=_LayerWeight(w_in_t, i), cw=ssd_conv_w[i], cb=row(ssd_conv_b[i]),
        dtb=pad_heads(dt_bias[i]), alog=pad_heads(a_log[i]),
        dskip=row(jnp.repeat(d_skip[i], HEAD_DIM)), snw=row(ssd_norm_w[i]), scw=sconv_w[i],
        wout=_LayerWeight(w_out, i))


def kernel(x_prompt, x_sample, state_ssm, state_ssd_conv, state_sconv, norm_ffn1_w, ffn1_w_gate, ffn1_w_up, ffn1_w_down, norm_mix_w, w_in, ssd_conv_w, ssd_conv_b, dt_bias, a_log, d_skip, ssd_norm_w, sconv_w, w_out, norm_ffn2_w, ffn2_w_gate, ffn2_w_up, ffn2_w_down, final_norm_w):
    depth = w_in.shape[0]
    bp, seq, d = x_prompt.shape
    bs, dec_seq, _ = x_sample.shape
    assert dec_seq == 1, "sample group is one token per sequence"
    fnw = final_norm_w.reshape(1, d)
    w_in_t = jnp.swapaxes(w_in, 1, 2)

    xp = x_prompt.reshape(bp * seq, d)
    xs = x_sample.reshape(bs, d)
    outs = [[] for _ in range(6)]
    for i in range(depth):
        p = _layer_params(i, norm_ffn1_w, ffn1_w_gate, ffn1_w_up, ffn1_w_down, norm_mix_w, w_in_t,
                          ssd_conv_w, ssd_conv_b, dt_bias, a_log, d_skip, ssd_norm_w, sconv_w,
                          w_out, norm_ffn2_w, ffn2_w_gate, ffn2_w_up, ffn2_w_down)
        last = i == depth - 1
        xp, xs = _ffn(xp, xs, *p["ffn1"], tile_m=FFN_TILE_M)
        xp3, s_p, c_p, sc_p = _mixer_prompt(
            xp.reshape(bp, seq, d), p["nw"], p["wint"], p["cw"], p["cb"], p["dtb"], p["alog"],
            p["dskip"], p["snw"], p["scw"], p["wout"])
        z, xs_conv, b_rows, c_rows, xdt_t, decay, ysc, c_s, sc_s = _sample_pre(
            xs, p["nw"], w_in_t[i], p["cw"], p["cb"], p["dtb"], p["alog"], p["scw"],
            state_ssd_conv[i].transpose(1, 0, 2), state_sconv[i].transpose(1, 0, 2))
        s_s, yraw = _sample_state(state_ssm[i].reshape(bs, SSD_WIDTH, D_STATE), xdt_t, decay, b_rows, c_rows)
        xs = _sample_post(xs, yraw, xs_conv, z, ysc, p["dskip"], p["snw"], w_out[i])
        xp, xs = _ffn(xp3.reshape(bp * seq, d), xs, *p["ffn2"], fnw if last else None, tile_m=FFN_TILE_M)
        for lst, v in zip(outs, (s_p.reshape(bp, N_HEADS, HEAD_DIM, D_STATE), c_p, sc_p,
                                 s_s.reshape(bs, N_HEADS, HEAD_DIM, D_STATE),
                                 c_s.transpose(1, 0, 2), sc_s.transpose(1, 0, 2))):
            lst.append(v)
    return (xp.reshape(bp, seq, d), xs.reshape(bs, dec_seq, d)) + tuple(jnp.stack(l) for l in outs)
```

```python
import functools

import jax
import jax.numpy as jnp
from jax import lax
from jax.experimental import pallas as pl
from jax.experimental.pallas import tpu as pltpu

F32 = jnp.float32
BF16 = jnp.bfloat16

D_MODEL = 1024
SSD_WIDTH = 1024
SC_WIDTH = 1024
HEAD_DIM = 64
N_HEADS = SSD_WIDTH // HEAD_DIM
N_GROUPS = 2
HEADS_PER_GROUP = N_HEADS // N_GROUPS
GROUP_WIDTH = SSD_WIDTH // N_GROUPS
D_STATE = 128
SSD_CONV_W = 4
SSD_CONV_DIM = SSD_WIDTH + 2 * N_GROUPS * D_STATE
SC_CONV_W = 3
CHUNK = 256
NORM_EPS = 1e-6
LOG2_E = 1.4426950408889634

LANES = 128
SUBLANES = 8

OFF_Z = 0
OFF_XBC = OFF_Z + SSD_WIDTH
OFF_DT = OFF_XBC + SSD_CONV_DIM
OFF_SCB = OFF_DT + N_HEADS
OFF_SCC = OFF_SCB + SC_WIDTH
OFF_SCH = OFF_SCC + SC_WIDTH
D_IN_PROJ = OFF_SCH + SC_WIDTH
PROJ_BLOCK = 512

VMEM_LIMIT_BYTES = 56 * 1024 * 1024


def _rmsnorm(x, w):
    ms = jnp.mean(x * x, axis=-1, keepdims=True)
    return x * lax.rsqrt(ms + NORM_EPS) * w


def _silu(x):
    half = 0.5 * x
    return half * jnp.tanh(half) + half


def _softplus(x):
    return jnp.maximum(x, 0.0) + jnp.log1p(jnp.exp(-jnp.abs(x)))


def _dot(a, b):
    return jnp.dot(a, b, preferred_element_type=F32)


def _dot_nt(a, b):
    return lax.dot_general(a, b, (((1,), (1,)), ((), ())), preferred_element_type=F32)


def _split3(x):
    hi = x.astype(BF16)
    r1 = x - hi.astype(F32)
    mid = r1.astype(BF16)
    lo = (r1 - mid.astype(F32)).astype(BF16)
    return hi, mid, lo


def _expand_heads(v, rows):
    lane = lax.broadcasted_iota(jnp.int32, (rows, LANES), 1)
    first_half = lane < HEAD_DIM
    blocks = []
    for q in range(N_HEADS // 2):
        c0 = jnp.broadcast_to(v[:, 2 * q:2 * q + 1], (rows, LANES))
        c1 = jnp.broadcast_to(v[:, 2 * q + 1:2 * q + 2], (rows, LANES))
        blocks.append(jnp.where(first_half, c0, c1))
    return jnp.concatenate(blocks, axis=1)


def _group_rmsnorm(y, w):
    outs = []
    for g in range(N_GROUPS):
        sl = slice(g * GROUP_WIDTH, (g + 1) * GROUP_WIDTH)
        outs.append(_rmsnorm(y[:, sl], w[:, sl]))
    return jnp.concatenate(outs, axis=1)


def _shift_rows(x, prev_tile, j):
    rolled = pltpu.roll(x, j, 0)
    row = lax.broadcasted_iota(jnp.int32, prev_tile.shape, 0)
    head = jnp.where(row < j, pltpu.roll(prev_tile, j, 0), rolled[0:SUBLANES])
    return jnp.concatenate([head, rolled[SUBLANES:]], axis=0)


def _causal_conv(u, prev_tile, w):
    k = w.shape[0]
    acc = u * w[k - 1:k, :]
    for j in range(1, k):
        acc = acc + _shift_rows(u, prev_tile, j) * w[k - 1 - j:k - j, :]
    return acc


WEIGHT_LOAD_STEPS = 8


def _load_weight_rows(step, w_ref, w_bf_ref):
    rows = w_ref.shape[0]
    r0 = pl.multiple_of(step * rows, rows)
    w_bf_ref[pl.ds(r0, rows), :] = w_ref[...].astype(BF16)


def _weight_block_rows(rows, steps):
    tile = 2 * SUBLANES
    return -(-rows // (steps * tile)) * tile


def _weight_rows_spec(w, steps):
    _, rows, cols = w.stacked.shape
    return pl.BlockSpec((None, _weight_block_rows(rows, steps), cols),
                        lambda i: (w.layer, jnp.minimum(i, steps - 1), 0))


def _weight_scratch(w, steps):
    rows, cols = w.shape
    return pltpu.VMEM((steps * _weight_block_rows(rows, steps), cols), BF16)


class _LayerWeight:
    def __init__(self, stacked, layer):
        self.stacked, self.layer = stacked, layer
        self.shape = stacked.shape[1:]


def _ffn_kernel(*refs, final_norm):
    if final_norm:
        (xp_ref, xs_ref, nw_ref, wg_ref, wu_ref, wd_ref, fnw_ref, op_ref, os_ref,
         wg_bf, wu_bf, wd_bf) = refs
    else:
        xp_ref, xs_ref, nw_ref, wg_ref, wu_ref, wd_ref, op_ref, os_ref, wg_bf, wu_bf, wd_bf = refs

    def half_step(x_ref, o_ref):
        x = x_ref[...]
        xn = _rmsnorm(x, nw_ref[...]).astype(BF16)
        g = _dot(xn, wg_bf[...])
        u = _dot(xn, wu_bf[...])
        hmid = (_silu(g) * u).astype(BF16)
        y = x + 0.5 * _dot(hmid, wd_bf[...])
        if final_norm:
            y = _rmsnorm(y, fnw_ref[...])
        o_ref[...] = y

    i = pl.program_id(0)
    sample_step = pl.num_programs(0) - 1

    @pl.when(i < WEIGHT_LOAD_STEPS)
    def _():
        for w_ref, w_bf in ((wg_ref, wg_bf), (wu_ref, wu_bf), (wd_ref, wd_bf)):
            _load_weight_rows(i, w_ref, w_bf)

    @pl.when((i >= WEIGHT_LOAD_STEPS) & (i < sample_step))
    def _():
        half_step(xp_ref, op_ref)

    @pl.when(i == sample_step)
    def _():
        half_step(xs_ref, os_ref)


def _resident(shape):
    return pl.BlockSpec(shape, lambda *_: (0,) * len(shape), pipeline_mode=pl.Buffered(1))


def _ffn(xp, xs, nw, wg, wu, wd, fnw=None, *, tile_m):
    m, d = xp.shape
    ms = xs.shape[0]
    dff = wg.shape[1]
    assert m % tile_m == 0
    n = m // tile_m
    final_norm = fnw is not None
    ws = WEIGHT_LOAD_STEPS
    prompt_tile = pl.BlockSpec((tile_m, d), lambda i: (jnp.clip(i - ws, 0, n - 1), 0))
    sample_tile = pl.BlockSpec((ms, d), lambda i: (0, 0))
    in_specs = [prompt_tile, sample_tile, _resident((1, d)),
                _weight_rows_spec(wg, ws), _weight_rows_spec(wu, ws), _weight_rows_spec(wd, ws)]
    args = [xp, xs, nw, wg.stacked, wu.stacked, wd.stacked]
    if final_norm:
        in_specs.append(_resident((1, d)))
        args.append(fnw)
    return pl.pallas_call(
        functools.partial(_ffn_kernel, final_norm=final_norm),
        out_shape=(jax.ShapeDtypeStruct((m, d), F32), jax.ShapeDtypeStruct((ms, d), F32)),
        grid=(ws + n + 1,),
        in_specs=in_specs,
        out_specs=(prompt_tile, sample_tile),
        scratch_shapes=[_weight_scratch(wg, ws), _weight_scratch(wu, ws), _weight_scratch(wd, ws)],
        compiler_params=pltpu.CompilerParams(
            dimension_semantics=("arbitrary",), vmem_limit_bytes=VMEM_LIMIT_BYTES),
        name="ffn_final" if final_norm else "ffn",
    )(*args)


def _masked_decay_scores(cb, acum_col, acum_row):
    L = cb.shape[0]
    H = L // 2
    r = lax.broadcasted_iota(jnp.int32, (H, H), 0)
    c = lax.broadcasted_iota(jnp.int32, (H, H), 1)
    causal = r >= c

    def block(rows, cols, masked):
        seg = acum_col[rows] - acum_row[:, cols]
        if masked:
            seg = jnp.where(causal, seg, -jnp.inf)
        return (cb[rows, cols] * jnp.exp2(seg)).astype(BF16)

    lo, hi = slice(0, H), slice(H, L)
    top = jnp.concatenate([block(lo, lo, True), jnp.zeros((H, H), BF16)], axis=1)
    bottom = jnp.concatenate([block(hi, lo, False), block(hi, hi, True)], axis=1)
    return jnp.concatenate([top, bottom], axis=0)


def _in_proj(h, wint_ref, col0, width):
    return _dot_nt(h, wint_ref[col0:col0 + width, :])


def _dt_proj(h, wint_ref):
    raw = _in_proj(h, wint_ref, OFF_DT, LANES)
    lane = lax.broadcasted_iota(jnp.int32, raw.shape, 1)
    return jnp.where(lane < N_HEADS, raw, 0.0)


def _mixer_kernel(chunks_per_seq, x_ref, nw_ref, wint_ref, cw_ref, cb_ref, dtb_ref, alog_ref,
                  dskip_ref, snw_ref, scw_ref, wout_ref,
                  o_ref, sfin_ref, cfin_ref, scfin_ref,
                  st_scr, hist_scr, uhist_scr, wint_bf, wout_bf):
    i = pl.program_id(0)

    @pl.when(i < WEIGHT_LOAD_STEPS)
    def _():
        _load_weight_rows(i, wint_ref, wint_bf)
        _load_weight_rows(i, wout_ref, wout_bf)

    @pl.when(i >= WEIGHT_LOAD_STEPS)
    def _():
        c = lax.rem(i - WEIGHT_LOAD_STEPS, chunks_per_seq)
        _mixer_chunk(c == 0, c == chunks_per_seq - 1, x_ref, nw_ref, wint_bf, cw_ref,
                     cb_ref, dtb_ref, alog_ref, dskip_ref, snw_ref, scw_ref, wout_bf,
                     o_ref, sfin_ref, cfin_ref, scfin_ref, st_scr, hist_scr, uhist_scr)


def _mixer_chunk(first, last, x_ref, nw_ref, wint_ref, cw_ref, cb_ref, dtb_ref, alog_ref,
                 dskip_ref, snw_ref, scw_ref, wout_ref,
                 o_ref, sfin_ref, cfin_ref, scfin_ref,
                 st_scr, hist_scr, uhist_scr):
    L = CHUNK

    @pl.when(first)
    def _():
        st_scr[...] = jnp.zeros_like(st_scr)
        hist_scr[...] = jnp.zeros_like(hist_scr)
        uhist_scr[...] = jnp.zeros_like(uhist_scr)

    x = x_ref[0]
    h = _rmsnorm(x, nw_ref[...]).astype(BF16)
    dt_raw = _dt_proj(h, wint_ref)

    dt = _softplus(dt_raw + dtb_ref[...])
    a = dt * (-jnp.exp(alog_ref[...])) * LOG2_E
    row = lax.broadcasted_iota(jnp.int32, (L, L), 0)
    col = lax.broadcasted_iota(jnp.int32, (L, L), 1)
    tril = jnp.where(row >= col, 1.0, 0.0).astype(BF16)
    a_hi, a_mid, a_lo = _split3(a)
    acum = _dot(tril, a_hi) + _dot(tril, a_mid) + _dot(tril, a_lo)
    acum_t = acum.T
    acum_e = _expand_heads(acum, L)
    dt_e = _expand_heads(dt, L)
    exp_acum_e = jnp.exp2(acum_e)
    decay_end_e = jnp.exp2(acum_e[L - 1:L, :] - acum_e)

    def xbc_block(b0, width):
        cols = slice(b0, b0 + width)
        xbc = _in_proj(h, wint_ref, OFF_XBC + b0, width)
        acc = _causal_conv(xbc, hist_scr[:, cols], cw_ref[:, cols])
        hist_scr[:, cols] = xbc[L - SUBLANES:L, :]
        return _silu(acc + cb_ref[:, cols])

    bc = xbc_block(SSD_WIDTH, 2 * N_GROUPS * D_STATE)
    scores = []
    b_groups, c_groups = [], []
    for g in range(N_GROUPS):
        b_g = bc[:, g * D_STATE:(g + 1) * D_STATE]
        c_gb = bc[:, (N_GROUPS + g) * D_STATE:(N_GROUPS + g + 1) * D_STATE].astype(BF16)
        b_groups.append(b_g)
        c_groups.append(c_gb)
        cb = _dot_nt(c_gb, b_g.astype(BF16))
        for q in range(g * HEADS_PER_GROUP // 2, (g + 1) * HEADS_PER_GROUP // 2):
            scores.append(jnp.concatenate(
                [_masked_decay_scores(cb, acum[:, hh:hh + 1], acum_t[hh:hh + 1, :])
                 for hh in (2 * q, 2 * q + 1)], axis=1))

    xs = jnp.concatenate([xbc_block(b0, PROJ_BLOCK) for b0 in range(0, SSD_WIDTH, PROJ_BLOCK)], axis=1)
    cfin_ref[0] = hist_scr[SUBLANES - (SSD_CONV_W - 1):SUBLANES, :]
    xdt = xs * dt_e
    xdt_b = xdt.astype(BF16)
    xdecay_b = (xdt * decay_end_e).astype(BF16)

    lane = lax.broadcasted_iota(jnp.int32, (L, LANES), 1)
    first_half = lane < HEAD_DIM
    y_diag_blocks = []
    for q in range(N_HEADS // 2):
        x2 = xdt_b[:, q * LANES:(q + 1) * LANES]
        zero = jnp.zeros_like(x2)
        rhs = jnp.concatenate([jnp.where(first_half, x2, zero),
                               jnp.where(first_half, zero, x2)], axis=0)
        y_diag_blocks.append(_dot(scores[q], rhs))
    y_off_blocks = []
    for g in range(N_GROUPS):
        gs = slice(g * GROUP_WIDTH, (g + 1) * GROUP_WIDTH)
        s_enter = st_scr[:, gs]
        y_off_blocks.append(_dot(c_groups[g], s_enter.astype(BF16)))
        new_states = _dot(b_groups[g].T.astype(BF16), xdecay_b[:, gs])
        st_scr[:, gs] = s_enter * exp_acum_e[L - 1:L, gs] + new_states
    y_diag = jnp.concatenate(y_diag_blocks, axis=1)
    y_off = jnp.concatenate(y_off_blocks, axis=1) * exp_acum_e

    y_sc_blocks = []
    for b0 in range(0, SC_WIDTH, PROJ_BLOCK):
        cols = slice(b0, b0 + PROJ_BLOCK)
        scc = _in_proj(h, wint_ref, OFF_SCC + b0, PROJ_BLOCK)
        sch = _in_proj(h, wint_ref, OFF_SCH + b0, PROJ_BLOCK)
        u = scc * sch
        v = _causal_conv(u, uhist_scr[:, cols], scw_ref[:, cols])
        uhist_scr[:, cols] = u[L - SUBLANES:L, :]
        scb = _in_proj(h, wint_ref, OFF_SCB + b0, PROJ_BLOCK)
        y_sc_blocks.append((scb * v).astype(BF16))
    scfin_ref[0] = uhist_scr[SUBLANES - (SC_CONV_W - 1):SUBLANES, :]
    y_sc = jnp.concatenate(y_sc_blocks, axis=1)

    z_gate = jnp.concatenate(
        [_silu(_in_proj(h, wint_ref, OFF_Z + b0, PROJ_BLOCK)) for b0 in range(0, SSD_WIDTH, PROJ_BLOCK)],
        axis=1)

    y = (y_diag + y_off + dskip_ref[...] * xs) * z_gate
    y_ssd = _group_rmsnorm(y, snw_ref[...]).astype(BF16)
    mixed = jnp.concatenate([y_ssd, y_sc], axis=1)
    o_ref[0] = x + _dot(mixed, wout_ref[...])

    @pl.when(last)
    def _():
        sfin_ref[0] = st_scr[...].T


def _mixer_prompt(x, nw, wint, cw, cb, dtb, alog, dskip, snw, scw, wout):
    nb, seq, d = x.shape
    assert seq % CHUNK == 0
    nc = seq // CHUNK
    ws = WEIGHT_LOAD_STEPS

    def chunk_map(i):
        t = jnp.maximum(i - ws, 0)
        return (t // nc, t % nc, 0)

    def seq_map(i):
        return (jnp.maximum(i - ws, 0) // nc, 0, 0)

    out_shape = (
        jax.ShapeDtypeStruct((nb, seq, d), F32),
        jax.ShapeDtypeStruct((nb, SSD_WIDTH, D_STATE), F32),
        jax.ShapeDtypeStruct((nb, SSD_CONV_W - 1, SSD_CONV_DIM), F32),
        jax.ShapeDtypeStruct((nb, SC_CONV_W - 1, SC_WIDTH), F32),
    )
    return pl.pallas_call(
        functools.partial(_mixer_kernel, nc),
        out_shape=out_shape,
        grid=(ws + nb * nc,),
        in_specs=[pl.BlockSpec((1, CHUNK, d), chunk_map),
                  _resident((1, d)), _weight_rows_spec(wint, ws),
                  _resident((SSD_CONV_W, SSD_CONV_DIM)), _resident((1, SSD_CONV_DIM)),
                  _resident((1, LANES)), _resident((1, LANES)), _resident((1, SSD_WIDTH)),
                  _resident((1, SSD_WIDTH)), _resident((SC_CONV_W, SC_WIDTH)),
                  _weight_rows_spec(wout, ws)],
        out_specs=(pl.BlockSpec((1, CHUNK, d), chunk_map),
                   pl.BlockSpec((1, SSD_WIDTH, D_STATE), seq_map),
                   pl.BlockSpec((1, SSD_CONV_W - 1, SSD_CONV_DIM), seq_map),
                   pl.BlockSpec((1, SC_CONV_W - 1, SC_WIDTH), seq_map)),
        scratch_shapes=[pltpu.VMEM((D_STATE, SSD_WIDTH), F32),
                        pltpu.VMEM((SUBLANES, SSD_CONV_DIM), F32),
                        pltpu.VMEM((SUBLANES, SC_WIDTH), F32),
                        _weight_scratch(wint, ws), _weight_scratch(wout, ws)],
        compiler_params=pltpu.CompilerParams(
            dimension_semantics=("arbitrary",), vmem_limit_bytes=VMEM_LIMIT_BYTES),
        name="mixer_prompt",
    )(x, nw, wint.stacked, cw, cb, dtb, alog, dskip, snw, scw, wout.stacked)


def _sample_pre_kernel(x_ref, nw_ref, wint_ref, cw_ref, cb_ref, dtb_ref, alog_ref,
                       scw_ref, cst_ref, scst_ref,
                       z_ref, xs_ref, b_ref, c_ref, xdt_blk_ref, da_ref, ysc_ref, cnew_ref, scnew_ref,
                       wint_bf):
    nb = x_ref.shape[0]
    x = x_ref[...]
    h = _rmsnorm(x, nw_ref[...]).astype(BF16)
    wint_bf[...] = wint_ref[...].astype(BF16)
    dt_raw = _dt_proj(h, wint_bf)
    xbc = _in_proj(h, wint_bf, OFF_XBC, SSD_CONV_DIM)
    z_ref[...] = _in_proj(h, wint_bf, OFF_Z, SSD_WIDTH)
    scb = _in_proj(h, wint_bf, OFF_SCB, SC_WIDTH)
    scc = _in_proj(h, wint_bf, OFF_SCC, SC_WIDTH)
    sch = _in_proj(h, wint_bf, OFF_SCH, SC_WIDTH)

    cw = cw_ref[...]
    acc = xbc * cw[SSD_CONV_W - 1:SSD_CONV_W, :]
    for k in range(SSD_CONV_W - 1):
        acc = acc + cst_ref[:, k, :] * cw[k:k + 1, :]
    for k in range(SSD_CONV_W - 2):
        cnew_ref[:, k, :] = cst_ref[:, k + 1, :]
    cnew_ref[:, SSD_CONV_W - 2, :] = xbc
    xbc_c = _silu(acc + cb_ref[...])
    xs = xbc_c[:, 0:SSD_WIDTH]
    xs_ref[...] = xs
    b_ref[...] = xbc_c[:, SSD_WIDTH:SSD_WIDTH + N_GROUPS * D_STATE]
    c_ref[...] = xbc_c[:, SSD_WIDTH + N_GROUPS * D_STATE:SSD_CONV_DIM]

    dt = _softplus(dt_raw + dtb_ref[...])
    da_ref[...] = jnp.exp(dt * (-jnp.exp(alog_ref[...])))
    xdt_t = (xs * _expand_heads(dt, nb)).T
    bb = xdt_blk_ref.shape[2]
    for j in range(nb // bb):
        xdt_blk_ref[j] = xdt_t[:, j * bb:(j + 1) * bb]

    u = scc * sch
    scw = scw_ref[...]
    v = u * scw[SC_CONV_W - 1:SC_CONV_W, :]
    for k in range(SC_CONV_W - 1):
        v = v + scst_ref[:, k, :] * scw[k:k + 1, :]
    for k in range(SC_CONV_W - 2):
        scnew_ref[:, k, :] = scst_ref[:, k + 1, :]
    scnew_ref[:, SC_CONV_W - 2, :] = u
    ysc_ref[...] = scb * v


def _sample_pre(x, nw, wint, cw, cb, dtb, alog, scw, cst, scst):
    nb, d = x.shape
    f = lambda *s: jax.ShapeDtypeStruct(s, F32)
    bb = STATE_BATCH_BLOCK
    assert nb % bb == 0
    out_shape = (f(nb, SSD_WIDTH), f(nb, SSD_WIDTH), f(nb, N_GROUPS * D_STATE), f(nb, N_GROUPS * D_STATE),
                 f(nb // bb, SSD_WIDTH, bb), f(nb, LANES), f(nb, SC_WIDTH),
                 f(nb, SSD_CONV_W - 1, SSD_CONV_DIM), f(nb, SC_CONV_W - 1, SC_WIDTH))
    return pl.pallas_call(
        _sample_pre_kernel,
        out_shape=out_shape,
        scratch_shapes=[pltpu.VMEM(wint.shape, BF16)],
        compiler_params=pltpu.CompilerParams(vmem_limit_bytes=VMEM_LIMIT_BYTES),
        name="sample_pre",
    )(x, nw, wint, cw, cb, dtb, alog, scw, cst, scst)


STATE_BATCH_BLOCK = 8


def _split2(x):
    hi = x.astype(BF16)
    return hi, (x - hi.astype(F32)).astype(BF16)


def _sample_state_kernel(da_ref, s0_ref, xdt_ref, b_ref, c_ref, snew_ref, y_ref):
    bb = STATE_BATCH_BLOCK
    blk = pl.program_id(0)
    xdt_t = xdt_ref[0]
    b_rows = b_ref[...]
    c_hi, c_lo = _split2(c_ref[...])
    row = lax.broadcasted_iota(jnp.int32, (bb, SSD_WIDTH), 0)
    y_blk = jnp.zeros((bb, SSD_WIDTH), F32)
    for i in range(bb):
        xdt_col = xdt_t[:, i:i + 1]
        y_parts = []
        for g in range(N_GROUPS):
            ns = slice(g * D_STATE, (g + 1) * D_STATE)
            b_row = b_rows[i:i + 1, ns]
            heads = []
            for hh in range(g * HEADS_PER_GROUP, (g + 1) * HEADS_PER_GROUP):
                rs = slice(hh * HEAD_DIM, (hh + 1) * HEAD_DIM)
                decay = da_ref[blk * bb + i, hh]
                heads.append(s0_ref[i, rs, :] * decay + xdt_col[rs] * b_row)
            s_new = jnp.concatenate(heads, axis=0)
            snew_ref[i, g * GROUP_WIDTH:(g + 1) * GROUP_WIDTH, :] = s_new
            s_hi, s_lo = _split2(s_new)
            lhs = jnp.concatenate([c_hi[:, ns], c_lo[:, ns]], axis=0)
            r_hi = _dot_nt(lhs, s_hi)
            r_lo = _dot_nt(c_hi[:, ns], s_lo)
            y_parts.append(r_hi[i:i + 1] + r_hi[bb + i:bb + i + 1] + r_lo[i:i + 1])
        y_row = jnp.concatenate(y_parts, axis=1)
        y_blk = jnp.where(row == i, y_row, y_blk)
    y_ref[...] = y_blk


def _sample_state(s0, xdt_blocks, decay, b_rows, c_rows):
    nb = s0.shape[0]
    nblk, _, bb = xdt_blocks.shape
    assert bb == STATE_BATCH_BLOCK and nblk * bb == nb
    return pl.pallas_call(
        _sample_state_kernel,
        out_shape=(jax.ShapeDtypeStruct((nb, SSD_WIDTH, D_STATE), F32),
                   jax.ShapeDtypeStruct((nb, SSD_WIDTH), F32)),
        grid=(nblk,),
        in_specs=[pl.BlockSpec(memory_space=pltpu.SMEM),
                  pl.BlockSpec((bb, SSD_WIDTH, D_STATE), lambda i: (i, 0, 0)),
                  pl.BlockSpec((1, SSD_WIDTH, bb), lambda i: (i, 0, 0)),
                  pl.BlockSpec((bb, N_GROUPS * D_STATE), lambda i: (i, 0)),
                  pl.BlockSpec((bb, N_GROUPS * D_STATE), lambda i: (i, 0))],
        out_specs=(pl.BlockSpec((bb, SSD_WIDTH, D_STATE), lambda i: (i, 0, 0)),
                   pl.BlockSpec((bb, SSD_WIDTH), lambda i: (i, 0))),
        compiler_params=pltpu.CompilerParams(
            dimension_semantics=("arbitrary",), vmem_limit_bytes=VMEM_LIMIT_BYTES),
        name="sample_state",
    )(decay, s0, xdt_blocks, b_rows, c_rows)


def _sample_post_kernel(x_ref, yraw_ref, xs_ref, z_ref, ysc_ref, dskip_ref, snw_ref, wout_ref, o_ref):
    y = yraw_ref[...] + dskip_ref[...] * xs_ref[...]
    y = y * _silu(z_ref[...])
    y_ssd = _group_rmsnorm(y, snw_ref[...])
    mixed = jnp.concatenate([y_ssd, ysc_ref[...]], axis=1).astype(BF16)
    o_ref[...] = x_ref[...] + _dot(mixed, wout_ref[...].astype(BF16))


def _sample_post(x, yraw, xs, z, ysc, dskip, snw, wout):
    return pl.pallas_call(
        _sample_post_kernel,
        out_shape=jax.ShapeDtypeStruct(x.shape, F32),
        compiler_params=pltpu.CompilerParams(vmem_limit_bytes=VMEM_LIMIT_BYTES),
        name="sample_post",
    )(x, yraw, xs, z, ysc, dskip, snw, wout)


FFN_TILE_M = 512


def _layer_params(i, norm_ffn1_w, ffn1_w_gate, ffn1_w_up, ffn1_w_down, norm_mix_w, w_in_t,
                  ssd_conv_w, ssd_conv_b, dt_bias, a_log, d_skip, ssd_norm_w, sconv_w, w_out,
                  norm_ffn2_w, ffn2_w_gate, ffn2_w_up, ffn2_w_down):
    pad_heads = lambda v: jnp.pad(v, (0, LANES - N_HEADS)).reshape(1, LANES)
    row = lambda v: v.reshape(1, -1)
    return dict(
        ffn1=(row(norm_ffn1_w[i]), _LayerWeight(ffn1_w_gate, i), _LayerWeight(ffn1_w_up, i),
              _LayerWeight(ffn1_w_down, i)),
        ffn2=(row(norm_ffn2_w[i]), _LayerWeight(ffn2_w_gate, i), _LayerWeight(ffn2_w_up, i),
              _LayerWeight(ffn2_w_down, i)),
        nw=row(norm_mix_w[i]), wint=_LayerWeight(w_in_t, i), cw=ssd_conv_w[i], cb=row(ssd_conv_b[i]),
        dtb=pad_heads(dt_bias[i]), alog=pad_heads(a_log[i]),
        dskip=row(jnp.repeat(d_skip[i], HEAD_DIM)), snw=row(ssd_norm_w[i]), scw=sconv_w[i],
        wout=_LayerWeight(w_out, i))


def kernel(x_prompt, x_sample, state_ssm, state_ssd_conv, state_sconv, norm_ffn1_w, ffn1_w_gate, ffn1_w_up, ffn1_w_down, norm_mix_w, w_in, ssd_conv_w, ssd_conv_b, dt_bias, a_log, d_skip, ssd_norm_w, sconv_w, w_out, norm_ffn2_w, ffn2_w_gate, ffn2_w_up, ffn2_w_down, final_norm_w):
    depth = w_in.shape[0]
    bp, seq, d = x_prompt.shape
    bs, dec_seq, _ = x_sample.shape
    assert dec_seq == 1, "sample group is one token per sequence"
    fnw = final_norm_w.reshape(1, d)
    w_in_t = jnp.swapaxes(w_in, 1, 2)

    xp = x_prompt.reshape(bp * seq, d)
    xs = x_sample.reshape(bs, d)
    outs = [[] for _ in range(6)]
    for i in range(depth):
        p = _layer_params(i, norm_ffn1_w, ffn1_w_gate, ffn1_w_up, ffn1_w_down, norm_mix_w, w_in_t,
                          ssd_conv_w, ssd_conv_b, dt_bias, a_log, d_skip, ssd_norm_w, sconv_w,
                          w_out, norm_ffn2_w, ffn2_w_gate, ffn2_w_up, ffn2_w_down)
        last = i == depth - 1
        xp, xs = _ffn(xp, xs, *p["ffn1"], tile_m=FFN_TILE_M)
        xp3, s_p, c_p, sc_p = _mixer_prompt(
            xp.reshape(bp, seq, d), p["nw"], p["wint"], p["cw"], p["cb"], p["dtb"], p["alog"],
            p["dskip"], p["snw"], p["scw"], p["wout"])
        z, xs_conv, b_rows, c_rows, xdt_t, decay, ysc, c_s, sc_s = _sample_pre(
            xs, p["nw"], w_in_t[i], p["cw"], p["cb"], p["dtb"], p["alog"], p["scw"],
            state_ssd_conv[i], state_sconv[i])
        s_s, yraw = _sample_state(state_ssm[i].reshape(bs, SSD_WIDTH, D_STATE), xdt_t, decay, b_rows, c_rows)
        xs = _sample_post(xs, yraw, xs_conv, z, ysc, p["dskip"], p["snw"], w_out[i])
        xp, xs = _ffn(xp3.reshape(bp * seq, d), xs, *p["ffn2"], fnw if last else None, tile_m=FFN_TILE_M)
        for lst, v in zip(outs, (s_p.reshape(bp, N_HEADS, HEAD_DIM, D_STATE), c_p, sc_p,
                                 s_s.reshape(bs, N_HEADS, HEAD_DIM, D_STATE),
                                 c_s, sc_s)):
            lst.append(v)
    return (xp.reshape(bp, seq, d), xs.reshape(bs, dec_seq, d)) + tuple(jnp.stack(l) for l in outs)
```

```python
import functools

import jax
import jax.numpy as jnp
from jax import lax
from jax.experimental import pallas as pl
from jax.experimental.pallas import tpu as pltpu

F32 = jnp.float32
BF16 = jnp.bfloat16

D_MODEL = 1024
SSD_WIDTH = 1024
SC_WIDTH = 1024
HEAD_DIM = 64
N_HEADS = SSD_WIDTH // HEAD_DIM
N_GROUPS = 2
HEADS_PER_GROUP = N_HEADS // N_GROUPS
GROUP_WIDTH = SSD_WIDTH // N_GROUPS
D_STATE = 128
SSD_CONV_W = 4
SSD_CONV_DIM = SSD_WIDTH + 2 * N_GROUPS * D_STATE
SC_CONV_W = 3
CHUNK = 256
NORM_EPS = 1e-6

LANES = 128
SUBLANES = 8

OFF_Z = 0
OFF_XBC = OFF_Z + SSD_WIDTH
OFF_DT = OFF_XBC + SSD_CONV_DIM
OFF_SCB = OFF_DT + N_HEADS
OFF_SCC = OFF_SCB + SC_WIDTH
OFF_SCH = OFF_SCC + SC_WIDTH
D_IN_PROJ = OFF_SCH + SC_WIDTH
PROJ_BLOCK = 512

VMEM_LIMIT_BYTES = 56 * 1024 * 1024


def _rmsnorm(x, w):
    ms = jnp.mean(x * x, axis=-1, keepdims=True)
    return x * lax.rsqrt(ms + NORM_EPS) * w


def _silu(x):
    half = 0.5 * x
    return half * jnp.tanh(half) + half


def _silu_mixer(x):
    return x * (0.5 * jnp.tanh(0.5 * x) + 0.5)


def _softplus(x):
    return jnp.maximum(x, 0.0) + jnp.log1p(jnp.exp(-jnp.abs(x)))


def _dot(a, b):
    return jnp.dot(a, b, preferred_element_type=F32)


def _dot_nt(a, b):
    return lax.dot_general(a, b, (((1,), (1,)), ((), ())), preferred_element_type=F32)


def _split3(x):
    hi = x.astype(BF16)
    r1 = x - hi.astype(F32)
    mid = r1.astype(BF16)
    lo = (r1 - mid.astype(F32)).astype(BF16)
    return hi, mid, lo


def _expand_heads(v, rows):
    lane = lax.broadcasted_iota(jnp.int32, (rows, LANES), 1)
    first_half = lane < HEAD_DIM
    blocks = []
    for q in range(N_HEADS // 2):
        c0 = jnp.broadcast_to(v[:, 2 * q:2 * q + 1], (rows, LANES))
        c1 = jnp.broadcast_to(v[:, 2 * q + 1:2 * q + 2], (rows, LANES))
        blocks.append(jnp.where(first_half, c0, c1))
    return jnp.concatenate(blocks, axis=1)


def _group_rmsnorm(y, w):
    outs = []
    for g in range(N_GROUPS):
        sl = slice(g * GROUP_WIDTH, (g + 1) * GROUP_WIDTH)
        outs.append(_rmsnorm(y[:, sl], w[:, sl]))
    return jnp.concatenate(outs, axis=1)


def _shift_rows(x, prev_tile, j):
    rolled = pltpu.roll(x, j, 0)
    row = lax.broadcasted_iota(jnp.int32, prev_tile.shape, 0)
    head = jnp.where(row < j, pltpu.roll(prev_tile, j, 0), rolled[0:SUBLANES])
    return jnp.concatenate([head, rolled[SUBLANES:]], axis=0)


def _causal_conv(u, prev_tile, w):
    k = w.shape[0]
    acc = u * w[k - 1:k, :]
    for j in range(1, k):
        acc = acc + _shift_rows(u, prev_tile, j) * w[k - 1 - j:k - j, :]
    return acc


WEIGHT_LOAD_STEPS = 8


def _load_weight_rows(step, w_ref, w_bf_ref):
    rows = w_ref.shape[0]
    r0 = pl.multiple_of(step * rows, rows)
    w_bf_ref[pl.ds(r0, rows), :] = w_ref[...].astype(BF16)


def _weight_block_rows(rows, steps):
    tile = 2 * SUBLANES
    return -(-rows // (steps * tile)) * tile


def _weight_rows_spec(w, steps):
    _, rows, cols = w.stacked.shape
    return pl.BlockSpec((None, _weight_block_rows(rows, steps), cols),
                        lambda i: (w.layer, jnp.minimum(i, steps - 1), 0))


def _weight_scratch(w, steps):
    rows, cols = w.shape
    return pltpu.VMEM((steps * _weight_block_rows(rows, steps), cols), BF16)


class _LayerWeight:
    def __init__(self, stacked, layer):
        self.stacked, self.layer = stacked, layer
        self.shape = stacked.shape[1:]


def _ffn_kernel(*refs, final_norm):
    if final_norm:
        (xp_ref, xs_ref, nw_ref, wg_ref, wu_ref, wd_ref, fnw_ref, op_ref, os_ref,
         wg_bf, wu_bf, wd_bf) = refs
    else:
        xp_ref, xs_ref, nw_ref, wg_ref, wu_ref, wd_ref, op_ref, os_ref, wg_bf, wu_bf, wd_bf = refs

    def half_step(x_ref, o_ref):
        x = x_ref[...]
        xn = _rmsnorm(x, nw_ref[...]).astype(BF16)
        g = _dot(xn, wg_bf[...])
        u = _dot(xn, wu_bf[...])
        hmid = (_silu(g) * u).astype(BF16)
        y = x + 0.5 * _dot(hmid, wd_bf[...])
        if final_norm:
            y = _rmsnorm(y, fnw_ref[...])
        o_ref[...] = y

    i = pl.program_id(0)
    sample_step = pl.num_programs(0) - 1

    @pl.when(i < WEIGHT_LOAD_STEPS)
    def _():
        for w_ref, w_bf in ((wg_ref, wg_bf), (wu_ref, wu_bf), (wd_ref, wd_bf)):
            _load_weight_rows(i, w_ref, w_bf)

    @pl.when((i >= WEIGHT_LOAD_STEPS) & (i < sample_step))
    def _():
        half_step(xp_ref, op_ref)

    @pl.when(i == sample_step)
    def _():
        half_step(xs_ref, os_ref)


def _resident(shape):
    return pl.BlockSpec(shape, lambda *_: (0,) * len(shape), pipeline_mode=pl.Buffered(1))


def _ffn(xp, xs, nw, wg, wu, wd, fnw=None, *, tile_m):
    m, d = xp.shape
    ms = xs.shape[0]
    dff = wg.shape[1]
    assert m % tile_m == 0
    n = m // tile_m
    final_norm = fnw is not None
    ws = WEIGHT_LOAD_STEPS
    prompt_tile = pl.BlockSpec((tile_m, d), lambda i: (jnp.clip(i - ws, 0, n - 1), 0))
    sample_tile = pl.BlockSpec((ms, d), lambda i: (0, 0))
    in_specs = [prompt_tile, sample_tile, _resident((1, d)),
                _weight_rows_spec(wg, ws), _weight_rows_spec(wu, ws), _weight_rows_spec(wd, ws)]
    args = [xp, xs, nw, wg.stacked, wu.stacked, wd.stacked]
    if final_norm:
        in_specs.append(_resident((1, d)))
        args.append(fnw)
    return pl.pallas_call(
        functools.partial(_ffn_kernel, final_norm=final_norm),
        out_shape=(jax.ShapeDtypeStruct((m, d), F32), jax.ShapeDtypeStruct((ms, d), F32)),
        grid=(ws + n + 1,),
        in_specs=in_specs,
        out_specs=(prompt_tile, sample_tile),
        scratch_shapes=[_weight_scratch(wg, ws), _weight_scratch(wu, ws), _weight_scratch(wd, ws)],
        compiler_params=pltpu.CompilerParams(
            dimension_semantics=("arbitrary",), vmem_limit_bytes=VMEM_LIMIT_BYTES),
        name="ffn_final" if final_norm else "ffn",
    )(*args)


def _in_proj(h, wint_ref, col0, width):
    return _dot_nt(h, wint_ref[col0:col0 + width, :])


def _dt_proj(h, wint_ref):
    raw = _in_proj(h, wint_ref, OFF_DT, LANES)
    lane = lax.broadcasted_iota(jnp.int32, raw.shape, 1)
    return jnp.where(lane < N_HEADS, raw, 0.0)


def _mixer_kernel(chunks_per_seq, x_ref, nw_ref, wint_ref, cw_ref, cb_ref, dtb_ref, alog_ref,
                  dskip_ref, snw_ref, scw_ref, wout_ref,
                  o_ref, sfin_ref, cfin_ref, scfin_ref,
                  st_scr, hist_scr, uhist_scr, wint_bf, wout_bf):
    i = pl.program_id(0)

    @pl.when(i < WEIGHT_LOAD_STEPS)
    def _():
        _load_weight_rows(i, wint_ref, wint_bf)
        _load_weight_rows(i, wout_ref, wout_bf)

    @pl.when(i >= WEIGHT_LOAD_STEPS)
    def _():
        c = lax.rem(i - WEIGHT_LOAD_STEPS, chunks_per_seq)
        _mixer_chunk(c == 0, c == chunks_per_seq - 1, x_ref, nw_ref, wint_bf, cw_ref,
                     cb_ref, dtb_ref, alog_ref, dskip_ref, snw_ref, scw_ref, wout_bf,
                     o_ref, sfin_ref, cfin_ref, scfin_ref, st_scr, hist_scr, uhist_scr)


def _mixer_chunk(first, last, x_ref, nw_ref, wint_ref, cw_ref, cb_ref, dtb_ref, alog_ref,
                 dskip_ref, snw_ref, scw_ref, wout_ref,
                 o_ref, sfin_ref, cfin_ref, scfin_ref,
                 st_scr, hist_scr, uhist_scr):
    L = CHUNK

    @pl.when(first)
    def _():
        st_scr[...] = jnp.zeros_like(st_scr)
        hist_scr[...] = jnp.zeros_like(hist_scr)
        uhist_scr[...] = jnp.zeros_like(uhist_scr)

    x = x_ref[0]
    h = _rmsnorm(x, nw_ref[...]).astype(BF16)
    dt_raw = _dt_proj(h, wint_ref)

    xbc_c_blocks = []
    for b0 in range(0, SSD_CONV_DIM, PROJ_BLOCK):
        cols = slice(b0, b0 + PROJ_BLOCK)
        xbc = _in_proj(h, wint_ref, OFF_XBC + b0, PROJ_BLOCK)
        acc = _causal_conv(xbc, hist_scr[:, cols], cw_ref[:, cols])
        hist_scr[:, cols] = xbc[L - SUBLANES:L, :]
        xbc_c_blocks.append(_silu_mixer(acc + cb_ref[:, cols]))
    cfin_ref[0] = hist_scr[SUBLANES - (SSD_CONV_W - 1):SUBLANES, :]
    xbc_c = jnp.concatenate(xbc_c_blocks, axis=1)
    xs = xbc_c[:, 0:SSD_WIDTH]
    b_all = xbc_c[:, SSD_WIDTH:SSD_WIDTH + N_GROUPS * D_STATE]
    c_all = xbc_c[:, SSD_WIDTH + N_GROUPS * D_STATE:SSD_CONV_DIM]

    y_sc_blocks = []
    for b0 in range(0, SC_WIDTH, PROJ_BLOCK):
        cols = slice(b0, b0 + PROJ_BLOCK)
        scc = _in_proj(h, wint_ref, OFF_SCC + b0, PROJ_BLOCK)
        sch = _in_proj(h, wint_ref, OFF_SCH + b0, PROJ_BLOCK)
        u = scc * sch
        v = _causal_conv(u, uhist_scr[:, cols], scw_ref[:, cols])
        uhist_scr[:, cols] = u[L - SUBLANES:L, :]
        scb = _in_proj(h, wint_ref, OFF_SCB + b0, PROJ_BLOCK)
        y_sc_blocks.append((scb * v).astype(BF16))
    scfin_ref[0] = uhist_scr[SUBLANES - (SC_CONV_W - 1):SUBLANES, :]
    y_sc = jnp.concatenate(y_sc_blocks, axis=1)

    z_gate = jnp.concatenate(
        [_silu_mixer(_in_proj(h, wint_ref, OFF_Z + b0, PROJ_BLOCK))
         for b0 in range(0, SSD_WIDTH, PROJ_BLOCK)], axis=1)

    dt = _softplus(dt_raw + dtb_ref[...])
    a = dt * (-jnp.exp(alog_ref[...]))
    row = lax.broadcasted_iota(jnp.int32, (L, L), 0)
    col = lax.broadcasted_iota(jnp.int32, (L, L), 1)
    causal = row >= col
    tril = jnp.where(causal, 1.0, 0.0).astype(BF16)
    a_hi, a_mid, a_lo = _split3(a)
    acum = _dot(tril, a_hi) + _dot(tril, a_mid) + _dot(tril, a_lo)
    acum_t = acum.T

    acum_e = _expand_heads(acum, L)
    dt_e = _expand_heads(dt, L)
    exp_acum_e = jnp.exp(acum_e)
    decay_end_e = jnp.exp(acum_e[L - 1:L, :] - acum_e)
    xdt = xs * dt_e
    xdt_b = xdt.astype(BF16)
    xdecay_b = (xdt * decay_end_e).astype(BF16)

    lane = lax.broadcasted_iota(jnp.int32, (L, LANES), 1)
    first_half = lane < HEAD_DIM
    y_diag_blocks = []
    y_off_blocks = []
    for g in range(N_GROUPS):
        b_g = b_all[:, g * D_STATE:(g + 1) * D_STATE]
        c_g = c_all[:, g * D_STATE:(g + 1) * D_STATE]
        b_gb = b_g.astype(BF16)
        c_gb = c_g.astype(BF16)
        cb = _dot_nt(c_gb, b_gb)
        for q in range(g * HEADS_PER_GROUP // 2, (g + 1) * HEADS_PER_GROUP // 2):
            ms = []
            for hh in (2 * q, 2 * q + 1):
                seg = acum[:, hh:hh + 1] - acum_t[hh:hh + 1, :]
                decay = jnp.exp(jnp.where(causal, seg, -jnp.inf))
                ms.append((cb * decay).astype(BF16))
            m_cat = jnp.concatenate(ms, axis=1)
            x2 = xdt_b[:, q * LANES:(q + 1) * LANES]
            zero = jnp.zeros_like(x2)
            rhs = jnp.concatenate([jnp.where(first_half, x2, zero),
                                   jnp.where(first_half, zero, x2)], axis=0)
            y_diag_blocks.append(_dot(m_cat, rhs))
        gs = slice(g * GROUP_WIDTH, (g + 1) * GROUP_WIDTH)
        s_enter = st_scr[:, gs]
        y_off_blocks.append(_dot(c_gb, s_enter.astype(BF16)))
        new_states = _dot(b_g.T.astype(BF16), xdecay_b[:, gs])
        st_scr[:, gs] = s_enter * exp_acum_e[L - 1:L, gs] + new_states
    y_diag = jnp.concatenate(y_diag_blocks, axis=1)
    y_off = jnp.concatenate(y_off_blocks, axis=1) * exp_acum_e

    y = (y_diag + y_off + dskip_ref[...] * xs) * z_gate
    y_ssd = _group_rmsnorm(y, snw_ref[...]).astype(BF16)
    mixed = jnp.concatenate([y_ssd, y_sc], axis=1)
    o_ref[0] = x + _dot(mixed, wout_ref[...])

    @pl.when(last)
    def _():
        sfin_ref[0] = st_scr[...].T


def _mixer_prompt(x, nw, wint, cw, cb, dtb, alog, dskip, snw, scw, wout):
    nb, seq, d = x.shape
    assert seq % CHUNK == 0
    nc = seq // CHUNK
    ws = WEIGHT_LOAD_STEPS

    def chunk_map(i):
        t = jnp.maximum(i - ws, 0)
        return (t // nc, t % nc, 0)

    def seq_map(i):
        return (jnp.maximum(i - ws, 0) // nc, 0, 0)

    out_shape = (
        jax.ShapeDtypeStruct((nb, seq, d), F32),
        jax.ShapeDtypeStruct((nb, SSD_WIDTH, D_STATE), F32),
        jax.ShapeDtypeStruct((nb, SSD_CONV_W - 1, SSD_CONV_DIM), F32),
        jax.ShapeDtypeStruct((nb, SC_CONV_W - 1, SC_WIDTH), F32),
    )
    return pl.pallas_call(
        functools.partial(_mixer_kernel, nc),
        out_shape=out_shape,
        grid=(ws + nb * nc,),
        in_specs=[pl.BlockSpec((1, CHUNK, d), chunk_map),
                  _resident((1, d)), _weight_rows_spec(wint, ws),
                  _resident((SSD_CONV_W, SSD_CONV_DIM)), _resident((1, SSD_CONV_DIM)),
                  _resident((1, LANES)), _resident((1, LANES)), _resident((1, SSD_WIDTH)),
                  _resident((1, SSD_WIDTH)), _resident((SC_CONV_W, SC_WIDTH)),
                  _weight_rows_spec(wout, ws)],
        out_specs=(pl.BlockSpec((1, CHUNK, d), chunk_map),
                   pl.BlockSpec((1, SSD_WIDTH, D_STATE), seq_map),
                   pl.BlockSpec((1, SSD_CONV_W - 1, SSD_CONV_DIM), seq_map),
                   pl.BlockSpec((1, SC_CONV_W - 1, SC_WIDTH), seq_map)),
        scratch_shapes=[pltpu.VMEM((D_STATE, SSD_WIDTH), F32),
                        pltpu.VMEM((SUBLANES, SSD_CONV_DIM), F32),
                        pltpu.VMEM((SUBLANES, SC_WIDTH), F32),
                        _weight_scratch(wint, ws), _weight_scratch(wout, ws)],
        compiler_params=pltpu.CompilerParams(
            dimension_semantics=("arbitrary",), vmem_limit_bytes=VMEM_LIMIT_BYTES),
        name="mixer_prompt",
    )(x, nw, wint.stacked, cw, cb, dtb, alog, dskip, snw, scw, wout.stacked)


def _sample_pre_kernel(x_ref, nw_ref, wint_ref, cw_ref, cb_ref, dtb_ref, alog_ref,
                       scw_ref, cst_ref, scst_ref,
                       z_ref, xs_ref, b_ref, c_ref, xdt_blk_ref, da_ref, ysc_ref, cnew_ref, scnew_ref,
                       wint_bf):
    nb = x_ref.shape[0]
    x = x_ref[...]
    h = _rmsnorm(x, nw_ref[...]).astype(BF16)
    wint_bf[...] = wint_ref[...].astype(BF16)
    dt_raw = _dt_proj(h, wint_bf)
    xbc = _in_proj(h, wint_bf, OFF_XBC, SSD_CONV_DIM)
    z_ref[...] = _in_proj(h, wint_bf, OFF_Z, SSD_WIDTH)
    scb = _in_proj(h, wint_bf, OFF_SCB, SC_WIDTH)
    scc = _in_proj(h, wint_bf, OFF_SCC, SC_WIDTH)
    sch = _in_proj(h, wint_bf, OFF_SCH, SC_WIDTH)

    cw = cw_ref[...]
    acc = xbc * cw[SSD_CONV_W - 1:SSD_CONV_W, :]
    for k in range(SSD_CONV_W - 1):
        acc = acc + cst_ref[:, k, :] * cw[k:k + 1, :]
    for k in range(SSD_CONV_W - 2):
        cnew_ref[:, k, :] = cst_ref[:, k + 1, :]
    cnew_ref[:, SSD_CONV_W - 2, :] = xbc
    xbc_c = _silu(acc + cb_ref[...])
    xs = xbc_c[:, 0:SSD_WIDTH]
    xs_ref[...] = xs
    b_ref[...] = xbc_c[:, SSD_WIDTH:SSD_WIDTH + N_GROUPS * D_STATE]
    c_ref[...] = xbc_c[:, SSD_WIDTH + N_GROUPS * D_STATE:SSD_CONV_DIM]

    dt = _softplus(dt_raw + dtb_ref[...])
    da_ref[...] = jnp.exp(dt * (-jnp.exp(alog_ref[...])))
    xdt_t = (xs * _expand_heads(dt, nb)).T
    bb = xdt_blk_ref.shape[2]
    for j in range(nb // bb):
        xdt_blk_ref[j] = xdt_t[:, j * bb:(j + 1) * bb]

    u = scc * sch
    scw = scw_ref[...]
    v = u * scw[SC_CONV_W - 1:SC_CONV_W, :]
    for k in range(SC_CONV_W - 1):
        v = v + scst_ref[:, k, :] * scw[k:k + 1, :]
    for k in range(SC_CONV_W - 2):
        scnew_ref[:, k, :] = scst_ref[:, k + 1, :]
    scnew_ref[:, SC_CONV_W - 2, :] = u
    ysc_ref[...] = scb * v


def _sample_pre(x, nw, wint, cw, cb, dtb, alog, scw, cst, scst):
    nb, d = x.shape
    f = lambda *s: jax.ShapeDtypeStruct(s, F32)
    bb = STATE_BATCH_BLOCK
    assert nb % bb == 0
    out_shape = (f(nb, SSD_WIDTH), f(nb, SSD_WIDTH), f(nb, N_GROUPS * D_STATE), f(nb, N_GROUPS * D_STATE),
                 f(nb // bb, SSD_WIDTH, bb), f(nb, LANES), f(nb, SC_WIDTH),
                 f(nb, SSD_CONV_W - 1, SSD_CONV_DIM), f(nb, SC_CONV_W - 1, SC_WIDTH))
    return pl.pallas_call(
        _sample_pre_kernel,
        out_shape=out_shape,
        scratch_shapes=[pltpu.VMEM(wint.shape, BF16)],
        compiler_params=pltpu.CompilerParams(vmem_limit_bytes=VMEM_LIMIT_BYTES),
        name="sample_pre",
    )(x, nw, wint, cw, cb, dtb, alog, scw, cst, scst)


STATE_BATCH_BLOCK = 16


def _split2(x):
    hi = x.astype(BF16)
    return hi, (x - hi.astype(F32)).astype(BF16)


def _sample_state_kernel(da_ref, s0_ref, xdt_ref, b_ref, c_ref, snew_ref, y_ref):
    bb = STATE_BATCH_BLOCK
    blk = pl.program_id(0)
    xdt_t = xdt_ref[0]
    b_rows = b_ref[...]
    c_hi, c_lo = _split2(c_ref[...])
    row = lax.broadcasted_iota(jnp.int32, (bb, SSD_WIDTH), 0)
    y_blk = jnp.zeros((bb, SSD_WIDTH), F32)
    for i in range(bb):
        xdt_col = xdt_t[:, i:i + 1]
        y_parts = []
        for g in range(N_GROUPS):
            ns = slice(g * D_STATE, (g + 1) * D_STATE)
            b_row = b_rows[i:i + 1, ns]
            heads = []
            for hh in range(g * HEADS_PER_GROUP, (g + 1) * HEADS_PER_GROUP):
                rs = slice(hh * HEAD_DIM, (hh + 1) * HEAD_DIM)
                decay = da_ref[blk * bb + i, hh]
                heads.append(s0_ref[i, rs, :] * decay + xdt_col[rs] * b_row)
            s_new = jnp.concatenate(heads, axis=0)
            snew_ref[i, g * GROUP_WIDTH:(g + 1) * GROUP_WIDTH, :] = s_new
            s_hi, s_lo = _split2(s_new)
            lhs = jnp.concatenate([c_hi[:, ns], c_lo[:, ns]], axis=0)
            r_hi = _dot_nt(lhs, s_hi)
            r_lo = _dot_nt(c_hi[:, ns], s_lo)
            y_parts.append(r_hi[i:i + 1] + r_hi[bb + i:bb + i + 1] + r_lo[i:i + 1])
        y_row = jnp.concatenate(y_parts, axis=1)
        y_blk = jnp.where(row == i, y_row, y_blk)
    y_ref[...] = y_blk


def _sample_state(s0, xdt_blocks, decay, b_rows, c_rows):
    nb = s0.shape[0]
    nblk, _, bb = xdt_blocks.shape
    assert bb == STATE_BATCH_BLOCK and nblk * bb == nb
    return pl.pallas_call(
        _sample_state_kernel,
        out_shape=(jax.ShapeDtypeStruct((nb, SSD_WIDTH, D_STATE), F32),
                   jax.ShapeDtypeStruct((nb, SSD_WIDTH), F32)),
        grid=(nblk,),
        in_specs=[pl.BlockSpec(memory_space=pltpu.SMEM),
                  pl.BlockSpec((bb, SSD_WIDTH, D_STATE), lambda i: (i, 0, 0)),
                  pl.BlockSpec((1, SSD_WIDTH, bb), lambda i: (i, 0, 0)),
                  pl.BlockSpec((bb, N_GROUPS * D_STATE), lambda i: (i, 0)),
                  pl.BlockSpec((bb, N_GROUPS * D_STATE), lambda i: (i, 0))],
        out_specs=(pl.BlockSpec((bb, SSD_WIDTH, D_STATE), lambda i: (i, 0, 0)),
                   pl.BlockSpec((bb, SSD_WIDTH), lambda i: (i, 0))),
        compiler_params=pltpu.CompilerParams(
            dimension_semantics=("arbitrary",), vmem_limit_bytes=VMEM_LIMIT_BYTES),
        name="sample_state",
    )(decay, s0, xdt_blocks, b_rows, c_rows)


def _sample_post_kernel(x_ref, yraw_ref, xs_ref, z_ref, ysc_ref, dskip_ref, snw_ref, wout_ref, o_ref):
    y = yraw_ref[...] + dskip_ref[...] * xs_ref[...]
    y = y * _silu(z_ref[...])
    y_ssd = _group_rmsnorm(y, snw_ref[...])
    mixed = jnp.concatenate([y_ssd, ysc_ref[...]], axis=1).astype(BF16)
    o_ref[...] = x_ref[...] + _dot(mixed, wout_ref[...].astype(BF16))


def _sample_post(x, yraw, xs, z, ysc, dskip, snw, wout):
    return pl.pallas_call(
        _sample_post_kernel,
        out_shape=jax.ShapeDtypeStruct(x.shape, F32),
        compiler_params=pltpu.CompilerParams(vmem_limit_bytes=VMEM_LIMIT_BYTES),
        name="sample_post",
    )(x, yraw, xs, z, ysc, dskip, snw, wout)


FFN_TILE_M = 512


def _layer_params(i, norm_ffn1_w, ffn1_w_gate, ffn1_w_up, ffn1_w_down, norm_mix_w, w_in_t,
                  ssd_conv_w, ssd_conv_b, dt_bias, a_log, d_skip, ssd_norm_w, sconv_w, w_out,
                  norm_ffn2_w, ffn2_w_gate, ffn2_w_up, ffn2_w_down):
    pad_heads = lambda v: jnp.pad(v, (0, LANES - N_HEADS)).reshape(1, LANES)
    row = lambda v: v.reshape(1, -1)
    return dict(
        ffn1=(row(norm_ffn1_w[i]), _LayerWeight(ffn1_w_gate, i), _LayerWeight(ffn1_w_up, i),
              _LayerWeight(ffn1_w_down, i)),
        ffn2=(row(norm_ffn2_w[i]), _LayerWeight(ffn2_w_gate, i), _LayerWeight(ffn2_w_up, i),
              _LayerWeight(ffn2_w_down, i)),
        nw=row(norm_mix_w[i]), wint=_LayerWeight(w_in_t, i), cw=ssd_conv_w[i], cb=row(ssd_conv_b[i]),
        dtb=pad_heads(dt_bias[i]), alog=pad_heads(a_log[i]),
        dskip=row(jnp.repeat(d_skip[i], HEAD_DIM)), snw=row(ssd_norm_w[i]), scw=sconv_w[i],
        wout=_LayerWeight(w_out, i))


def kernel(x_prompt, x_sample, state_ssm, state_ssd_conv, state_sconv, norm_ffn1_w, ffn1_w_gate, ffn1_w_up, ffn1_w_down, norm_mix_w, w_in, ssd_conv_w, ssd_conv_b, dt_bias, a_log, d_skip, ssd_norm_w, sconv_w, w_out, norm_ffn2_w, ffn2_w_gate, ffn2_w_up, ffn2_w_down, final_norm_w):
    depth = w_in.shape[0]
    bp, seq, d = x_prompt.shape
    bs, dec_seq, _ = x_sample.shape
    assert dec_seq == 1, "sample group is one token per sequence"
    fnw = final_norm_w.reshape(1, d)
    w_in_t = jnp.swapaxes(w_in, 1, 2)

    xp = x_prompt.reshape(bp * seq, d)
    xs = x_sample.reshape(bs, d)
    outs = [[] for _ in range(6)]
    for i in range(depth):
        p = _layer_params(i, norm_ffn1_w, ffn1_w_gate, ffn1_w_up, ffn1_w_down, norm_mix_w, w_in_t,
                          ssd_conv_w, ssd_conv_b, dt_bias, a_log, d_skip, ssd_norm_w, sconv_w,
                          w_out, norm_ffn2_w, ffn2_w_gate, ffn2_w_up, ffn2_w_down)
        last = i == depth - 1
        xp, xs = _ffn(xp, xs, *p["ffn1"], tile_m=FFN_TILE_M)
        xp3, s_p, c_p, sc_p = _mixer_prompt(
            xp.reshape(bp, seq, d), p["nw"], p["wint"], p["cw"], p["cb"], p["dtb"], p["alog"],
            p["dskip"], p["snw"], p["scw"], p["wout"])
        z, xs_conv, b_rows, c_rows, xdt_t, decay, ysc, c_s, sc_s = _sample_pre(
            xs, p["nw"], w_in_t[i], p["cw"], p["cb"], p["dtb"], p["alog"], p["scw"],
            state_ssd_conv[i], state_sconv[i])
        s_s, yraw = _sample_state(state_ssm[i].reshape(bs, SSD_WIDTH, D_STATE), xdt_t, decay, b_rows, c_rows)
        xs = _sample_post(xs, yraw, xs_conv, z, ysc, p["dskip"], p["snw"], w_out[i])
        xp, xs = _ffn(xp3.reshape(bp * seq, d), xs, *p["ffn2"], fnw if last else None, tile_m=FFN_TILE_M)
        for lst, v in zip(outs, (s_p.reshape(bp, N_HEADS, HEAD_DIM, D_STATE), c_p, sc_p,
                                 s_s.reshape(bs, N_HEADS, HEAD_DIM, D_STATE),
                                 c_s, sc_s)):
            lst.append(v)
    return (xp.reshape(bp, seq, d), xs.reshape(bs, dec_seq, d)) + tuple(jnp.stack(l) for l in outs)
```

```python
import functools

import jax
import jax.numpy as jnp
from jax import lax
from jax.experimental import pallas as pl
from jax.experimental.pallas import tpu as pltpu

F32 = jnp.float32
BF16 = jnp.bfloat16

D_MODEL = 1024
SSD_WIDTH = 1024
SC_WIDTH = 1024
HEAD_DIM = 64
N_HEADS = SSD_WIDTH // HEAD_DIM
N_GROUPS = 2
HEADS_PER_GROUP = N_HEADS // N_GROUPS
GROUP_WIDTH = SSD_WIDTH // N_GROUPS
D_STATE = 128
SSD_CONV_W = 4
SSD_CONV_DIM = SSD_WIDTH + 2 * N_GROUPS * D_STATE
SC_CONV_W = 3
CHUNK = 256
NORM_EPS = 1e-6

LANES = 128
SUBLANES = 8

OFF_Z = 0
OFF_XBC = OFF_Z + SSD_WIDTH
OFF_DT = OFF_XBC + SSD_CONV_DIM
OFF_SCB = OFF_DT + N_HEADS
OFF_SCC = OFF_SCB + SC_WIDTH
OFF_SCH = OFF_SCC + SC_WIDTH
D_IN_PROJ = OFF_SCH + SC_WIDTH
PROJ_BLOCK = 512

VMEM_LIMIT_BYTES = 56 * 1024 * 1024


def _rmsnorm(x, w):
    ms = jnp.mean(x * x, axis=-1, keepdims=True)
    return x * lax.rsqrt(ms + NORM_EPS) * w


def _silu(x):
    half = 0.5 * x
    return half * jnp.tanh(half) + half


def _silu_mixer(x):
    return x * (0.5 * jnp.tanh(0.5 * x) + 0.5)


def _softplus(x):
    return jnp.maximum(x, 0.0) + jnp.log1p(jnp.exp(-jnp.abs(x)))


def _dot(a, b):
    return jnp.dot(a, b, preferred_element_type=F32)


def _dot_nt(a, b):
    return lax.dot_general(a, b, (((1,), (1,)), ((), ())), preferred_element_type=F32)


def _split3(x):
    hi = x.astype(BF16)
    r1 = x - hi.astype(F32)
    mid = r1.astype(BF16)
    lo = (r1 - mid.astype(F32)).astype(BF16)
    return hi, mid, lo


def _expand_heads(v, rows):
    lane = lax.broadcasted_iota(jnp.int32, (rows, LANES), 1)
    first_half = lane < HEAD_DIM
    blocks = []
    for q in range(N_HEADS // 2):
        c0 = jnp.broadcast_to(v[:, 2 * q:2 * q + 1], (rows, LANES))
        c1 = jnp.broadcast_to(v[:, 2 * q + 1:2 * q + 2], (rows, LANES))
        blocks.append(jnp.where(first_half, c0, c1))
    return jnp.concatenate(blocks, axis=1)


def _group_rmsnorm(y, w):
    outs = []
    for g in range(N_GROUPS):
        sl = slice(g * GROUP_WIDTH, (g + 1) * GROUP_WIDTH)
        outs.append(_rmsnorm(y[:, sl], w[:, sl]))
    return jnp.concatenate(outs, axis=1)


def _shift_rows(x, prev_tile, j):
    rolled = pltpu.roll(x, j, 0)
    row = lax.broadcasted_iota(jnp.int32, prev_tile.shape, 0)
    head = jnp.where(row < j, pltpu.roll(prev_tile, j, 0), rolled[0:SUBLANES])
    return jnp.concatenate([head, rolled[SUBLANES:]], axis=0)


def _causal_conv(u, prev_tile, w):
    k = w.shape[0]
    acc = u * w[k - 1:k, :]
    for j in range(1, k):
        acc = acc + _shift_rows(u, prev_tile, j) * w[k - 1 - j:k - j, :]
    return acc


WEIGHT_LOAD_STEPS = 8


def _load_weight_rows(step, w_ref, w_bf_ref):
    rows = w_ref.shape[0]
    r0 = pl.multiple_of(step * rows, rows)
    w_bf_ref[pl.ds(r0, rows), :] = w_ref[...].astype(BF16)


def _weight_block_rows(rows, steps):
    tile = 2 * SUBLANES
    return -(-rows // (steps * tile)) * tile


def _weight_rows_spec(w, steps):
    _, rows, cols = w.stacked.shape
    return pl.BlockSpec((None, _weight_block_rows(rows, steps), cols),
                        lambda i: (w.layer, jnp.minimum(i, steps - 1), 0))


def _weight_scratch(w, steps):
    rows, cols = w.shape
    return pltpu.VMEM((steps * _weight_block_rows(rows, steps), cols), BF16)


class _LayerWeight:
    def __init__(self, stacked, layer):
        self.stacked, self.layer = stacked, layer
        self.shape = stacked.shape[1:]


def _ffn_kernel(*refs, final_norm):
    if final_norm:
        (xp_ref, xs_ref, nw_ref, wg_ref, wu_ref, wd_ref, fnw_ref, op_ref, os_ref,
         wg_bf, wu_bf, wd_bf) = refs
    else:
        xp_ref, xs_ref, nw_ref, wg_ref, wu_ref, wd_ref, op_ref, os_ref, wg_bf, wu_bf, wd_bf = refs

    def half_step(x_ref, o_ref):
        x = x_ref[...]
        xn = _rmsnorm(x, nw_ref[...]).astype(BF16)
        g = _dot(xn, wg_bf[...])
        u = _dot(xn, wu_bf[...])
        hmid = (_silu(g) * u).astype(BF16)
        y = x + 0.5 * _dot(hmid, wd_bf[...])
        if final_norm:
            y = _rmsnorm(y, fnw_ref[...])
        o_ref[...] = y

    i = pl.program_id(0)
    sample_step = pl.num_programs(0) - 1

    @pl.when(i < WEIGHT_LOAD_STEPS)
    def _():
        for w_ref, w_bf in ((wg_ref, wg_bf), (wu_ref, wu_bf), (wd_ref, wd_bf)):
            _load_weight_rows(i, w_ref, w_bf)

    @pl.when((i >= WEIGHT_LOAD_STEPS) & (i < sample_step))
    def _():
        half_step(xp_ref, op_ref)

    @pl.when(i == sample_step)
    def _():
        half_step(xs_ref, os_ref)


def _resident(shape):
    return pl.BlockSpec(shape, lambda *_: (0,) * len(shape), pipeline_mode=pl.Buffered(1))


def _ffn(xp, xs, nw, wg, wu, wd, fnw=None, *, tile_m):
    m, d = xp.shape
    ms = xs.shape[0]
    dff = wg.shape[1]
    assert m % tile_m == 0
    n = m // tile_m
    final_norm = fnw is not None
    ws = WEIGHT_LOAD_STEPS
    prompt_tile = pl.BlockSpec((tile_m, d), lambda i: (jnp.clip(i - ws, 0, n - 1), 0))
    sample_tile = pl.BlockSpec((ms, d), lambda i: (0, 0))
    in_specs = [prompt_tile, sample_tile, _resident((1, d)),
                _weight_rows_spec(wg, ws), _weight_rows_spec(wu, ws), _weight_rows_spec(wd, ws)]
    args = [xp, xs, nw, wg.stacked, wu.stacked, wd.stacked]
    if final_norm:
        in_specs.append(_resident((1, d)))
        args.append(fnw)
    return pl.pallas_call(
        functools.partial(_ffn_kernel, final_norm=final_norm),
        out_shape=(jax.ShapeDtypeStruct((m, d), F32), jax.ShapeDtypeStruct((ms, d), F32)),
        grid=(ws + n + 1,),
        in_specs=in_specs,
        out_specs=(prompt_tile, sample_tile),
        scratch_shapes=[_weight_scratch(wg, ws), _weight_scratch(wu, ws), _weight_scratch(wd, ws)],
        compiler_params=pltpu.CompilerParams(
            dimension_semantics=("arbitrary",), vmem_limit_bytes=VMEM_LIMIT_BYTES),
        name="ffn_final" if final_norm else "ffn",
    )(*args)


def _in_proj(h, wint_ref, col0, width):
    return _dot_nt(h, wint_ref[col0:col0 + width, :])


def _dt_proj(h, wint_ref):
    raw = _in_proj(h, wint_ref, OFF_DT, LANES)
    lane = lax.broadcasted_iota(jnp.int32, raw.shape, 1)
    return jnp.where(lane < N_HEADS, raw, 0.0)


def _mixer_kernel(tiles_per_seq, x_ref, nw_ref, wint_ref, cw_ref, cb_ref, dtb_ref, alog_ref,
                  dskip_ref, snw_ref, scw_ref, wout_ref,
                  o_ref, sfin_ref, cfin_ref, scfin_ref,
                  st_scr, hist_scr, uhist_scr, wint_bf, wout_bf):
    i = pl.program_id(0)

    @pl.when(i < WEIGHT_LOAD_STEPS)
    def _():
        _load_weight_rows(i, wint_ref, wint_bf)
        _load_weight_rows(i, wout_ref, wout_bf)

    @pl.when(i >= WEIGHT_LOAD_STEPS)
    def _():
        c = lax.rem(i - WEIGHT_LOAD_STEPS, tiles_per_seq)
        _mixer_chunk(c == 0, c == tiles_per_seq - 1, x_ref, nw_ref, wint_bf, cw_ref,
                     cb_ref, dtb_ref, alog_ref, dskip_ref, snw_ref, scw_ref, wout_bf,
                     o_ref, sfin_ref, cfin_ref, scfin_ref, st_scr, hist_scr, uhist_scr)


def _mixer_chunk(first, last, x_ref, nw_ref, wint_ref, cw_ref, cb_ref, dtb_ref, alog_ref,
                 dskip_ref, snw_ref, scw_ref, wout_ref,
                 o_ref, sfin_ref, cfin_ref, scfin_ref,
                 st_scr, hist_scr, uhist_scr):
    L = CHUNK
    T = x_ref.shape[1]

    @pl.when(first)
    def _():
        st_scr[...] = jnp.zeros_like(st_scr)
        hist_scr[...] = jnp.zeros_like(hist_scr)
        uhist_scr[...] = jnp.zeros_like(uhist_scr)

    x = x_ref[0]
    h = _rmsnorm(x, nw_ref[...]).astype(BF16)
    dt_raw = _dt_proj(h, wint_ref)

    xbc_c_blocks = []
    for b0 in range(0, SSD_CONV_DIM, PROJ_BLOCK):
        cols = slice(b0, b0 + PROJ_BLOCK)
        xbc = _in_proj(h, wint_ref, OFF_XBC + b0, PROJ_BLOCK)
        acc = _causal_conv(xbc, hist_scr[:, cols], cw_ref[:, cols])
        hist_scr[:, cols] = xbc[T - SUBLANES:T, :]
        xbc_c_blocks.append(_silu_mixer(acc + cb_ref[:, cols]))
    cfin_ref[0] = hist_scr[SUBLANES - (SSD_CONV_W - 1):SUBLANES, :]
    xbc_c = jnp.concatenate(xbc_c_blocks, axis=1)

    y_sc_blocks = []
    for b0 in range(0, SC_WIDTH, PROJ_BLOCK):
        cols = slice(b0, b0 + PROJ_BLOCK)
        scc = _in_proj(h, wint_ref, OFF_SCC + b0, PROJ_BLOCK)
        sch = _in_proj(h, wint_ref, OFF_SCH + b0, PROJ_BLOCK)
        u = scc * sch
        v = _causal_conv(u, uhist_scr[:, cols], scw_ref[:, cols])
        uhist_scr[:, cols] = u[T - SUBLANES:T, :]
        scb = _in_proj(h, wint_ref, OFF_SCB + b0, PROJ_BLOCK)
        y_sc_blocks.append((scb * v).astype(BF16))
    scfin_ref[0] = uhist_scr[SUBLANES - (SC_CONV_W - 1):SUBLANES, :]
    y_sc = jnp.concatenate(y_sc_blocks, axis=1)

    z_gate_all = jnp.concatenate(
        [_silu_mixer(_in_proj(h, wint_ref, OFF_Z + b0, PROJ_BLOCK))
         for b0 in range(0, SSD_WIDTH, PROJ_BLOCK)], axis=1)

    dt_all = _softplus(dt_raw + dtb_ref[...])
    a_all = dt_all * (-jnp.exp(alog_ref[...]))
    row = lax.broadcasted_iota(jnp.int32, (L, L), 0)
    col = lax.broadcasted_iota(jnp.int32, (L, L), 1)
    causal = row >= col
    tril = jnp.where(causal, 1.0, 0.0).astype(BF16)
    lane = lax.broadcasted_iota(jnp.int32, (L, LANES), 1)
    first_half = lane < HEAD_DIM

    y_ssd_chunks = []
    for k in range(T // L):
        rows = slice(k * L, (k + 1) * L)
        xs = xbc_c[rows, 0:SSD_WIDTH]
        b_all = xbc_c[rows, SSD_WIDTH:SSD_WIDTH + N_GROUPS * D_STATE]
        c_all = xbc_c[rows, SSD_WIDTH + N_GROUPS * D_STATE:SSD_CONV_DIM]
        dt = dt_all[rows]
        a_hi, a_mid, a_lo = _split3(a_all[rows])
        acum = _dot(tril, a_hi) + _dot(tril, a_mid) + _dot(tril, a_lo)
        acum_t = acum.T

        acum_e = _expand_heads(acum, L)
        dt_e = _expand_heads(dt, L)
        exp_acum_e = jnp.exp(acum_e)
        decay_end_e = jnp.exp(acum_e[L - 1:L, :] - acum_e)
        xdt = xs * dt_e
        xdt_b = xdt.astype(BF16)
        xdecay_b = (xdt * decay_end_e).astype(BF16)

        y_diag_blocks = []
        y_off_blocks = []
        for g in range(N_GROUPS):
            b_g = b_all[:, g * D_STATE:(g + 1) * D_STATE]
            c_g = c_all[:, g * D_STATE:(g + 1) * D_STATE]
            b_gb = b_g.astype(BF16)
            c_gb = c_g.astype(BF16)
            cb = _dot_nt(c_gb, b_gb)
            for q in range(g * HEADS_PER_GROUP // 2, (g + 1) * HEADS_PER_GROUP // 2):
                ms = []
                for hh in (2 * q, 2 * q + 1):
                    seg = acum[:, hh:hh + 1] - acum_t[hh:hh + 1, :]
                    decay = jnp.exp(jnp.where(causal, seg, -jnp.inf))
                    ms.append((cb * decay).astype(BF16))
                m_cat = jnp.concatenate(ms, axis=1)
                x2 = xdt_b[:, q * LANES:(q + 1) * LANES]
                zero = jnp.zeros_like(x2)
                rhs = jnp.concatenate([jnp.where(first_half, x2, zero),
                                       jnp.where(first_half, zero, x2)], axis=0)
                y_diag_blocks.append(_dot(m_cat, rhs))
            gs = slice(g * GROUP_WIDTH, (g + 1) * GROUP_WIDTH)
            s_enter = st_scr[:, gs]
            y_off_blocks.append(_dot(c_gb, s_enter.astype(BF16)))
            new_states = _dot(b_g.T.astype(BF16), xdecay_b[:, gs])
            st_scr[:, gs] = s_enter * exp_acum_e[L - 1:L, gs] + new_states
        y_diag = jnp.concatenate(y_diag_blocks, axis=1)
        y_off = jnp.concatenate(y_off_blocks, axis=1) * exp_acum_e

        y = (y_diag + y_off + dskip_ref[...] * xs) * z_gate_all[rows]
        y_ssd_chunks.append(_group_rmsnorm(y, snw_ref[...]).astype(BF16))

    mixed = jnp.concatenate([jnp.concatenate(y_ssd_chunks, axis=0), y_sc], axis=1)
    o_ref[0] = x + _dot(mixed, wout_ref[...])

    @pl.when(last)
    def _():
        sfin_ref[0] = st_scr[...].T


MIXER_CHUNKS_PER_STEP = 2


def _mixer_prompt(x, nw, wint, cw, cb, dtb, alog, dskip, snw, scw, wout):
    nb, seq, d = x.shape
    tile_t = MIXER_CHUNKS_PER_STEP * CHUNK
    assert seq % tile_t == 0
    nc = seq // tile_t
    ws = WEIGHT_LOAD_STEPS

    def chunk_map(i):
        t = jnp.maximum(i - ws, 0)
        return (t // nc, t % nc, 0)

    def seq_map(i):
        return (jnp.maximum(i - ws, 0) // nc, 0, 0)

    out_shape = (
        jax.ShapeDtypeStruct((nb, seq, d), F32),
        jax.ShapeDtypeStruct((nb, SSD_WIDTH, D_STATE), F32),
        jax.ShapeDtypeStruct((nb, SSD_CONV_W - 1, SSD_CONV_DIM), F32),
        jax.ShapeDtypeStruct((nb, SC_CONV_W - 1, SC_WIDTH), F32),
    )
    return pl.pallas_call(
        functools.partial(_mixer_kernel, nc),
        out_shape=out_shape,
        grid=(ws + nb * nc,),
        in_specs=[pl.BlockSpec((1, tile_t, d), chunk_map),
                  _resident((1, d)), _weight_rows_spec(wint, ws),
                  _resident((SSD_CONV_W, SSD_CONV_DIM)), _resident((1, SSD_CONV_DIM)),
                  _resident((1, LANES)), _resident((1, LANES)), _resident((1, SSD_WIDTH)),
                  _resident((1, SSD_WIDTH)), _resident((SC_CONV_W, SC_WIDTH)),
                  _weight_rows_spec(wout, ws)],
        out_specs=(pl.BlockSpec((1, tile_t, d), chunk_map),
                   pl.BlockSpec((1, SSD_WIDTH, D_STATE), seq_map),
                   pl.BlockSpec((1, SSD_CONV_W - 1, SSD_CONV_DIM), seq_map),
                   pl.BlockSpec((1, SC_CONV_W - 1, SC_WIDTH), seq_map)),
        scratch_shapes=[pltpu.VMEM((D_STATE, SSD_WIDTH), F32),
                        pltpu.VMEM((SUBLANES, SSD_CONV_DIM), F32),
                        pltpu.VMEM((SUBLANES, SC_WIDTH), F32),
                        _weight_scratch(wint, ws), _weight_scratch(wout, ws)],
        compiler_params=pltpu.CompilerParams(
            dimension_semantics=("arbitrary",), vmem_limit_bytes=VMEM_LIMIT_BYTES),
        name="mixer_prompt",
    )(x, nw, wint.stacked, cw, cb, dtb, alog, dskip, snw, scw, wout.stacked)


def _sample_pre_kernel(x_ref, nw_ref, wint_ref, cw_ref, cb_ref, dtb_ref, alog_ref,
                       scw_ref, cst_ref, scst_ref,
                       z_ref, xs_ref, b_ref, c_ref, xdt_blk_ref, da_ref, ysc_ref, cnew_ref, scnew_ref,
                       wint_bf):
    nb = x_ref.shape[0]
    x = x_ref[...]
    h = _rmsnorm(x, nw_ref[...]).astype(BF16)
    wint_bf[...] = wint_ref[...].astype(BF16)
    dt_raw = _dt_proj(h, wint_bf)
    xbc = _in_proj(h, wint_bf, OFF_XBC, SSD_CONV_DIM)
    z_ref[...] = _in_proj(h, wint_bf, OFF_Z, SSD_WIDTH)
    scb = _in_proj(h, wint_bf, OFF_SCB, SC_WIDTH)
    scc = _in_proj(h, wint_bf, OFF_SCC, SC_WIDTH)
    sch = _in_proj(h, wint_bf, OFF_SCH, SC_WIDTH)

    cw = cw_ref[...]
    acc = xbc * cw[SSD_CONV_W - 1:SSD_CONV_W, :]
    for k in range(SSD_CONV_W - 1):
        acc = acc + cst_ref[:, k, :] * cw[k:k + 1, :]
    for k in range(SSD_CONV_W - 2):
        cnew_ref[:, k, :] = cst_ref[:, k + 1, :]
    cnew_ref[:, SSD_CONV_W - 2, :] = xbc
    xbc_c = _silu(acc + cb_ref[...])
    xs = xbc_c[:, 0:SSD_WIDTH]
    xs_ref[...] = xs
    b_ref[...] = xbc_c[:, SSD_WIDTH:SSD_WIDTH + N_GROUPS * D_STATE]
    c_ref[...] = xbc_c[:, SSD_WIDTH + N_GROUPS * D_STATE:SSD_CONV_DIM]

    dt = _softplus(dt_raw + dtb_ref[...])
    da_ref[...] = jnp.exp(dt * (-jnp.exp(alog_ref[...])))
    xdt_t = (xs * _expand_heads(dt, nb)).T
    bb = xdt_blk_ref.shape[2]
    for j in range(nb // bb):
        xdt_blk_ref[j] = xdt_t[:, j * bb:(j + 1) * bb]

    u = scc * sch
    scw = scw_ref[...]
    v = u * scw[SC_CONV_W - 1:SC_CONV_W, :]
    for k in range(SC_CONV_W - 1):
        v = v + scst_ref[:, k, :] * scw[k:k + 1, :]
    for k in range(SC_CONV_W - 2):
        scnew_ref[:, k, :] = scst_ref[:, k + 1, :]
    scnew_ref[:, SC_CONV_W - 2, :] = u
    ysc_ref[...] = scb * v


def _sample_pre(x, nw, wint, cw, cb, dtb, alog, scw, cst, scst):
    nb, d = x.shape
    f = lambda *s: jax.ShapeDtypeStruct(s, F32)
    bb = STATE_BATCH_BLOCK
    assert nb % bb == 0
    out_shape = (f(nb, SSD_WIDTH), f(nb, SSD_WIDTH), f(nb, N_GROUPS * D_STATE), f(nb, N_GROUPS * D_STATE),
                 f(nb // bb, SSD_WIDTH, bb), f(nb, LANES), f(nb, SC_WIDTH),
                 f(nb, SSD_CONV_W - 1, SSD_CONV_DIM), f(nb, SC_CONV_W - 1, SC_WIDTH))
    return pl.pallas_call(
        _sample_pre_kernel,
        out_shape=out_shape,
        scratch_shapes=[pltpu.VMEM(wint.shape, BF16)],
        compiler_params=pltpu.CompilerParams(vmem_limit_bytes=VMEM_LIMIT_BYTES),
        name="sample_pre",
    )(x, nw, wint, cw, cb, dtb, alog, scw, cst, scst)


STATE_BATCH_BLOCK = 16


def _split2(x):
    hi = x.astype(BF16)
    return hi, (x - hi.astype(F32)).astype(BF16)


def _sample_state_kernel(da_ref, s0_ref, xdt_ref, b_ref, c_ref, snew_ref, y_ref):
    bb = STATE_BATCH_BLOCK
    blk = pl.program_id(0)
    xdt_t = xdt_ref[0]
    b_rows = b_ref[...]
    c_hi, c_lo = _split2(c_ref[...])
    row = lax.broadcasted_iota(jnp.int32, (bb, SSD_WIDTH), 0)
    y_blk = jnp.zeros((bb, SSD_WIDTH), F32)
    for i in range(bb):
        xdt_col = xdt_t[:, i:i + 1]
        y_parts = []
        for g in range(N_GROUPS):
            ns = slice(g * D_STATE, (g + 1) * D_STATE)
            b_row = b_rows[i:i + 1, ns]
            heads = []
            for hh in range(g * HEADS_PER_GROUP, (g + 1) * HEADS_PER_GROUP):
                rs = slice(hh * HEAD_DIM, (hh + 1) * HEAD_DIM)
                decay = da_ref[blk * bb + i, hh]
                heads.append(s0_ref[i, rs, :] * decay + xdt_col[rs] * b_row)
            s_new = jnp.concatenate(heads, axis=0)
            snew_ref[i, g * GROUP_WIDTH:(g + 1) * GROUP_WIDTH, :] = s_new
            s_hi, s_lo = _split2(s_new)
            lhs = jnp.concatenate([c_hi[:, ns], c_lo[:, ns]], axis=0)
            r_hi = _dot_nt(lhs, s_hi)
            r_lo = _dot_nt(c_hi[:, ns], s_lo)
            y_parts.append(r_hi[i:i + 1] + r_hi[bb + i:bb + i + 1] + r_lo[i:i + 1])
        y_row = jnp.concatenate(y_parts, axis=1)
        y_blk = jnp.where(row == i, y_row, y_blk)
    y_ref[...] = y_blk


def _sample_state(s0, xdt_blocks, decay, b_rows, c_rows):
    nb = s0.shape[0]
    nblk, _, bb = xdt_blocks.shape
    assert bb == STATE_BATCH_BLOCK and nblk * bb == nb
    return pl.pallas_call(
        _sample_state_kernel,
        out_shape=(jax.ShapeDtypeStruct((nb, SSD_WIDTH, D_STATE), F32),
                   jax.ShapeDtypeStruct((nb, SSD_WIDTH), F32)),
        grid=(nblk,),
        in_specs=[pl.BlockSpec(memory_space=pltpu.SMEM),
                  pl.BlockSpec((bb, SSD_WIDTH, D_STATE), lambda i: (i, 0, 0)),
                  pl.BlockSpec((1, SSD_WIDTH, bb), lambda i: (i, 0, 0)),
                  pl.BlockSpec((bb, N_GROUPS * D_STATE), lambda i: (i, 0)),
                  pl.BlockSpec((bb, N_GROUPS * D_STATE), lambda i: (i, 0))],
        out_specs=(pl.BlockSpec((bb, SSD_WIDTH, D_STATE), lambda i: (i, 0, 0)),
                   pl.BlockSpec((bb, SSD_WIDTH), lambda i: (i, 0))),
        compiler_params=pltpu.CompilerParams(
            dimension_semantics=("arbitrary",), vmem_limit_bytes=VMEM_LIMIT_BYTES),
        name="sample_state",
    )(decay, s0, xdt_blocks, b_rows, c_rows)


def _sample_post_kernel(x_ref, yraw_ref, xs_ref, z_ref, ysc_ref, dskip_ref, snw_ref, wout_ref, o_ref):
    y = yraw_ref[...] + dskip_ref[...] * xs_ref[...]
    y = y * _silu(z_ref[...])
    y_ssd = _group_rmsnorm(y, snw_ref[...])
    mixed = jnp.concatenate([y_ssd, ysc_ref[...]], axis=1).astype(BF16)
    o_ref[...] = x_ref[...] + _dot(mixed, wout_ref[...].astype(BF16))


def _sample_post(x, yraw, xs, z, ysc, dskip, snw, wout):
    return pl.pallas_call(
        _sample_post_kernel,
        out_shape=jax.ShapeDtypeStruct(x.shape, F32),
        compiler_params=pltpu.CompilerParams(vmem_limit_bytes=VMEM_LIMIT_BYTES),
        name="sample_post",
    )(x, yraw, xs, z, ysc, dskip, snw, wout)


FFN_TILE_M = 512


def _layer_params(i, norm_ffn1_w, ffn1_w_gate, ffn1_w_up, ffn1_w_down, norm_mix_w, w_in_t,
                  ssd_conv_w, ssd_conv_b, dt_bias, a_log, d_skip, ssd_norm_w, sconv_w, w_out,
                  norm_ffn2_w, ffn2_w_gate, ffn2_w_up, ffn2_w_down):
    pad_heads = lambda v: jnp.pad(v, (0, LANES - N_HEADS)).reshape(1, LANES)
    row = lambda v: v.reshape(1, -1)
    return dict(
        ffn1=(row(norm_ffn1_w[i]), _LayerWeight(ffn1_w_gate, i), _LayerWeight(ffn1_w_up, i),
              _LayerWeight(ffn1_w_down, i)),
        ffn2=(row(norm_ffn2_w[i]), _LayerWeight(ffn2_w_gate, i), _LayerWeight(ffn2_w_up, i),
              _LayerWeight(ffn2_w_down, i)),
        nw=row(norm_mix_w[i]), wint=_LayerWeight(w_in_t, i), cw=ssd_conv_w[i], cb=row(ssd_conv_b[i]),
        dtb=pad_heads(dt_bias[i]), alog=pad_heads(a_log[i]),
        dskip=row(jnp.repeat(d_skip[i], HEAD_DIM)), snw=row(ssd_norm_w[i]), scw=sconv_w[i],
        wout=_LayerWeight(w_out, i))


def kernel(x_prompt, x_sample, state_ssm, state_ssd_conv, state_sconv, norm_ffn1_w, ffn1_w_gate, ffn1_w_up, ffn1_w_down, norm_mix_w, w_in, ssd_conv_w, ssd_conv_b, dt_bias, a_log, d_skip, ssd_norm_w, sconv_w, w_out, norm_ffn2_w, ffn2_w_gate, ffn2_w_up, ffn2_w_down, final_norm_w):
    depth = w_in.shape[0]
    bp, seq, d = x_prompt.shape
    bs, dec_seq, _ = x_sample.shape
    assert dec_seq == 1, "sample group is one token per sequence"
    fnw = final_norm_w.reshape(1, d)
    w_in_t = jnp.swapaxes(w_in, 1, 2)

    xp = x_prompt.reshape(bp * seq, d)
    xs = x_sample.reshape(bs, d)
    outs = [[] for _ in range(6)]
    for i in range(depth):
        p = _layer_params(i, norm_ffn1_w, ffn1_w_gate, ffn1_w_up, ffn1_w_down, norm_mix_w, w_in_t,
                          ssd_conv_w, ssd_conv_b, dt_bias, a_log, d_skip, ssd_norm_w, sconv_w,
                          w_out, norm_ffn2_w, ffn2_w_gate, ffn2_w_up, ffn2_w_down)
        last = i == depth - 1
        xp, xs = _ffn(xp, xs, *p["ffn1"], tile_m=FFN_TILE_M)
        xp3, s_p, c_p, sc_p = _mixer_prompt(
            xp.reshape(bp, seq, d), p["nw"], p["wint"], p["cw"], p["cb"], p["dtb"], p["alog"],
            p["dskip"], p["snw"], p["scw"], p["wout"])
        z, xs_conv, b_rows, c_rows, xdt_t, decay, ysc, c_s, sc_s = _sample_pre(
            xs, p["nw"], w_in_t[i], p["cw"], p["cb"], p["dtb"], p["alog"], p["scw"],
            state_ssd_conv[i], state_sconv[i])
        s_s, yraw = _sample_state(state_ssm[i].reshape(bs, SSD_WIDTH, D_STATE), xdt_t, decay, b_rows, c_rows)
        xs = _sample_post(xs, yraw, xs_conv, z, ysc, p["dskip"], p["snw"], w_out[i])
        xp, xs = _ffn(xp3.reshape(bp * seq, d), xs, *p["ffn2"], fnw if last else None, tile_m=FFN_TILE_M)
        for lst, v in zip(outs, (s_p.reshape(bp, N_HEADS, HEAD_DIM, D_STATE), c_p, sc_p,
                                 s_s.reshape(bs, N_HEADS, HEAD_DIM, D_STATE),
                                 c_s, sc_s)):
            lst.append(v)
    return (xp.reshape(bp, seq, d), xs.reshape(bs, dec_seq, d)) + tuple(jnp.stack(l) for l in outs)
```

```python
import functools

import jax
import jax.numpy as jnp
from jax import lax
from jax.experimental import pallas as pl
from jax.experimental.pallas import tpu as pltpu

F32 = jnp.float32
BF16 = jnp.bfloat16

D_MODEL = 1024
SSD_WIDTH = 1024
SC_WIDTH = 1024
HEAD_DIM = 64
N_HEADS = SSD_WIDTH // HEAD_DIM
N_GROUPS = 2
HEADS_PER_GROUP = N_HEADS // N_GROUPS
GROUP_WIDTH = SSD_WIDTH // N_GROUPS
D_STATE = 128
SSD_CONV_W = 4
SSD_CONV_DIM = SSD_WIDTH + 2 * N_GROUPS * D_STATE
SC_CONV_W = 3
CHUNK = 256
NORM_EPS = 1e-6

LANES = 128
SUBLANES = 8

OFF_Z = 0
OFF_XBC = OFF_Z + SSD_WIDTH
OFF_DT = OFF_XBC + SSD_CONV_DIM
OFF_SCB = OFF_DT + N_HEADS
OFF_SCC = OFF_SCB + SC_WIDTH
OFF_SCH = OFF_SCC + SC_WIDTH
D_IN_PROJ = OFF_SCH + SC_WIDTH
PROJ_BLOCK = 512

VMEM_LIMIT_BYTES = 56 * 1024 * 1024


def _rmsnorm(x, w):
    ms = jnp.mean(x * x, axis=-1, keepdims=True)
    return x * lax.rsqrt(ms + NORM_EPS) * w


def _silu(x):
    half = 0.5 * x
    return half * jnp.tanh(half) + half


def _silu_mixer(x):
    return x * (0.5 * jnp.tanh(0.5 * x) + 0.5)


def _softplus(x):
    return jnp.maximum(x, 0.0) + jnp.log1p(jnp.exp(-jnp.abs(x)))


def _dot(a, b):
    return jnp.dot(a, b, preferred_element_type=F32)


def _dot_nt(a, b):
    return lax.dot_general(a, b, (((1,), (1,)), ((), ())), preferred_element_type=F32)


def _split3(x):
    hi = x.astype(BF16)
    r1 = x - hi.astype(F32)
    mid = r1.astype(BF16)
    lo = (r1 - mid.astype(F32)).astype(BF16)
    return hi, mid, lo


def _expand_heads(v, rows):
    lane = lax.broadcasted_iota(jnp.int32, (rows, LANES), 1)
    first_half = lane < HEAD_DIM
    blocks = []
    for q in range(N_HEADS // 2):
        c0 = jnp.broadcast_to(v[:, 2 * q:2 * q + 1], (rows, LANES))
        c1 = jnp.broadcast_to(v[:, 2 * q + 1:2 * q + 2], (rows, LANES))
        blocks.append(jnp.where(first_half, c0, c1))
    return jnp.concatenate(blocks, axis=1)


def _group_rmsnorm(y, w):
    outs = []
    for g in range(N_GROUPS):
        sl = slice(g * GROUP_WIDTH, (g + 1) * GROUP_WIDTH)
        outs.append(_rmsnorm(y[:, sl], w[:, sl]))
    return jnp.concatenate(outs, axis=1)


def _shift_rows(x, prev_tile, j):
    rolled = pltpu.roll(x, j, 0)
    row = lax.broadcasted_iota(jnp.int32, prev_tile.shape, 0)
    head = jnp.where(row < j, pltpu.roll(prev_tile, j, 0), rolled[0:SUBLANES])
    return jnp.concatenate([head, rolled[SUBLANES:]], axis=0)


def _causal_conv(u, prev_tile, w):
    k = w.shape[0]
    acc = u * w[k - 1:k, :]
    for j in range(1, k):
        acc = acc + _shift_rows(u, prev_tile, j) * w[k - 1 - j:k - j, :]
    return acc


WEIGHT_LOAD_STEPS = 8


def _load_weight_rows(step, w_ref, w_bf_ref):
    rows = w_ref.shape[0]
    r0 = pl.multiple_of(step * rows, rows)
    w_bf_ref[pl.ds(r0, rows), :] = w_ref[...].astype(BF16)


def _weight_block_rows(rows, steps):
    tile = 2 * SUBLANES
    return -(-rows // (steps * tile)) * tile


def _weight_rows_spec(w, steps):
    _, rows, cols = w.stacked.shape
    return pl.BlockSpec((None, _weight_block_rows(rows, steps), cols),
                        lambda i: (w.layer, jnp.minimum(i, steps - 1), 0))


def _weight_scratch(w, steps):
    rows, cols = w.shape
    return pltpu.VMEM((steps * _weight_block_rows(rows, steps), cols), BF16)


class _LayerWeight:
    def __init__(self, stacked, layer):
        self.stacked, self.layer = stacked, layer
        self.shape = stacked.shape[1:]


def _ffn_kernel(*refs, final_norm):
    if final_norm:
        (xp_ref, xs_ref, nw_ref, wg_ref, wu_ref, wd_ref, fnw_ref, op_ref, os_ref,
         wg_bf, wu_bf, wd_bf) = refs
    else:
        xp_ref, xs_ref, nw_ref, wg_ref, wu_ref, wd_ref, op_ref, os_ref, wg_bf, wu_bf, wd_bf = refs

    def half_step(x_ref, o_ref):
        x = x_ref[...]
        xn = _rmsnorm(x, nw_ref[...]).astype(BF16)
        g = _dot(xn, wg_bf[...])
        u = _dot(xn, wu_bf[...])
        hmid = (_silu(g) * u).astype(BF16)
        y = x + 0.5 * _dot(hmid, wd_bf[...])
        if final_norm:
            y = _rmsnorm(y, fnw_ref[...])
        o_ref[...] = y

    i = pl.program_id(0)
    sample_step = pl.num_programs(0) - 1

    @pl.when(i < WEIGHT_LOAD_STEPS)
    def _():
        for w_ref, w_bf in ((wg_ref, wg_bf), (wu_ref, wu_bf), (wd_ref, wd_bf)):
            _load_weight_rows(i, w_ref, w_bf)

    @pl.when((i >= WEIGHT_LOAD_STEPS) & (i < sample_step))
    def _():
        half_step(xp_ref, op_ref)

    @pl.when(i == sample_step)
    def _():
        half_step(xs_ref, os_ref)


def _resident(shape):
    return pl.BlockSpec(shape, lambda *_: (0,) * len(shape), pipeline_mode=pl.Buffered(1))


def _ffn(xp, xs, nw, wg, wu, wd, fnw=None, *, tile_m):
    m, d = xp.shape
    ms = xs.shape[0]
    dff = wg.shape[1]
    assert m % tile_m == 0
    n = m // tile_m
    final_norm = fnw is not None
    ws = WEIGHT_LOAD_STEPS
    prompt_tile = pl.BlockSpec((tile_m, d), lambda i: (jnp.clip(i - ws, 0, n - 1), 0))
    sample_tile = pl.BlockSpec((ms, d), lambda i: (0, 0))
    in_specs = [prompt_tile, sample_tile, _resident((1, d)),
                _weight_rows_spec(wg, ws), _weight_rows_spec(wu, ws), _weight_rows_spec(wd, ws)]
    args = [xp, xs, nw, wg.stacked, wu.stacked, wd.stacked]
    if final_norm:
        in_specs.append(_resident((1, d)))
        args.append(fnw)
    return pl.pallas_call(
        functools.partial(_ffn_kernel, final_norm=final_norm),
        out_shape=(jax.ShapeDtypeStruct((m, d), F32), jax.ShapeDtypeStruct((ms, d), F32)),
        grid=(ws + n + 1,),
        in_specs=in_specs,
        out_specs=(prompt_tile, sample_tile),
        scratch_shapes=[_weight_scratch(wg, ws), _weight_scratch(wu, ws), _weight_scratch(wd, ws)],
        compiler_params=pltpu.CompilerParams(
            dimension_semantics=("arbitrary",), vmem_limit_bytes=VMEM_LIMIT_BYTES),
        name="ffn_final" if final_norm else "ffn",
    )(*args)


def _in_proj(h, wint_ref, col0, width):
    return _dot_nt(h, wint_ref[col0:col0 + width, :])


def _dt_proj(h, wint_ref):
    raw = _in_proj(h, wint_ref, OFF_DT, LANES)
    lane = lax.broadcasted_iota(jnp.int32, raw.shape, 1)
    return jnp.where(lane < N_HEADS, raw, 0.0)


def _mixer_kernel(tiles_per_seq, x_ref, nw_ref, wint_ref, cw_ref, cb_ref, dtb_ref, alog_ref,
                  dskip_ref, snw_ref, scw_ref, wout_ref,
                  o_ref, sfin_ref, cfin_ref, scfin_ref,
                  st_scr, hist_scr, uhist_scr, wint_bf, wout_bf):
    i = pl.program_id(0)

    @pl.when(i < WEIGHT_LOAD_STEPS)
    def _():
        _load_weight_rows(i, wint_ref, wint_bf)
        _load_weight_rows(i, wout_ref, wout_bf)

    @pl.when(i >= WEIGHT_LOAD_STEPS)
    def _():
        c = lax.rem(i - WEIGHT_LOAD_STEPS, tiles_per_seq)
        _mixer_chunk(c == 0, c == tiles_per_seq - 1, x_ref, nw_ref, wint_bf, cw_ref,
                     cb_ref, dtb_ref, alog_ref, dskip_ref, snw_ref, scw_ref, wout_bf,
                     o_ref, sfin_ref, cfin_ref, scfin_ref, st_scr, hist_scr, uhist_scr)


def _mixer_chunk(first, last, x_ref, nw_ref, wint_ref, cw_ref, cb_ref, dtb_ref, alog_ref,
                 dskip_ref, snw_ref, scw_ref, wout_ref,
                 o_ref, sfin_ref, cfin_ref, scfin_ref,
                 st_scr, hist_scr, uhist_scr):
    L = CHUNK
    T = x_ref.shape[1]

    @pl.when(first)
    def _():
        st_scr[...] = jnp.zeros_like(st_scr)
        hist_scr[...] = jnp.zeros_like(hist_scr)
        uhist_scr[...] = jnp.zeros_like(uhist_scr)

    x = x_ref[0]
    h = _rmsnorm(x, nw_ref[...]).astype(BF16)
    dt_raw = _dt_proj(h, wint_ref)

    xbc_c_blocks = []
    for b0 in range(0, SSD_CONV_DIM, PROJ_BLOCK):
        cols = slice(b0, b0 + PROJ_BLOCK)
        xbc = _in_proj(h, wint_ref, OFF_XBC + b0, PROJ_BLOCK)
        acc = _causal_conv(xbc, hist_scr[:, cols], cw_ref[:, cols])
        hist_scr[:, cols] = xbc[T - SUBLANES:T, :]
        xbc_c_blocks.append(_silu_mixer(acc + cb_ref[:, cols]))
    cfin_ref[0] = hist_scr[SUBLANES - (SSD_CONV_W - 1):SUBLANES, :]
    xbc_c = jnp.concatenate(xbc_c_blocks, axis=1)

    dt_all = _softplus(dt_raw + dtb_ref[...])
    a_all = dt_all * (-jnp.exp(alog_ref[...]))
    row = lax.broadcasted_iota(jnp.int32, (L, L), 0)
    col = lax.broadcasted_iota(jnp.int32, (L, L), 1)
    causal = row >= col
    tril = jnp.where(causal, 1.0, 0.0).astype(BF16)
    lane = lax.broadcasted_iota(jnp.int32, (L, LANES), 1)
    first_half = lane < HEAD_DIM

    y_ssd_chunks = []
    out_sc_chunks = []
    for k in range(T // L):
        rows = slice(k * L, (k + 1) * L)
        xs = xbc_c[rows, 0:SSD_WIDTH]
        b_all = xbc_c[rows, SSD_WIDTH:SSD_WIDTH + N_GROUPS * D_STATE]
        c_all = xbc_c[rows, SSD_WIDTH + N_GROUPS * D_STATE:SSD_CONV_DIM]
        dt = dt_all[rows]
        a_hi, a_mid, a_lo = _split3(a_all[rows])
        acum = _dot(tril, a_hi) + _dot(tril, a_mid) + _dot(tril, a_lo)
        acum_t = acum.T

        bits = pltpu.bitcast(acum[0:SUBLANES, :], jnp.uint32)
        half_word = jnp.uint32(16)
        zero_row = pltpu.bitcast(
            lax.shift_right_logical(lax.shift_right_logical(bits, half_word), half_word), F32)[0:1, :]
        zero_row = jnp.concatenate([zero_row] * (D_MODEL // LANES), axis=1).astype(BF16)
        h_late = h[rows] + zero_row
        z_gate = jnp.concatenate(
            [_silu_mixer(_in_proj(h_late, wint_ref, OFF_Z + b0, PROJ_BLOCK))
             for b0 in range(0, SSD_WIDTH, PROJ_BLOCK)], axis=1)

        y_sc_blocks = []
        for b0 in range(0, SC_WIDTH, PROJ_BLOCK):
            cols = slice(b0, b0 + PROJ_BLOCK)
            scc = _in_proj(h_late, wint_ref, OFF_SCC + b0, PROJ_BLOCK)
            sch = _in_proj(h_late, wint_ref, OFF_SCH + b0, PROJ_BLOCK)
            u = scc * sch
            v = _causal_conv(u, uhist_scr[:, cols], scw_ref[:, cols])
            uhist_scr[:, cols] = u[L - SUBLANES:L, :]
            scb = _in_proj(h_late, wint_ref, OFF_SCB + b0, PROJ_BLOCK)
            y_sc_blocks.append((scb * v).astype(BF16))
        out_sc_chunks.append(_dot(jnp.concatenate(y_sc_blocks, axis=1),
                                  wout_ref[SSD_WIDTH:SSD_WIDTH + SC_WIDTH, :]))

        acum_e = _expand_heads(acum, L)
        dt_e = _expand_heads(dt, L)
        exp_acum_e = jnp.exp(acum_e)
        decay_end_e = jnp.exp(acum_e[L - 1:L, :] - acum_e)
        xdt = xs * dt_e
        xdt_b = xdt.astype(BF16)
        xdecay_b = (xdt * decay_end_e).astype(BF16)

        y_diag_blocks = []
        y_off_blocks = []
        for g in range(N_GROUPS):
            b_g = b_all[:, g * D_STATE:(g + 1) * D_STATE]
            c_g = c_all[:, g * D_STATE:(g + 1) * D_STATE]
            b_gb = b_g.astype(BF16)
            c_gb = c_g.astype(BF16)
            cb = _dot_nt(c_gb, b_gb)
            for q in range(g * HEADS_PER_GROUP // 2, (g + 1) * HEADS_PER_GROUP // 2):
                ms = []
                for hh in (2 * q, 2 * q + 1):
                    seg = acum[:, hh:hh + 1] - acum_t[hh:hh + 1, :]
                    decay = jnp.exp(jnp.where(causal, seg, -jnp.inf))
                    ms.append((cb * decay).astype(BF16))
                m_cat = jnp.concatenate(ms, axis=1)
                x2 = xdt_b[:, q * LANES:(q + 1) * LANES]
                zero = jnp.zeros_like(x2)
                rhs = jnp.concatenate([jnp.where(first_half, x2, zero),
                                       jnp.where(first_half, zero, x2)], axis=0)
                y_diag_blocks.append(_dot(m_cat, rhs))
            gs = slice(g * GROUP_WIDTH, (g + 1) * GROUP_WIDTH)
            s_enter = st_scr[:, gs]
            y_off_blocks.append(_dot(c_gb, s_enter.astype(BF16)))
            new_states = _dot(b_g.T.astype(BF16), xdecay_b[:, gs])
            st_scr[:, gs] = s_enter * exp_acum_e[L - 1:L, gs] + new_states
        y_diag = jnp.concatenate(y_diag_blocks, axis=1)
        y_off = jnp.concatenate(y_off_blocks, axis=1) * exp_acum_e

        y = (y_diag + y_off + dskip_ref[...] * xs) * z_gate
        y_ssd_chunks.append(_group_rmsnorm(y, snw_ref[...]).astype(BF16))

    scfin_ref[0] = uhist_scr[SUBLANES - (SC_CONV_W - 1):SUBLANES, :]
    out_ssd = _dot(jnp.concatenate(y_ssd_chunks, axis=0), wout_ref[0:SSD_WIDTH, :])
    o_ref[0] = x + jnp.concatenate(out_sc_chunks, axis=0) + out_ssd

    @pl.when(last)
    def _():
        sfin_ref[0] = st_scr[...].T


MIXER_CHUNKS_PER_STEP = 2


def _mixer_prompt(x, nw, wint, cw, cb, dtb, alog, dskip, snw, scw, wout):
    nb, seq, d = x.shape
    tile_t = MIXER_CHUNKS_PER_STEP * CHUNK
    assert seq % tile_t == 0
    nc = seq // tile_t
    ws = WEIGHT_LOAD_STEPS

    def chunk_map(i):
        t = jnp.maximum(i - ws, 0)
        return (t // nc, t % nc, 0)

    def seq_map(i):
        return (jnp.maximum(i - ws, 0) // nc, 0, 0)

    out_shape = (
        jax.ShapeDtypeStruct((nb, seq, d), F32),
        jax.ShapeDtypeStruct((nb, SSD_WIDTH, D_STATE), F32),
        jax.ShapeDtypeStruct((nb, SSD_CONV_W - 1, SSD_CONV_DIM), F32),
        jax.ShapeDtypeStruct((nb, SC_CONV_W - 1, SC_WIDTH), F32),
    )
    return pl.pallas_call(
        functools.partial(_mixer_kernel, nc),
        out_shape=out_shape,
        grid=(ws + nb * nc,),
        in_specs=[pl.BlockSpec((1, tile_t, d), chunk_map),
                  _resident((1, d)), _weight_rows_spec(wint, ws),
                  _resident((SSD_CONV_W, SSD_CONV_DIM)), _resident((1, SSD_CONV_DIM)),
                  _resident((1, LANES)), _resident((1, LANES)), _resident((1, SSD_WIDTH)),
                  _resident((1, SSD_WIDTH)), _resident((SC_CONV_W, SC_WIDTH)),
                  _weight_rows_spec(wout, ws)],
        out_specs=(pl.BlockSpec((1, tile_t, d), chunk_map),
                   pl.BlockSpec((1, SSD_WIDTH, D_STATE), seq_map),
                   pl.BlockSpec((1, SSD_CONV_W - 1, SSD_CONV_DIM), seq_map),
                   pl.BlockSpec((1, SC_CONV_W - 1, SC_WIDTH), seq_map)),
        scratch_shapes=[pltpu.VMEM((D_STATE, SSD_WIDTH), F32),
                        pltpu.VMEM((SUBLANES, SSD_CONV_DIM), F32),
                        pltpu.VMEM((SUBLANES, SC_WIDTH), F32),
                        _weight_scratch(wint, ws), _weight_scratch(wout, ws)],
        compiler_params=pltpu.CompilerParams(
            dimension_semantics=("arbitrary",), vmem_limit_bytes=VMEM_LIMIT_BYTES),
        name="mixer_prompt",
    )(x, nw, wint.stacked, cw, cb, dtb, alog, dskip, snw, scw, wout.stacked)


def _sample_pre_kernel(x_ref, nw_ref, wint_ref, cw_ref, cb_ref, dtb_ref, alog_ref,
                       scw_ref, cst_ref, scst_ref,
                       z_ref, xs_ref, b_ref, c_ref, xdt_blk_ref, da_ref, ysc_ref, cnew_ref, scnew_ref,
                       wint_bf):
    nb = x_ref.shape[0]
    x = x_ref[...]
    h = _rmsnorm(x, nw_ref[...]).astype(BF16)
    wint_bf[...] = wint_ref[...].astype(BF16)
    dt_raw = _dt_proj(h, wint_bf)
    xbc = _in_proj(h, wint_bf, OFF_XBC, SSD_CONV_DIM)
    z_ref[...] = _in_proj(h, wint_bf, OFF_Z, SSD_WIDTH)
    scb = _in_proj(h, wint_bf, OFF_SCB, SC_WIDTH)
    scc = _in_proj(h, wint_bf, OFF_SCC, SC_WIDTH)
    sch = _in_proj(h, wint_bf, OFF_SCH, SC_WIDTH)

    cw = cw_ref[...]
    acc = xbc * cw[SSD_CONV_W - 1:SSD_CONV_W, :]
    for k in range(SSD_CONV_W - 1):
        acc = acc + cst_ref[:, k, :] * cw[k:k + 1, :]
    for k in range(SSD_CONV_W - 2):
        cnew_ref[:, k, :] = cst_ref[:, k + 1, :]
    cnew_ref[:, SSD_CONV_W - 2, :] = xbc
    xbc_c = _silu(acc + cb_ref[...])
    xs = xbc_c[:, 0:SSD_WIDTH]
    xs_ref[...] = xs
    b_ref[...] = xbc_c[:, SSD_WIDTH:SSD_WIDTH + N_GROUPS * D_STATE]
    c_ref[...] = xbc_c[:, SSD_WIDTH + N_GROUPS * D_STATE:SSD_CONV_DIM]

    dt = _softplus(dt_raw + dtb_ref[...])
    da_ref[...] = jnp.exp(dt * (-jnp.exp(alog_ref[...])))
    xdt_t = (xs * _expand_heads(dt, nb)).T
    bb = xdt_blk_ref.shape[2]
    for j in range(nb // bb):
        xdt_blk_ref[j] = xdt_t[:, j * bb:(j + 1) * bb]

    u = scc * sch
    scw = scw_ref[...]
    v = u * scw[SC_CONV_W - 1:SC_CONV_W, :]
    for k in range(SC_CONV_W - 1):
        v = v + scst_ref[:, k, :] * scw[k:k + 1, :]
    for k in range(SC_CONV_W - 2):
        scnew_ref[:, k, :] = scst_ref[:, k + 1, :]
    scnew_ref[:, SC_CONV_W - 2, :] = u
    ysc_ref[...] = scb * v


def _sample_pre(x, nw, wint, cw, cb, dtb, alog, scw, cst, scst):
    nb, d = x.shape
    f = lambda *s: jax.ShapeDtypeStruct(s, F32)
    bb = STATE_BATCH_BLOCK
    assert nb % bb == 0
    out_shape = (f(nb, SSD_WIDTH), f(nb, SSD_WIDTH), f(nb, N_GROUPS * D_STATE), f(nb, N_GROUPS * D_STATE),
                 f(nb // bb, SSD_WIDTH, bb), f(nb, LANES), f(nb, SC_WIDTH),
                 f(nb, SSD_CONV_W - 1, SSD_CONV_DIM), f(nb, SC_CONV_W - 1, SC_WIDTH))
    return pl.pallas_call(
        _sample_pre_kernel,
        out_shape=out_shape,
        scratch_shapes=[pltpu.VMEM(wint.shape, BF16)],
        compiler_params=pltpu.CompilerParams(vmem_limit_bytes=VMEM_LIMIT_BYTES),
        name="sample_pre",
    )(x, nw, wint, cw, cb, dtb, alog, scw, cst, scst)


STATE_BATCH_BLOCK = 16


def _split2(x):
    hi = x.astype(BF16)
    return hi, (x - hi.astype(F32)).astype(BF16)


def _sample_state_kernel(da_ref, s0_ref, xdt_ref, b_ref, c_ref, snew_ref, y_ref):
    bb = STATE_BATCH_BLOCK
    blk = pl.program_id(0)
    xdt_t = xdt_ref[0]
    b_rows = b_ref[...]
    c_hi, c_lo = _split2(c_ref[...])
    row = lax.broadcasted_iota(jnp.int32, (bb, SSD_WIDTH), 0)
    y_blk = jnp.zeros((bb, SSD_WIDTH), F32)
    for i in range(bb):
        xdt_col = xdt_t[:, i:i + 1]
        y_parts = []
        for g in range(N_GROUPS):
            ns = slice(g * D_STATE, (g + 1) * D_STATE)
            b_row = b_rows[i:i + 1, ns]
            heads = []
            for hh in range(g * HEADS_PER_GROUP, (g + 1) * HEADS_PER_GROUP):
                rs = slice(hh * HEAD_DIM, (hh + 1) * HEAD_DIM)
                decay = da_ref[blk * bb + i, hh]
                heads.append(s0_ref[i, rs, :] * decay + xdt_col[rs] * b_row)
            s_new = jnp.concatenate(heads, axis=0)
            snew_ref[i, g * GROUP_WIDTH:(g + 1) * GROUP_WIDTH, :] = s_new
            s_hi, s_lo = _split2(s_new)
            lhs = jnp.concatenate([c_hi[:, ns], c_lo[:, ns]], axis=0)
            r_hi = _dot_nt(lhs, s_hi)
            r_lo = _dot_nt(c_hi[:, ns], s_lo)
            y_parts.append(r_hi[i:i + 1] + r_hi[bb + i:bb + i + 1] + r_lo[i:i + 1])
        y_row = jnp.concatenate(y_parts, axis=1)
        y_blk = jnp.where(row == i, y_row, y_blk)
    y_ref[...] = y_blk


def _sample_state(s0, xdt_blocks, decay, b_rows, c_rows):
    nb = s0.shape[0]
    nblk, _, bb = xdt_blocks.shape
    assert bb == STATE_BATCH_BLOCK and nblk * bb == nb
    return pl.pallas_call(
        _sample_state_kernel,
        out_shape=(jax.ShapeDtypeStruct((nb, SSD_WIDTH, D_STATE), F32),
                   jax.ShapeDtypeStruct((nb, SSD_WIDTH), F32)),
        grid=(nblk,),
        in_specs=[pl.BlockSpec(memory_space=pltpu.SMEM),
                  pl.BlockSpec((bb, SSD_WIDTH, D_STATE), lambda i: (i, 0, 0)),
                  pl.BlockSpec((1, SSD_WIDTH, bb), lambda i: (i, 0, 0)),
                  pl.BlockSpec((bb, N_GROUPS * D_STATE), lambda i: (i, 0)),
                  pl.BlockSpec((bb, N_GROUPS * D_STATE), lambda i: (i, 0))],
        out_specs=(pl.BlockSpec((bb, SSD_WIDTH, D_STATE), lambda i: (i, 0, 0)),
                   pl.BlockSpec((bb, SSD_WIDTH), lambda i: (i, 0))),
        compiler_params=pltpu.CompilerParams(
            dimension_semantics=("arbitrary",), vmem_limit_bytes=VMEM_LIMIT_BYTES),
        name="sample_state",
    )(decay, s0, xdt_blocks, b_rows, c_rows)


def _sample_post_kernel(x_ref, yraw_ref, xs_ref, z_ref, ysc_ref, dskip_ref, snw_ref, wout_ref, o_ref):
    y = yraw_ref[...] + dskip_ref[...] * xs_ref[...]
    y = y * _silu(z_ref[...])
    y_ssd = _group_rmsnorm(y, snw_ref[...])
    mixed = jnp.concatenate([y_ssd, ysc_ref[...]], axis=1).astype(BF16)
    o_ref[...] = x_ref[...] + _dot(mixed, wout_ref[...].astype(BF16))


def _sample_post(x, yraw, xs, z, ysc, dskip, snw, wout):
    return pl.pallas_call(
        _sample_post_kernel,
        out_shape=jax.ShapeDtypeStruct(x.shape, F32),
        compiler_params=pltpu.CompilerParams(vmem_limit_bytes=VMEM_LIMIT_BYTES),
        name="sample_post",
    )(x, yraw, xs, z, ysc, dskip, snw, wout)


FFN_TILE_M = 512


def _layer_params(i, norm_ffn1_w, ffn1_w_gate, ffn1_w_up, ffn1_w_down, norm_mix_w, w_in_t,
                  ssd_conv_w, ssd_conv_b, dt_bias, a_log, d_skip, ssd_norm_w, sconv_w, w_out,
                  norm_ffn2_w, ffn2_w_gate, ffn2_w_up, ffn2_w_down):
    pad_heads = lambda v: jnp.pad(v, (0, LANES - N_HEADS)).reshape(1, LANES)
    row = lambda v: v.reshape(1, -1)
    return dict(
        ffn1=(row(norm_ffn1_w[i]), _LayerWeight(ffn1_w_gate, i), _LayerWeight(ffn1_w_up, i),
              _LayerWeight(ffn1_w_down, i)),
        ffn2=(row(norm_ffn2_w[i]), _LayerWeight(ffn2_w_gate, i), _LayerWeight(ffn2_w_up, i),
              _LayerWeight(ffn2_w_down, i)),
        nw=row(norm_mix_w[i]), wint=_LayerWeight(w_in_t, i), cw=ssd_conv_w[i], cb=row(ssd_conv_b[i]),
        dtb=pad_heads(dt_bias[i]), alog=pad_heads(a_log[i]),
        dskip=row(jnp.repeat(d_skip[i], HEAD_DIM)), snw=row(ssd_norm_w[i]), scw=sconv_w[i],
        wout=_LayerWeight(w_out, i))


def kernel(x_prompt, x_sample, state_ssm, state_ssd_conv, state_sconv, norm_ffn1_w, ffn1_w_gate, ffn1_w_up, ffn1_w_down, norm_mix_w, w_in, ssd_conv_w, ssd_conv_b, dt_bias, a_log, d_skip, ssd_norm_w, sconv_w, w_out, norm_ffn2_w, ffn2_w_gate, ffn2_w_up, ffn2_w_down, final_norm_w):
    depth = w_in.shape[0]
    bp, seq, d = x_prompt.shape
    bs, dec_seq, _ = x_sample.shape
    assert dec_seq == 1, "sample group is one token per sequence"
    fnw = final_norm_w.reshape(1, d)
    w_in_t = jnp.swapaxes(w_in, 1, 2)

    xp = x_prompt.reshape(bp * seq, d)
    xs = x_sample.reshape(bs, d)
    outs = [[] for _ in range(6)]
    for i in range(depth):
        p = _layer_params(i, norm_ffn1_w, ffn1_w_gate, ffn1_w_up, ffn1_w_down, norm_mix_w, w_in_t,
                          ssd_conv_w, ssd_conv_b, dt_bias, a_log, d_skip, ssd_norm_w, sconv_w,
                          w_out, norm_ffn2_w, ffn2_w_gate, ffn2_w_up, ffn2_w_down)
        last = i == depth - 1
        xp, xs = _ffn(xp, xs, *p["ffn1"], tile_m=FFN_TILE_M)
        xp3, s_p, c_p, sc_p = _mixer_prompt(
            xp.reshape(bp, seq, d), p["nw"], p["wint"], p["cw"], p["cb"], p["dtb"], p["alog"],
            p["dskip"], p["snw"], p["scw"], p["wout"])
        z, xs_conv, b_rows, c_rows, xdt_t, decay, ysc, c_s, sc_s = _sample_pre(
            xs, p["nw"], w_in_t[i], p["cw"], p["cb"], p["dtb"], p["alog"], p["scw"],
            state_ssd_conv[i], state_sconv[i])
        s_s, yraw = _sample_state(state_ssm[i].reshape(bs, SSD_WIDTH, D_STATE), xdt_t, decay, b_rows, c_rows)
        xs = _sample_post(xs, yraw, xs_conv, z, ysc, p["dskip"], p["snw"], w_out[i])
        xp, xs = _ffn(xp3.reshape(bp * seq, d), xs, *p["ffn2"], fnw if last else None, tile_m=FFN_TILE_M)
        for lst, v in zip(outs, (s_p.reshape(bp, N_HEADS, HEAD_DIM, D_STATE), c_p, sc_p,
                                 s_s.reshape(bs, N_HEADS, HEAD_DIM, D_STATE),
                                 c_s, sc_s)):
            lst.append(v)
    return (xp.reshape(bp, seq, d), xs.reshape(bs, dec_seq, d)) + tuple(jnp.stack(l) for l in outs)
```

```python
import functools

import jax
import jax.numpy as jnp
from jax import lax
from jax.experimental import pallas as pl
from jax.experimental.pallas import tpu as pltpu

F32 = jnp.float32
BF16 = jnp.bfloat16

D_MODEL = 1024
SSD_WIDTH = 1024
SC_WIDTH = 1024
HEAD_DIM = 64
N_HEADS = SSD_WIDTH // HEAD_DIM
N_GROUPS = 2
HEADS_PER_GROUP = N_HEADS // N_GROUPS
GROUP_WIDTH = SSD_WIDTH // N_GROUPS
D_STATE = 128
SSD_CONV_W = 4
SSD_CONV_DIM = SSD_WIDTH + 2 * N_GROUPS * D_STATE
SC_CONV_W = 3
CHUNK = 256
NORM_EPS = 1e-6

LANES = 128
SUBLANES = 8

OFF_Z = 0
OFF_XBC = OFF_Z + SSD_WIDTH
OFF_DT = OFF_XBC + SSD_CONV_DIM
OFF_SCB = OFF_DT + N_HEADS
OFF_SCC = OFF_SCB + SC_WIDTH
OFF_SCH = OFF_SCC + SC_WIDTH
D_IN_PROJ = OFF_SCH + SC_WIDTH
PROJ_BLOCK = 512

VMEM_LIMIT_BYTES = 56 * 1024 * 1024


def _rmsnorm(x, w):
    ms = jnp.mean(x * x, axis=-1, keepdims=True)
    return x * lax.rsqrt(ms + NORM_EPS) * w


def _silu(x):
    half = 0.5 * x
    return half * jnp.tanh(half) + half


def _silu_mixer(x):
    return x * (0.5 * jnp.tanh(0.5 * x) + 0.5)


def _softplus(x):
    return jnp.maximum(x, 0.0) + jnp.log1p(jnp.exp(-jnp.abs(x)))


def _dot(a, b):
    return jnp.dot(a, b, preferred_element_type=F32)


def _dot_nt(a, b):
    return lax.dot_general(a, b, (((1,), (1,)), ((), ())), preferred_element_type=F32)


def _split3(x):
    hi = x.astype(BF16)
    r1 = x - hi.astype(F32)
    mid = r1.astype(BF16)
    lo = (r1 - mid.astype(F32)).astype(BF16)
    return hi, mid, lo


def _expand_heads(v, rows):
    lane = lax.broadcasted_iota(jnp.int32, (rows, LANES), 1)
    first_half = lane < HEAD_DIM
    blocks = []
    for q in range(N_HEADS // 2):
        c0 = jnp.broadcast_to(v[:, 2 * q:2 * q + 1], (rows, LANES))
        c1 = jnp.broadcast_to(v[:, 2 * q + 1:2 * q + 2], (rows, LANES))
        blocks.append(jnp.where(first_half, c0, c1))
    return jnp.concatenate(blocks, axis=1)


def _group_rmsnorm(y, w):
    outs = []
    for g in range(N_GROUPS):
        sl = slice(g * GROUP_WIDTH, (g + 1) * GROUP_WIDTH)
        outs.append(_rmsnorm(y[:, sl], w[:, sl]))
    return jnp.concatenate(outs, axis=1)


def _shift_rows(x, prev_tile, j):
    rolled = pltpu.roll(x, j, 0)
    row = lax.broadcasted_iota(jnp.int32, prev_tile.shape, 0)
    head = jnp.where(row < j, pltpu.roll(prev_tile, j, 0), rolled[0:SUBLANES])
    return jnp.concatenate([head, rolled[SUBLANES:]], axis=0)


def _causal_conv(u, prev_tile, w):
    k = w.shape[0]
    acc = u * w[k - 1:k, :]
    for j in range(1, k):
        acc = acc + _shift_rows(u, prev_tile, j) * w[k - 1 - j:k - j, :]
    return acc


WEIGHT_LOAD_STEPS = 8


def _load_weight_rows(step, w_ref, w_bf_ref):
    rows = w_ref.shape[0]
    r0 = pl.multiple_of(step * rows, rows)
    w_bf_ref[pl.ds(r0, rows), :] = w_ref[...].astype(BF16)


def _weight_block_rows(rows, steps):
    tile = 2 * SUBLANES
    return -(-rows // (steps * tile)) * tile


def _weight_rows_spec(w, steps):
    _, rows, cols = w.stacked.shape
    return pl.BlockSpec((None, _weight_block_rows(rows, steps), cols),
                        lambda i: (w.layer, jnp.minimum(i, steps - 1), 0))


def _weight_scratch(w, steps):
    rows, cols = w.shape
    return pltpu.VMEM((steps * _weight_block_rows(rows, steps), cols), BF16)


class _LayerWeight:
    def __init__(self, stacked, layer):
        self.stacked, self.layer = stacked, layer
        self.shape = stacked.shape[1:]


def _ffn_kernel(*refs, final_norm):
    if final_norm:
        (xp_ref, xs_ref, nw_ref, wg_ref, wu_ref, wd_ref, fnw_ref, op_ref, os_ref,
         wg_bf, wu_bf, wd_bf) = refs
    else:
        xp_ref, xs_ref, nw_ref, wg_ref, wu_ref, wd_ref, op_ref, os_ref, wg_bf, wu_bf, wd_bf = refs

    def half_step(x_ref, o_ref):
        rows = x_ref.shape[0]
        sub = min(rows, FFN_SUB_TILE_M)
        for r0 in range(0, rows, sub):
            x = x_ref[r0:r0 + sub, :]
            xn = _rmsnorm(x, nw_ref[...]).astype(BF16)
            g = _dot(xn, wg_bf[...])
            u = _dot(xn, wu_bf[...])
            hmid = (_silu(g) * u).astype(BF16)
            y = x + 0.5 * _dot(hmid, wd_bf[...])
            if final_norm:
                y = _rmsnorm(y, fnw_ref[...])
            o_ref[r0:r0 + sub, :] = y

    i = pl.program_id(0)
    sample_step = pl.num_programs(0) - 1

    @pl.when(i < WEIGHT_LOAD_STEPS)
    def _():
        for w_ref, w_bf in ((wg_ref, wg_bf), (wu_ref, wu_bf), (wd_ref, wd_bf)):
            _load_weight_rows(i, w_ref, w_bf)

    @pl.when((i >= WEIGHT_LOAD_STEPS) & (i < sample_step))
    def _():
        half_step(xp_ref, op_ref)

    @pl.when(i == sample_step)
    def _():
        half_step(xs_ref, os_ref)


def _resident(shape):
    return pl.BlockSpec(shape, lambda *_: (0,) * len(shape), pipeline_mode=pl.Buffered(1))


def _ffn(xp, xs, nw, wg, wu, wd, fnw=None, *, tile_m):
    m, d = xp.shape
    ms = xs.shape[0]
    dff = wg.shape[1]
    assert m % tile_m == 0
    n = m // tile_m
    final_norm = fnw is not None
    ws = WEIGHT_LOAD_STEPS
    prompt_tile = pl.BlockSpec((tile_m, d), lambda i: (jnp.clip(i - ws, 0, n - 1), 0))
    sample_tile = pl.BlockSpec((ms, d), lambda i: (0, 0))
    in_specs = [prompt_tile, sample_tile, _resident((1, d)),
                _weight_rows_spec(wg, ws), _weight_rows_spec(wu, ws), _weight_rows_spec(wd, ws)]
    args = [xp, xs, nw, wg.stacked, wu.stacked, wd.stacked]
    if final_norm:
        in_specs.append(_resident((1, d)))
        args.append(fnw)
    return pl.pallas_call(
        functools.partial(_ffn_kernel, final_norm=final_norm),
        out_shape=(jax.ShapeDtypeStruct((m, d), F32), jax.ShapeDtypeStruct((ms, d), F32)),
        grid=(ws + n + 1,),
        in_specs=in_specs,
        out_specs=(prompt_tile, sample_tile),
        scratch_shapes=[_weight_scratch(wg, ws), _weight_scratch(wu, ws), _weight_scratch(wd, ws)],
        compiler_params=pltpu.CompilerParams(
            dimension_semantics=("arbitrary",), vmem_limit_bytes=VMEM_LIMIT_BYTES),
        name="ffn_final" if final_norm else "ffn",
    )(*args)


def _in_proj(h, wint_ref, col0, width):
    return _dot_nt(h, wint_ref[col0:col0 + width, :])


def _dt_proj(h, wint_ref):
    raw = _in_proj(h, wint_ref, OFF_DT, LANES)
    lane = lax.broadcasted_iota(jnp.int32, raw.shape, 1)
    return jnp.where(lane < N_HEADS, raw, 0.0)


def _mixer_kernel(tiles_per_seq, x_ref, nw_ref, wint_ref, cw_ref, cb_ref, dtb_ref, alog_ref,
                  dskip_ref, snw_ref, scw_ref, wout_ref,
                  o_ref, sfin_ref, cfin_ref, scfin_ref,
                  st_scr, hist_scr, uhist_scr, wint_bf, wout_bf):
    i = pl.program_id(0)

    @pl.when(i < MIXER_WEIGHT_LOAD_STEPS)
    def _():
        _load_weight_rows(i, wint_ref, wint_bf)
        _load_weight_rows(i, wout_ref, wout_bf)

    @pl.when(i >= MIXER_WEIGHT_LOAD_STEPS)
    def _():
        c = lax.rem(i - MIXER_WEIGHT_LOAD_STEPS, tiles_per_seq)
        _mixer_chunk(c == 0, c == tiles_per_seq - 1, x_ref, nw_ref, wint_bf, cw_ref,
                     cb_ref, dtb_ref, alog_ref, dskip_ref, snw_ref, scw_ref, wout_bf,
                     o_ref, sfin_ref, cfin_ref, scfin_ref, st_scr, hist_scr, uhist_scr)


def _mixer_chunk(first, last, x_ref, nw_ref, wint_ref, cw_ref, cb_ref, dtb_ref, alog_ref,
                 dskip_ref, snw_ref, scw_ref, wout_ref,
                 o_ref, sfin_ref, cfin_ref, scfin_ref,
                 st_scr, hist_scr, uhist_scr):
    L = CHUNK
    T = x_ref.shape[1]

    @pl.when(first)
    def _():
        st_scr[...] = jnp.zeros_like(st_scr)
        hist_scr[...] = jnp.zeros_like(hist_scr)
        uhist_scr[...] = jnp.zeros_like(uhist_scr)

    x = x_ref[0]
    h = _rmsnorm(x, nw_ref[...]).astype(BF16)
    dt_raw = _dt_proj(h, wint_ref)

    xbc_c_blocks = []
    for b0 in range(0, SSD_CONV_DIM, PROJ_BLOCK):
        cols = slice(b0, b0 + PROJ_BLOCK)
        xbc = _in_proj(h, wint_ref, OFF_XBC + b0, PROJ_BLOCK)
        acc = _causal_conv(xbc, hist_scr[:, cols], cw_ref[:, cols])
        hist_scr[:, cols] = xbc[T - SUBLANES:T, :]
        xbc_c_blocks.append(_silu_mixer(acc + cb_ref[:, cols]))
    cfin_ref[0] = hist_scr[SUBLANES - (SSD_CONV_W - 1):SUBLANES, :]
    xbc_c_all = jnp.concatenate(xbc_c_blocks, axis=1)

    def plus_zero_of(v, gate):
        bits = pltpu.bitcast(gate[0:SUBLANES, 0:LANES], jnp.uint32)
        half_word = jnp.uint32(16)
        zero_row = pltpu.bitcast(
            lax.shift_right_logical(lax.shift_right_logical(bits, half_word), half_word), F32)[0:1, :]
        return v + jnp.concatenate([zero_row] * (D_MODEL // LANES), axis=1).astype(BF16)

    dt_all = _softplus(dt_raw + dtb_ref[...])
    a_all = dt_all * (-jnp.exp(alog_ref[...]))
    row = lax.broadcasted_iota(jnp.int32, (L, L), 0)
    col = lax.broadcasted_iota(jnp.int32, (L, L), 1)
    causal = row >= col
    tril = jnp.where(causal, 1.0, 0.0).astype(BF16)
    lane = lax.broadcasted_iota(jnp.int32, (L, LANES), 1)
    first_half = lane < HEAD_DIM

    y_ssd_chunks = []
    out_sc_chunks = []
    for k in range(T // L):
        rows = slice(k * L, (k + 1) * L)
        xbc_c = xbc_c_all[rows]
        xs = xbc_c[:, 0:SSD_WIDTH]
        b_all = xbc_c[:, SSD_WIDTH:SSD_WIDTH + N_GROUPS * D_STATE]
        c_all = xbc_c[:, SSD_WIDTH + N_GROUPS * D_STATE:SSD_CONV_DIM]
        dt = dt_all[rows]
        a_hi, a_mid, a_lo = _split3(a_all[rows])
        acum = _dot(tril, a_hi) + _dot(tril, a_mid) + _dot(tril, a_lo)
        acum_t = acum.T

        def h_after(gate):
            return plus_zero_of(h[rows], gate)

        def z_gate_after(gate):
            h_late = h_after(gate)
            return jnp.concatenate(
                [_silu_mixer(_in_proj(h_late, wint_ref, OFF_Z + b0, PROJ_BLOCK))
                 for b0 in range(0, SSD_WIDTH, PROJ_BLOCK)], axis=1)

        def short_conv_block_after(gate, b0):
            h_late = h_after(gate)
            cols = slice(b0, b0 + PROJ_BLOCK)
            scc = _in_proj(h_late, wint_ref, OFF_SCC + b0, PROJ_BLOCK)
            sch = _in_proj(h_late, wint_ref, OFF_SCH + b0, PROJ_BLOCK)
            u = scc * sch
            v = _causal_conv(u, uhist_scr[:, cols], scw_ref[:, cols])
            uhist_scr[:, cols] = u[L - SUBLANES:L, :]
            scb = _in_proj(h_late, wint_ref, OFF_SCB + b0, PROJ_BLOCK)
            return (scb * v).astype(BF16)

        z_gate = z_gate_after(acum)
        y_sc = jnp.concatenate(
            [short_conv_block_after(acum, b0) for b0 in range(0, SC_WIDTH, PROJ_BLOCK)], axis=1)
        out_sc_chunks.append(_dot(y_sc, wout_ref[SSD_WIDTH:SSD_WIDTH + SC_WIDTH, :]))

        acum_e = _expand_heads(acum, L)
        dt_e = _expand_heads(dt, L)
        exp_acum_e = jnp.exp(acum_e)
        decay_end_e = jnp.exp(acum_e[L - 1:L, :] - acum_e)
        xdt = xs * dt_e
        xdt_b = xdt.astype(BF16)
        xdecay_b = (xdt * decay_end_e).astype(BF16)

        y_diag_blocks = []
        y_off_blocks = []
        for g in range(N_GROUPS):
            b_g = b_all[:, g * D_STATE:(g + 1) * D_STATE]
            c_g = c_all[:, g * D_STATE:(g + 1) * D_STATE]
            b_gb = b_g.astype(BF16)
            c_gb = c_g.astype(BF16)
            cb = _dot_nt(c_gb, b_gb)
            for q in range(g * HEADS_PER_GROUP // 2, (g + 1) * HEADS_PER_GROUP // 2):
                ms = []
                for hh in (2 * q, 2 * q + 1):
                    seg = acum[:, hh:hh + 1] - acum_t[hh:hh + 1, :]
                    decay = jnp.exp(jnp.where(causal, seg, -jnp.inf))
                    ms.append((cb * decay).astype(BF16))
                m_cat = jnp.concatenate(ms, axis=1)
                x2 = xdt_b[:, q * LANES:(q + 1) * LANES]
                zero = jnp.zeros_like(x2)
                rhs = jnp.concatenate([jnp.where(first_half, x2, zero),
                                       jnp.where(first_half, zero, x2)], axis=0)
                y_diag_blocks.append(_dot(m_cat, rhs))
            gs = slice(g * GROUP_WIDTH, (g + 1) * GROUP_WIDTH)
            s_enter = st_scr[:, gs]
            y_off_blocks.append(_dot(c_gb, s_enter.astype(BF16)))
            new_states = _dot(b_g.T.astype(BF16), xdecay_b[:, gs])
            st_scr[:, gs] = s_enter * exp_acum_e[L - 1:L, gs] + new_states
        y_diag = jnp.concatenate(y_diag_blocks, axis=1)
        y_off = jnp.concatenate(y_off_blocks, axis=1) * exp_acum_e

        y = (y_diag + y_off + dskip_ref[...] * xs) * z_gate
        y_ssd_chunks.append(_group_rmsnorm(y, snw_ref[...]).astype(BF16))

    scfin_ref[0] = uhist_scr[SUBLANES - (SC_CONV_W - 1):SUBLANES, :]
    out_ssd = _dot(jnp.concatenate(y_ssd_chunks, axis=0), wout_ref[0:SSD_WIDTH, :])
    o_ref[0] = x + jnp.concatenate(out_sc_chunks, axis=0) + out_ssd

    @pl.when(last)
    def _():
        sfin_ref[0] = st_scr[...].T


MIXER_CHUNKS_PER_STEP = 2
MIXER_WEIGHT_LOAD_STEPS = WEIGHT_LOAD_STEPS


def _mixer_prompt(x, nw, wint, cw, cb, dtb, alog, dskip, snw, scw, wout):
    nb, seq, d = x.shape
    tile_t = MIXER_CHUNKS_PER_STEP * CHUNK
    assert seq % tile_t == 0
    nc = seq // tile_t
    ws = MIXER_WEIGHT_LOAD_STEPS

    def chunk_map(i):
        t = jnp.maximum(i - ws, 0)
        return (t // nc, t % nc, 0)

    def seq_map(i):
        return (jnp.maximum(i - ws, 0) // nc, 0, 0)

    out_shape = (
        jax.ShapeDtypeStruct((nb, seq, d), F32),
        jax.ShapeDtypeStruct((nb, SSD_WIDTH, D_STATE), F32),
        jax.ShapeDtypeStruct((nb, SSD_CONV_W - 1, SSD_CONV_DIM), F32),
        jax.ShapeDtypeStruct((nb, SC_CONV_W - 1, SC_WIDTH), F32),
    )
    return pl.pallas_call(
        functools.partial(_mixer_kernel, nc),
        out_shape=out_shape,
        grid=(ws + nb * nc,),
        in_specs=[pl.BlockSpec((1, tile_t, d), chunk_map),
                  _resident((1, d)), _weight_rows_spec(wint, ws),
                  _resident((SSD_CONV_W, SSD_CONV_DIM)), _resident((1, SSD_CONV_DIM)),
                  _resident((1, LANES)), _resident((1, LANES)), _resident((1, SSD_WIDTH)),
                  _resident((1, SSD_WIDTH)), _resident((SC_CONV_W, SC_WIDTH)),
                  _weight_rows_spec(wout, ws)],
        out_specs=(pl.BlockSpec((1, tile_t, d), chunk_map),
                   pl.BlockSpec((1, SSD_WIDTH, D_STATE), seq_map),
                   pl.BlockSpec((1, SSD_CONV_W - 1, SSD_CONV_DIM), seq_map),
                   pl.BlockSpec((1, SC_CONV_W - 1, SC_WIDTH), seq_map)),
        scratch_shapes=[pltpu.VMEM((D_STATE, SSD_WIDTH), F32),
                        pltpu.VMEM((SUBLANES, SSD_CONV_DIM), F32),
                        pltpu.VMEM((SUBLANES, SC_WIDTH), F32),
                        _weight_scratch(wint, ws), _weight_scratch(wout, ws)],
        compiler_params=pltpu.CompilerParams(
            dimension_semantics=("arbitrary",), vmem_limit_bytes=VMEM_LIMIT_BYTES),
        name="mixer_prompt",
    )(x, nw, wint.stacked, cw, cb, dtb, alog, dskip, snw, scw, wout.stacked)


def _sample_pre_kernel(x_ref, nw_ref, wint_ref, cw_ref, cb_ref, dtb_ref, alog_ref,
                       scw_ref, cst_ref, scst_ref,
                       z_ref, xs_ref, b_ref, c_ref, xdt_blk_ref, da_ref, ysc_ref, cnew_ref, scnew_ref,
                       wint_bf):
    nb = x_ref.shape[0]
    x = x_ref[...]
    h = _rmsnorm(x, nw_ref[...]).astype(BF16)
    wint_bf[...] = wint_ref[...].astype(BF16)
    dt_raw = _dt_proj(h, wint_bf)
    xbc = _in_proj(h, wint_bf, OFF_XBC, SSD_CONV_DIM)
    z_ref[...] = _in_proj(h, wint_bf, OFF_Z, SSD_WIDTH)
    scb = _in_proj(h, wint_bf, OFF_SCB, SC_WIDTH)
    scc = _in_proj(h, wint_bf, OFF_SCC, SC_WIDTH)
    sch = _in_proj(h, wint_bf, OFF_SCH, SC_WIDTH)

    cw = cw_ref[...]
    acc = xbc * cw[SSD_CONV_W - 1:SSD_CONV_W, :]
    for k in range(SSD_CONV_W - 1):
        acc = acc + cst_ref[:, k, :] * cw[k:k + 1, :]
    for k in range(SSD_CONV_W - 2):
        cnew_ref[:, k, :] = cst_ref[:, k + 1, :]
    cnew_ref[:, SSD_CONV_W - 2, :] = xbc
    xbc_c = _silu(acc + cb_ref[...])
    xs = xbc_c[:, 0:SSD_WIDTH]
    xs_ref[...] = xs
    b_ref[...] = xbc_c[:, SSD_WIDTH:SSD_WIDTH + N_GROUPS * D_STATE]
    c_ref[...] = xbc_c[:, SSD_WIDTH + N_GROUPS * D_STATE:SSD_CONV_DIM]

    dt = _softplus(dt_raw + dtb_ref[...])
    da_ref[...] = jnp.exp(dt * (-jnp.exp(alog_ref[...])))
    xdt_t = (xs * _expand_heads(dt, nb)).T
    bb = xdt_blk_ref.shape[2]
    for j in range(nb // bb):
        xdt_blk_ref[j] = xdt_t[:, j * bb:(j + 1) * bb]

    u = scc * sch
    scw = scw_ref[...]
    v = u * scw[SC_CONV_W - 1:SC_CONV_W, :]
    for k in range(SC_CONV_W - 1):
        v = v + scst_ref[:, k, :] * scw[k:k + 1, :]
    for k in range(SC_CONV_W - 2):
        scnew_ref[:, k, :] = scst_ref[:, k + 1, :]
    scnew_ref[:, SC_CONV_W - 2, :] = u
    ysc_ref[...] = scb * v


def _sample_pre(x, nw, wint, cw, cb, dtb, alog, scw, cst, scst):
    nb, d = x.shape
    f = lambda *s: jax.ShapeDtypeStruct(s, F32)
    bb = STATE_BATCH_BLOCK
    assert nb % bb == 0
    out_shape = (f(nb, SSD_WIDTH), f(nb, SSD_WIDTH), f(nb, N_GROUPS * D_STATE), f(nb, N_GROUPS * D_STATE),
                 f(nb // bb, SSD_WIDTH, bb), f(nb, LANES), f(nb, SC_WIDTH),
                 f(nb, SSD_CONV_W - 1, SSD_CONV_DIM), f(nb, SC_CONV_W - 1, SC_WIDTH))
    return pl.pallas_call(
        _sample_pre_kernel,
        out_shape=out_shape,
        scratch_shapes=[pltpu.VMEM(wint.shape, BF16)],
        compiler_params=pltpu.CompilerParams(vmem_limit_bytes=VMEM_LIMIT_BYTES),
        name="sample_pre",
    )(x, nw, wint, cw, cb, dtb, alog, scw, cst, scst)


STATE_BATCH_BLOCK = 16


def _split2(x):
    hi = x.astype(BF16)
    return hi, (x - hi.astype(F32)).astype(BF16)


def _sample_state_kernel(da_ref, s0_ref, xdt_ref, b_ref, c_ref, snew_ref, y_ref):
    bb = STATE_BATCH_BLOCK
    blk = pl.program_id(0)
    xdt_t = xdt_ref[0]
    b_rows = b_ref[...]
    c_hi, c_lo = _split2(c_ref[...])
    row = lax.broadcasted_iota(jnp.int32, (bb, SSD_WIDTH), 0)
    y_blk = jnp.zeros((bb, SSD_WIDTH), F32)
    for i in range(bb):
        xdt_col = xdt_t[:, i:i + 1]
        y_parts = []
        for g in range(N_GROUPS):
            ns = slice(g * D_STATE, (g + 1) * D_STATE)
            b_row = b_rows[i:i + 1, ns]
            heads = []
            for hh in range(g * HEADS_PER_GROUP, (g + 1) * HEADS_PER_GROUP):
                rs = slice(hh * HEAD_DIM, (hh + 1) * HEAD_DIM)
                decay = da_ref[blk * bb + i, hh]
                heads.append(s0_ref[i, rs, :] * decay + xdt_col[rs] * b_row)
            s_new = jnp.concatenate(heads, axis=0)
            snew_ref[i, g * GROUP_WIDTH:(g + 1) * GROUP_WIDTH, :] = s_new
            s_hi, s_lo = _split2(s_new)
            lhs = jnp.concatenate([c_hi[:, ns], c_lo[:, ns]], axis=0)
            r_hi = _dot_nt(lhs, s_hi)
            r_lo = _dot_nt(c_hi[:, ns], s_lo)
            y_parts.append(r_hi[i:i + 1] + r_hi[bb + i:bb + i + 1] + r_lo[i:i + 1])
        y_row = jnp.concatenate(y_parts, axis=1)
        y_blk = jnp.where(row == i, y_row, y_blk)
    y_ref[...] = y_blk


def _sample_state(s0, xdt_blocks, decay, b_rows, c_rows):
    nb = s0.shape[0]
    nblk, _, bb = xdt_blocks.shape
    assert bb == STATE_BATCH_BLOCK and nblk * bb == nb
    return pl.pallas_call(
        _sample_state_kernel,
        out_shape=(jax.ShapeDtypeStruct((nb, SSD_WIDTH, D_STATE), F32),
                   jax.ShapeDtypeStruct((nb, SSD_WIDTH), F32)),
        grid=(nblk,),
        in_specs=[pl.BlockSpec(memory_space=pltpu.SMEM),
                  pl.BlockSpec((bb, SSD_WIDTH, D_STATE), lambda i: (i, 0, 0)),
                  pl.BlockSpec((1, SSD_WIDTH, bb), lambda i: (i, 0, 0)),
                  pl.BlockSpec((bb, N_GROUPS * D_STATE), lambda i: (i, 0)),
                  pl.BlockSpec((bb, N_GROUPS * D_STATE), lambda i: (i, 0))],
        out_specs=(pl.BlockSpec((bb, SSD_WIDTH, D_STATE), lambda i: (i, 0, 0)),
                   pl.BlockSpec((bb, SSD_WIDTH), lambda i: (i, 0))),
        compiler_params=pltpu.CompilerParams(
            dimension_semantics=("arbitrary",), vmem_limit_bytes=VMEM_LIMIT_BYTES),
        name="sample_state",
    )(decay, s0, xdt_blocks, b_rows, c_rows)


def _sample_post_kernel(x_ref, yraw_ref, xs_ref, z_ref, ysc_ref, dskip_ref, snw_ref, wout_ref, o_ref):
    y = yraw_ref[...] + dskip_ref[...] * xs_ref[...]
    y = y * _silu(z_ref[...])
    y_ssd = _group_rmsnorm(y, snw_ref[...])
    mixed = jnp.concatenate([y_ssd, ysc_ref[...]], axis=1).astype(BF16)
    o_ref[...] = x_ref[...] + _dot(mixed, wout_ref[...].astype(BF16))


def _sample_post(x, yraw, xs, z, ysc, dskip, snw, wout):
    return pl.pallas_call(
        _sample_post_kernel,
        out_shape=jax.ShapeDtypeStruct(x.shape, F32),
        compiler_params=pltpu.CompilerParams(vmem_limit_bytes=VMEM_LIMIT_BYTES),
        name="sample_post",
    )(x, yraw, xs, z, ysc, dskip, snw, wout)


FFN_TILE_M = 1024
FFN_SUB_TILE_M = 512


def _layer_params(i, norm_ffn1_w, ffn1_w_gate, ffn1_w_up, ffn1_w_down, norm_mix_w, w_in_t,
                  ssd_conv_w, ssd_conv_b, dt_bias, a_log, d_skip, ssd_norm_w, sconv_w, w_out,
                  norm_ffn2_w, ffn2_w_gate, ffn2_w_up, ffn2_w_down):
    pad_heads = lambda v: jnp.pad(v, (0, LANES - N_HEADS)).reshape(1, LANES)
    row = lambda v: v.reshape(1, -1)
    return dict(
        ffn1=(row(norm_ffn1_w[i]), _LayerWeight(ffn1_w_gate, i), _LayerWeight(ffn1_w_up, i),
              _LayerWeight(ffn1_w_down, i)),
        ffn2=(row(norm_ffn2_w[i]), _LayerWeight(ffn2_w_gate, i), _LayerWeight(ffn2_w_up, i),
              _LayerWeight(ffn2_w_down, i)),
        nw=row(norm_mix_w[i]), wint=_LayerWeight(w_in_t, i), cw=ssd_conv_w[i], cb=row(ssd_conv_b[i]),
        dtb=pad_heads(dt_bias[i]), alog=pad_heads(a_log[i]),
        dskip=row(jnp.repeat(d_skip[i], HEAD_DIM)), snw=row(ssd_norm_w[i]), scw=sconv_w[i],
        wout=_LayerWeight(w_out, i))


def kernel(x_prompt, x_sample, state_ssm, state_ssd_conv, state_sconv, norm_ffn1_w, ffn1_w_gate, ffn1_w_up, ffn1_w_down, norm_mix_w, w_in, ssd_conv_w, ssd_conv_b, dt_bias, a_log, d_skip, ssd_norm_w, sconv_w, w_out, norm_ffn2_w, ffn2_w_gate, ffn2_w_up, ffn2_w_down, final_norm_w):
    depth = w_in.shape[0]
    bp, seq, d = x_prompt.shape
    bs, dec_seq, _ = x_sample.shape
    assert dec_seq == 1, "sample group is one token per sequence"
    fnw = final_norm_w.reshape(1, d)
    w_in_t = jnp.swapaxes(w_in, 1, 2)

    xp = x_prompt.reshape(bp * seq, d)
    xs = x_sample.reshape(bs, d)
    outs = [[] for _ in range(6)]
    for i in range(depth):
        p = _layer_params(i, norm_ffn1_w, ffn1_w_gate, ffn1_w_up, ffn1_w_down, norm_mix_w, w_in_t,
                          ssd_conv_w, ssd_conv_b, dt_bias, a_log, d_skip, ssd_norm_w, sconv_w,
                          w_out, norm_ffn2_w, ffn2_w_gate, ffn2_w_up, ffn2_w_down)
        last = i == depth - 1
        xp, xs = _ffn(xp, xs, *p["ffn1"], tile_m=FFN_TILE_M)
        xp3, s_p, c_p, sc_p = _mixer_prompt(
            xp.reshape(bp, seq, d), p["nw"], p["wint"], p["cw"], p["cb"], p["dtb"], p["alog"],
            p["dskip"], p["snw"], p["scw"], p["wout"])
        z, xs_conv, b_rows, c_rows, xdt_t, decay, ysc, c_s, sc_s = _sample_pre(
            xs, p["nw"], w_in_t[i], p["cw"], p["cb"], p["dtb"], p["alog"], p["scw"],
            state_ssd_conv[i], state_sconv[i])
        s_s, yraw = _sample_state(state_ssm[i].reshape(bs, SSD_WIDTH, D_STATE), xdt_t, decay, b_rows, c_rows)
        xs = _sample_post(xs, yraw, xs_conv, z, ysc, p["dskip"], p["snw"], w_out[i])
        xp, xs = _ffn(xp3.reshape(bp * seq, d), xs, *p["ffn2"], fnw if last else None, tile_m=FFN_TILE_M)
        for lst, v in zip(outs, (s_p.reshape(bp, N_HEADS, HEAD_DIM, D_STATE), c_p, sc_p,
                                 s_s.reshape(bs, N_HEADS, HEAD_DIM, D_STATE),
                                 c_s, sc_s)):
            lst.append(v)
    return (xp.reshape(bp, seq, d), xs.reshape(bs, dec_seq, d)) + tuple(jnp.stack(l) for l in outs)
```

```python
import functools

import jax
import jax.numpy as jnp
from jax import lax
from jax.experimental import pallas as pl
from jax.experimental.pallas import tpu as pltpu

F32 = jnp.float32
BF16 = jnp.bfloat16

D_MODEL = 1024
SSD_WIDTH = 1024
SC_WIDTH = 1024
HEAD_DIM = 64
N_HEADS = SSD_WIDTH // HEAD_DIM
N_GROUPS = 2
HEADS_PER_GROUP = N_HEADS // N_GROUPS
GROUP_WIDTH = SSD_WIDTH // N_GROUPS
D_STATE = 128
SSD_CONV_W = 4
SSD_CONV_DIM = SSD_WIDTH + 2 * N_GROUPS * D_STATE
SC_CONV_W = 3
CHUNK = 256
NORM_EPS = 1e-6

LANES = 128
SUBLANES = 8

OFF_Z = 0
OFF_XBC = OFF_Z + SSD_WIDTH
OFF_DT = OFF_XBC + SSD_CONV_DIM
OFF_SCB = OFF_DT + N_HEADS
OFF_SCC = OFF_SCB + SC_WIDTH
OFF_SCH = OFF_SCC + SC_WIDTH
D_IN_PROJ = OFF_SCH + SC_WIDTH
PROJ_BLOCK = 512

VMEM_LIMIT_BYTES = 56 * 1024 * 1024


def _rmsnorm(x, w):
    ms = jnp.mean(x * x, axis=-1, keepdims=True)
    return x * lax.rsqrt(ms + NORM_EPS) * w


def _silu(x):
    half = 0.5 * x
    return half * jnp.tanh(half) + half


def _silu_mixer(x):
    return x * (0.5 * jnp.tanh(0.5 * x) + 0.5)


def _softplus(x):
    return jnp.maximum(x, 0.0) + jnp.log1p(jnp.exp(-jnp.abs(x)))


def _dot(a, b):
    return jnp.dot(a, b, preferred_element_type=F32)


def _dot_nt(a, b):
    return lax.dot_general(a, b, (((1,), (1,)), ((), ())), preferred_element_type=F32)


def _split3(x):
    hi = x.astype(BF16)
    r1 = x - hi.astype(F32)
    mid = r1.astype(BF16)
    lo = (r1 - mid.astype(F32)).astype(BF16)
    return hi, mid, lo


def _expand_heads(v, rows):
    lane = lax.broadcasted_iota(jnp.int32, (rows, LANES), 1)
    first_half = lane < HEAD_DIM
    blocks = []
    for q in range(N_HEADS // 2):
        c0 = jnp.broadcast_to(v[:, 2 * q:2 * q + 1], (rows, LANES))
        c1 = jnp.broadcast_to(v[:, 2 * q + 1:2 * q + 2], (rows, LANES))
        blocks.append(jnp.where(first_half, c0, c1))
    return jnp.concatenate(blocks, axis=1)


def _group_rmsnorm(y, w):
    outs = []
    for g in range(N_GROUPS):
        sl = slice(g * GROUP_WIDTH, (g + 1) * GROUP_WIDTH)
        outs.append(_rmsnorm(y[:, sl], w[:, sl]))
    return jnp.concatenate(outs, axis=1)


def _shift_rows(x, prev_tile, j):
    rolled = pltpu.roll(x, j, 0)
    row = lax.broadcasted_iota(jnp.int32, prev_tile.shape, 0)
    head = jnp.where(row < j, pltpu.roll(prev_tile, j, 0), rolled[0:SUBLANES])
    return jnp.concatenate([head, rolled[SUBLANES:]], axis=0)


def _causal_conv(u, prev_tile, w):
    k = w.shape[0]
    acc = u * w[k - 1:k, :]
    for j in range(1, k):
        acc = acc + _shift_rows(u, prev_tile, j) * w[k - 1 - j:k - j, :]
    return acc


WEIGHT_LOAD_STEPS = 8


def _load_weight_rows(step, w_ref, w_bf_ref, export_ref=None):
    rows = w_ref.shape[0]
    r0 = pl.multiple_of(step * rows, rows)
    block = w_ref[...].astype(BF16)
    w_bf_ref[pl.ds(r0, rows), :] = block
    if export_ref is not None:
        export_ref[...] = block


def _weight_block_rows(rows, steps):
    tile = 2 * SUBLANES
    return -(-rows // (steps * tile)) * tile


def _weight_rows_spec(w, steps):
    _, rows, cols = w.stacked.shape
    return pl.BlockSpec((None, _weight_block_rows(rows, steps), cols),
                        lambda i: (w.layer, jnp.minimum(i, steps - 1), 0))


def _weight_scratch(w, steps):
    rows, cols = w.shape
    return pltpu.VMEM((steps * _weight_block_rows(rows, steps), cols), BF16)


class _LayerWeight:
    def __init__(self, stacked, layer):
        self.stacked, self.layer = stacked, layer
        self.shape = stacked.shape[1:]


def _ffn_kernel(*refs, final_norm):
    if final_norm:
        (xp_ref, xs_ref, nw_ref, wg_ref, wu_ref, wd_ref, fnw_ref, op_ref, os_ref,
         wg_bf, wu_bf, wd_bf) = refs
    else:
        xp_ref, xs_ref, nw_ref, wg_ref, wu_ref, wd_ref, op_ref, os_ref, wg_bf, wu_bf, wd_bf = refs

    def half_step(x_ref, o_ref):
        rows = x_ref.shape[0]
        sub = min(rows, FFN_SUB_TILE_M)
        for r0 in range(0, rows, sub):
            x = x_ref[r0:r0 + sub, :]
            xn = _rmsnorm(x, nw_ref[...]).astype(BF16)
            g = _dot(xn, wg_bf[...])
            u = _dot(xn, wu_bf[...])
            hmid = (_silu(g) * u).astype(BF16)
            y = x + 0.5 * _dot(hmid, wd_bf[...])
            if final_norm:
                y = _rmsnorm(y, fnw_ref[...])
            o_ref[r0:r0 + sub, :] = y

    i = pl.program_id(0)
    sample_step = pl.num_programs(0) - 1

    @pl.when(i < WEIGHT_LOAD_STEPS)
    def _():
        for w_ref, w_bf in ((wg_ref, wg_bf), (wu_ref, wu_bf), (wd_ref, wd_bf)):
            _load_weight_rows(i, w_ref, w_bf)

    @pl.when((i >= WEIGHT_LOAD_STEPS) & (i < sample_step))
    def _():
        half_step(xp_ref, op_ref)

    @pl.when(i == sample_step)
    def _():
        half_step(xs_ref, os_ref)


def _resident(shape):
    return pl.BlockSpec(shape, lambda *_: (0,) * len(shape), pipeline_mode=pl.Buffered(1))


def _ffn(xp, xs, nw, wg, wu, wd, fnw=None, *, tile_m):
    m, d = xp.shape
    ms = xs.shape[0]
    dff = wg.shape[1]
    assert m % tile_m == 0
    n = m // tile_m
    final_norm = fnw is not None
    ws = WEIGHT_LOAD_STEPS
    prompt_tile = pl.BlockSpec((tile_m, d), lambda i: (jnp.clip(i - ws, 0, n - 1), 0))
    sample_tile = pl.BlockSpec((ms, d), lambda i: (0, 0))
    in_specs = [prompt_tile, sample_tile, _resident((1, d)),
                _weight_rows_spec(wg, ws), _weight_rows_spec(wu, ws), _weight_rows_spec(wd, ws)]
    args = [xp, xs, nw, wg.stacked, wu.stacked, wd.stacked]
    if final_norm:
        in_specs.append(_resident((1, d)))
        args.append(fnw)
    return pl.pallas_call(
        functools.partial(_ffn_kernel, final_norm=final_norm),
        out_shape=(jax.ShapeDtypeStruct((m, d), F32), jax.ShapeDtypeStruct((ms, d), F32)),
        grid=(ws + n + 1,),
        in_specs=in_specs,
        out_specs=(prompt_tile, sample_tile),
        scratch_shapes=[_weight_scratch(wg, ws), _weight_scratch(wu, ws), _weight_scratch(wd, ws)],
        compiler_params=pltpu.CompilerParams(
            dimension_semantics=("arbitrary",), vmem_limit_bytes=VMEM_LIMIT_BYTES),
        name="ffn_final" if final_norm else "ffn",
    )(*args)


def _in_proj(h, wint_ref, col0, width):
    return _dot_nt(h, wint_ref[col0:col0 + width, :])


def _dt_proj(h, wint_ref):
    raw = _in_proj(h, wint_ref, OFF_DT, LANES)
    lane = lax.broadcasted_iota(jnp.int32, raw.shape, 1)
    return jnp.where(lane < N_HEADS, raw, 0.0)


def _mixer_kernel(tiles_per_seq, x_ref, nw_ref, wint_ref, cw_ref, cb_ref, dtb_ref, alog_ref,
                  dskip_ref, snw_ref, scw_ref, wout_ref,
                  o_ref, sfin_ref, cfin_ref, scfin_ref, wint_export_ref, wout_export_ref,
                  st_scr, hist_scr, uhist_scr, wint_bf, wout_bf):
    i = pl.program_id(0)

    @pl.when(i < MIXER_WEIGHT_LOAD_STEPS)
    def _():
        _load_weight_rows(i, wint_ref, wint_bf, wint_export_ref)
        _load_weight_rows(i, wout_ref, wout_bf, wout_export_ref)

    @pl.when(i >= MIXER_WEIGHT_LOAD_STEPS)
    def _():
        c = lax.rem(i - MIXER_WEIGHT_LOAD_STEPS, tiles_per_seq)
        _mixer_chunk(c == 0, c == tiles_per_seq - 1, x_ref, nw_ref, wint_bf, cw_ref,
                     cb_ref, dtb_ref, alog_ref, dskip_ref, snw_ref, scw_ref, wout_bf,
                     o_ref, sfin_ref, cfin_ref, scfin_ref, st_scr, hist_scr, uhist_scr)


def _mixer_chunk(first, last, x_ref, nw_ref, wint_ref, cw_ref, cb_ref, dtb_ref, alog_ref,
                 dskip_ref, snw_ref, scw_ref, wout_ref,
                 o_ref, sfin_ref, cfin_ref, scfin_ref,
                 st_scr, hist_scr, uhist_scr):
    L = CHUNK
    T = x_ref.shape[1]

    @pl.when(first)
    def _():
        st_scr[...] = jnp.zeros_like(st_scr)
        hist_scr[...] = jnp.zeros_like(hist_scr)
        uhist_scr[...] = jnp.zeros_like(uhist_scr)

    x = x_ref[0]
    h = _rmsnorm(x, nw_ref[...]).astype(BF16)
    dt_raw = _dt_proj(h, wint_ref)

    xbc_c_blocks = []
    for b0 in range(0, SSD_CONV_DIM, PROJ_BLOCK):
        cols = slice(b0, b0 + PROJ_BLOCK)
        xbc = _in_proj(h, wint_ref, OFF_XBC + b0, PROJ_BLOCK)
        acc = _causal_conv(xbc, hist_scr[:, cols], cw_ref[:, cols])
        hist_scr[:, cols] = xbc[T - SUBLANES:T, :]
        xbc_c_blocks.append(_silu_mixer(acc + cb_ref[:, cols]))
    cfin_ref[0] = hist_scr[SUBLANES - (SSD_CONV_W - 1):SUBLANES, :]
    xbc_c_all = jnp.concatenate(xbc_c_blocks, axis=1)

    def plus_zero_of(v, gate):
        bits = pltpu.bitcast(gate[0:SUBLANES, 0:LANES], jnp.uint32)
        half_word = jnp.uint32(16)
        zero_row = pltpu.bitcast(
            lax.shift_right_logical(lax.shift_right_logical(bits, half_word), half_word), F32)[0:1, :]
        return v + jnp.concatenate([zero_row] * (D_MODEL // LANES), axis=1).astype(BF16)

    dt_all = _softplus(dt_raw + dtb_ref[...])
    a_all = dt_all * (-jnp.exp(alog_ref[...]))
    row = lax.broadcasted_iota(jnp.int32, (L, L), 0)
    col = lax.broadcasted_iota(jnp.int32, (L, L), 1)
    causal = row >= col
    tril = jnp.where(causal, 1.0, 0.0).astype(BF16)
    lane = lax.broadcasted_iota(jnp.int32, (L, LANES), 1)
    first_half = lane < HEAD_DIM

    y_ssd_chunks = []
    out_sc_chunks = []
    for k in range(T // L):
        rows = slice(k * L, (k + 1) * L)
        xbc_c = xbc_c_all[rows]
        xs = xbc_c[:, 0:SSD_WIDTH]
        b_all = xbc_c[:, SSD_WIDTH:SSD_WIDTH + N_GROUPS * D_STATE]
        c_all = xbc_c[:, SSD_WIDTH + N_GROUPS * D_STATE:SSD_CONV_DIM]
        dt = dt_all[rows]
        a_hi, a_mid, a_lo = _split3(a_all[rows])
        acum = _dot(tril, a_hi) + _dot(tril, a_mid) + _dot(tril, a_lo)
        acum_t = acum.T

        def h_after(gate):
            return plus_zero_of(h[rows], gate)

        def z_gate_after(gate):
            h_late = h_after(gate)
            return jnp.concatenate(
                [_silu_mixer(_in_proj(h_late, wint_ref, OFF_Z + b0, PROJ_BLOCK))
                 for b0 in range(0, SSD_WIDTH, PROJ_BLOCK)], axis=1)

        def short_conv_block_after(gate, b0):
            h_late = h_after(gate)
            cols = slice(b0, b0 + PROJ_BLOCK)
            scc = _in_proj(h_late, wint_ref, OFF_SCC + b0, PROJ_BLOCK)
            sch = _in_proj(h_late, wint_ref, OFF_SCH + b0, PROJ_BLOCK)
            u = scc * sch
            v = _causal_conv(u, uhist_scr[:, cols], scw_ref[:, cols])
            uhist_scr[:, cols] = u[L - SUBLANES:L, :]
            scb = _in_proj(h_late, wint_ref, OFF_SCB + b0, PROJ_BLOCK)
            return (scb * v).astype(BF16)

        z_gate = z_gate_after(acum)
        y_sc = jnp.concatenate(
            [short_conv_block_after(acum, b0) for b0 in range(0, SC_WIDTH, PROJ_BLOCK)], axis=1)
        out_sc_chunks.append(_dot(y_sc, wout_ref[SSD_WIDTH:SSD_WIDTH + SC_WIDTH, :]))

        acum_e = _expand_heads(acum, L)
        dt_e = _expand_heads(dt, L)
        exp_acum_e = jnp.exp(acum_e)
        decay_end_e = jnp.exp(acum_e[L - 1:L, :] - acum_e)
        xdt = xs * dt_e
        xdt_b = xdt.astype(BF16)
        xdecay_b = (xdt * decay_end_e).astype(BF16)

        y_diag_blocks = []
        y_off_blocks = []
        for g in range(N_GROUPS):
            b_g = b_all[:, g * D_STATE:(g + 1) * D_STATE]
            c_g = c_all[:, g * D_STATE:(g + 1) * D_STATE]
            b_gb = b_g.astype(BF16)
            c_gb = c_g.astype(BF16)
            cb = _dot_nt(c_gb, b_gb)
            for q in range(g * HEADS_PER_GROUP // 2, (g + 1) * HEADS_PER_GROUP // 2):
                ms = []
                for hh in (2 * q, 2 * q + 1):
                    seg = acum[:, hh:hh + 1] - acum_t[hh:hh + 1, :]
                    decay = jnp.exp(jnp.where(causal, seg, -jnp.inf))
                    ms.append((cb * decay).astype(BF16))
                m_cat = jnp.concatenate(ms, axis=1)
                x2 = xdt_b[:, q * LANES:(q + 1) * LANES]
                zero = jnp.zeros_like(x2)
                rhs = jnp.concatenate([jnp.where(first_half, x2, zero),
                                       jnp.where(first_half, zero, x2)], axis=0)
                y_diag_blocks.append(_dot(m_cat, rhs))
            gs = slice(g * GROUP_WIDTH, (g + 1) * GROUP_WIDTH)
            s_enter = st_scr[:, gs]
            y_off_blocks.append(_dot(c_gb, s_enter.astype(BF16)))
            new_states = _dot(b_g.T.astype(BF16), xdecay_b[:, gs])
            st_scr[:, gs] = s_enter * exp_acum_e[L - 1:L, gs] + new_states
        y_diag = jnp.concatenate(y_diag_blocks, axis=1)
        y_off = jnp.concatenate(y_off_blocks, axis=1) * exp_acum_e

        y = (y_diag + y_off + dskip_ref[...] * xs) * z_gate
        y_ssd_chunks.append(_group_rmsnorm(y, snw_ref[...]).astype(BF16))

    scfin_ref[0] = uhist_scr[SUBLANES - (SC_CONV_W - 1):SUBLANES, :]
    out_ssd = _dot(jnp.concatenate(y_ssd_chunks, axis=0), wout_ref[0:SSD_WIDTH, :])
    o_ref[0] = x + jnp.concatenate(out_sc_chunks, axis=0) + out_ssd

    @pl.when(last)
    def _():
        sfin_ref[0] = st_scr[...].T


MIXER_CHUNKS_PER_STEP = 2
MIXER_WEIGHT_LOAD_STEPS = WEIGHT_LOAD_STEPS


def _mixer_prompt(x, nw, wint, cw, cb, dtb, alog, dskip, snw, scw, wout):
    nb, seq, d = x.shape
    tile_t = MIXER_CHUNKS_PER_STEP * CHUNK
    assert seq % tile_t == 0
    nc = seq // tile_t
    ws = MIXER_WEIGHT_LOAD_STEPS

    def chunk_map(i):
        t = jnp.maximum(i - ws, 0)
        return (t // nc, t % nc, 0)

    def seq_map(i):
        return (jnp.maximum(i - ws, 0) // nc, 0, 0)

    out_shape = (
        jax.ShapeDtypeStruct((nb, seq, d), F32),
        jax.ShapeDtypeStruct((nb, SSD_WIDTH, D_STATE), F32),
        jax.ShapeDtypeStruct((nb, SSD_CONV_W - 1, SSD_CONV_DIM), F32),
        jax.ShapeDtypeStruct((nb, SC_CONV_W - 1, SC_WIDTH), F32),
        jax.ShapeDtypeStruct(_weight_scratch(wint, ws).shape, BF16),
        jax.ShapeDtypeStruct(_weight_scratch(wout, ws).shape, BF16),
    )

    def export_spec(w):
        return pl.BlockSpec((_weight_block_rows(w.shape[0], ws), w.shape[1]),
                            lambda i: (jnp.minimum(i, ws - 1), 0))

    return pl.pallas_call(
        functools.partial(_mixer_kernel, nc),
        out_shape=out_shape,
        grid=(ws + nb * nc,),
        in_specs=[pl.BlockSpec((1, tile_t, d), chunk_map),
                  _resident((1, d)), _weight_rows_spec(wint, ws),
                  _resident((SSD_CONV_W, SSD_CONV_DIM)), _resident((1, SSD_CONV_DIM)),
                  _resident((1, LANES)), _resident((1, LANES)), _resident((1, SSD_WIDTH)),
                  _resident((1, SSD_WIDTH)), _resident((SC_CONV_W, SC_WIDTH)),
                  _weight_rows_spec(wout, ws)],
        out_specs=(pl.BlockSpec((1, tile_t, d), chunk_map),
                   pl.BlockSpec((1, SSD_WIDTH, D_STATE), seq_map),
                   pl.BlockSpec((1, SSD_CONV_W - 1, SSD_CONV_DIM), seq_map),
                   pl.BlockSpec((1, SC_CONV_W - 1, SC_WIDTH), seq_map),
                   export_spec(wint), export_spec(wout)),
        scratch_shapes=[pltpu.VMEM((D_STATE, SSD_WIDTH), F32),
                        pltpu.VMEM((SUBLANES, SSD_CONV_DIM), F32),
                        pltpu.VMEM((SUBLANES, SC_WIDTH), F32),
                        _weight_scratch(wint, ws), _weight_scratch(wout, ws)],
        compiler_params=pltpu.CompilerParams(
            dimension_semantics=("arbitrary",), vmem_limit_bytes=VMEM_LIMIT_BYTES),
        name="mixer_prompt",
    )(x, nw, wint.stacked, cw, cb, dtb, alog, dskip, snw, scw, wout.stacked)


def _sample_pre_kernel(x_ref, nw_ref, wint_ref, cw_ref, cb_ref, dtb_ref, alog_ref,
                       scw_ref, cst_ref, scst_ref,
                       z_ref, xs_ref, b_ref, c_ref, xdt_blk_ref, da_ref, ysc_ref, cnew_ref, scnew_ref):
    nb = x_ref.shape[0]
    x = x_ref[...]
    h = _rmsnorm(x, nw_ref[...]).astype(BF16)
    dt_raw = _dt_proj(h, wint_ref)
    xbc = _in_proj(h, wint_ref, OFF_XBC, SSD_CONV_DIM)
    z_ref[...] = _in_proj(h, wint_ref, OFF_Z, SSD_WIDTH)
    scb = _in_proj(h, wint_ref, OFF_SCB, SC_WIDTH)
    scc = _in_proj(h, wint_ref, OFF_SCC, SC_WIDTH)
    sch = _in_proj(h, wint_ref, OFF_SCH, SC_WIDTH)

    cw = cw_ref[...]
    acc = xbc * cw[SSD_CONV_W - 1:SSD_CONV_W, :]
    for k in range(SSD_CONV_W - 1):
        acc = acc + cst_ref[:, k, :] * cw[k:k + 1, :]
    for k in range(SSD_CONV_W - 2):
        cnew_ref[:, k, :] = cst_ref[:, k + 1, :]
    cnew_ref[:, SSD_CONV_W - 2, :] = xbc
    xbc_c = _silu(acc + cb_ref[...])
    xs = xbc_c[:, 0:SSD_WIDTH]
    xs_ref[...] = xs
    b_ref[...] = xbc_c[:, SSD_WIDTH:SSD_WIDTH + N_GROUPS * D_STATE]
    c_ref[...] = xbc_c[:, SSD_WIDTH + N_GROUPS * D_STATE:SSD_CONV_DIM]

    dt = _softplus(dt_raw + dtb_ref[...])
    da_ref[...] = jnp.exp(dt * (-jnp.exp(alog_ref[...])))
    xdt_t = (xs * _expand_heads(dt, nb)).T
    bb = xdt_blk_ref.shape[2]
    for j in range(nb // bb):
        xdt_blk_ref[j] = xdt_t[:, j * bb:(j + 1) * bb]

    u = scc * sch
    scw = scw_ref[...]
    v = u * scw[SC_CONV_W - 1:SC_CONV_W, :]
    for k in range(SC_CONV_W - 1):
        v = v + scst_ref[:, k, :] * scw[k:k + 1, :]
    for k in range(SC_CONV_W - 2):
        scnew_ref[:, k, :] = scst_ref[:, k + 1, :]
    scnew_ref[:, SC_CONV_W - 2, :] = u
    ysc_ref[...] = scb * v


def _sample_pre(x, nw, wint, cw, cb, dtb, alog, scw, cst, scst):
    nb, d = x.shape
    f = lambda *s: jax.ShapeDtypeStruct(s, F32)
    bb = STATE_BATCH_BLOCK
    assert nb % bb == 0
    out_shape = (f(nb, SSD_WIDTH), f(nb, SSD_WIDTH), f(nb, N_GROUPS * D_STATE), f(nb, N_GROUPS * D_STATE),
                 f(nb // bb, SSD_WIDTH, bb), f(nb, LANES), f(nb, SC_WIDTH),
                 f(nb, SSD_CONV_W - 1, SSD_CONV_DIM), f(nb, SC_CONV_W - 1, SC_WIDTH))
    return pl.pallas_call(
        _sample_pre_kernel,
        out_shape=out_shape,
        compiler_params=pltpu.CompilerParams(vmem_limit_bytes=VMEM_LIMIT_BYTES),
        name="sample_pre",
    )(x, nw, wint, cw, cb, dtb, alog, scw, cst, scst)


STATE_BATCH_BLOCK = 16


def _split2(x):
    hi = x.astype(BF16)
    return hi, (x - hi.astype(F32)).astype(BF16)


def _sample_state_kernel(da_ref, s0_ref, xdt_ref, b_ref, c_ref, snew_ref, y_ref):
    bb = STATE_BATCH_BLOCK
    blk = pl.program_id(0)
    xdt_t = xdt_ref[0]
    b_rows = b_ref[...]
    c_hi, c_lo = _split2(c_ref[...])
    row = lax.broadcasted_iota(jnp.int32, (bb, SSD_WIDTH), 0)
    y_blk = jnp.zeros((bb, SSD_WIDTH), F32)
    for i in range(bb):
        xdt_col = xdt_t[:, i:i + 1]
        y_parts = []
        for g in range(N_GROUPS):
            ns = slice(g * D_STATE, (g + 1) * D_STATE)
            b_row = b_rows[i:i + 1, ns]
            heads = []
            for hh in range(g * HEADS_PER_GROUP, (g + 1) * HEADS_PER_GROUP):
                rs = slice(hh * HEAD_DIM, (hh + 1) * HEAD_DIM)
                decay = da_ref[blk * bb + i, hh]
                heads.append(s0_ref[i, rs, :] * decay + xdt_col[rs] * b_row)
            s_new = jnp.concatenate(heads, axis=0)
            snew_ref[i, g * GROUP_WIDTH:(g + 1) * GROUP_WIDTH, :] = s_new
            s_hi, s_lo = _split2(s_new)
            lhs = jnp.concatenate([c_hi[:, ns], c_lo[:, ns]], axis=0)
            r_hi = _dot_nt(lhs, s_hi)
            r_lo = _dot_nt(c_hi[:, ns], s_lo)
            y_parts.append(r_hi[i:i + 1] + r_hi[bb + i:bb + i + 1] + r_lo[i:i + 1])
        y_row = jnp.concatenate(y_parts, axis=1)
        y_blk = jnp.where(row == i, y_row, y_blk)
    y_ref[...] = y_blk


def _sample_state(s0, xdt_blocks, decay, b_rows, c_rows):
    nb = s0.shape[0]
    nblk, _, bb = xdt_blocks.shape
    assert bb == STATE_BATCH_BLOCK and nblk * bb == nb
    return pl.pallas_call(
        _sample_state_kernel,
        out_shape=(jax.ShapeDtypeStruct((nb, SSD_WIDTH, D_STATE), F32),
                   jax.ShapeDtypeStruct((nb, SSD_WIDTH), F32)),
        grid=(nblk,),
        in_specs=[pl.BlockSpec(memory_space=pltpu.SMEM),
                  pl.BlockSpec((bb, SSD_WIDTH, D_STATE), lambda i: (i, 0, 0)),
                  pl.BlockSpec((1, SSD_WIDTH, bb), lambda i: (i, 0, 0)),
                  pl.BlockSpec((bb, N_GROUPS * D_STATE), lambda i: (i, 0)),
                  pl.BlockSpec((bb, N_GROUPS * D_STATE), lambda i: (i, 0))],
        out_specs=(pl.BlockSpec((bb, SSD_WIDTH, D_STATE), lambda i: (i, 0, 0)),
                   pl.BlockSpec((bb, SSD_WIDTH), lambda i: (i, 0))),
        compiler_params=pltpu.CompilerParams(
            dimension_semantics=("arbitrary",), vmem_limit_bytes=VMEM_LIMIT_BYTES),
        name="sample_state",
    )(decay, s0, xdt_blocks, b_rows, c_rows)


def _sample_post_kernel(x_ref, yraw_ref, xs_ref, z_ref, ysc_ref, dskip_ref, snw_ref, wout_ref, o_ref):
    y = yraw_ref[...] + dskip_ref[...] * xs_ref[...]
    y = y * _silu(z_ref[...])
    y_ssd = _group_rmsnorm(y, snw_ref[...])
    mixed = jnp.concatenate([y_ssd, ysc_ref[...]], axis=1).astype(BF16)
    o_ref[...] = x_ref[...] + _dot(mixed, wout_ref[...])


def _sample_post(x, yraw, xs, z, ysc, dskip, snw, wout):
    return pl.pallas_call(
        _sample_post_kernel,
        out_shape=jax.ShapeDtypeStruct(x.shape, F32),
        compiler_params=pltpu.CompilerParams(vmem_limit_bytes=VMEM_LIMIT_BYTES),
        name="sample_post",
    )(x, yraw, xs, z, ysc, dskip, snw, wout)


FFN_TILE_M = 1024
FFN_SUB_TILE_M = 512


def _layer_params(i, norm_ffn1_w, ffn1_w_gate, ffn1_w_up, ffn1_w_down, norm_mix_w, w_in_t,
                  ssd_conv_w, ssd_conv_b, dt_bias, a_log, d_skip, ssd_norm_w, sconv_w, w_out,
                  norm_ffn2_w, ffn2_w_gate, ffn2_w_up, ffn2_w_down):
    pad_heads = lambda v: jnp.pad(v, (0, LANES - N_HEADS)).reshape(1, LANES)
    row = lambda v: v.reshape(1, -1)
    return dict(
        ffn1=(row(norm_ffn1_w[i]), _LayerWeight(ffn1_w_gate, i), _LayerWeight(ffn1_w_up, i),
              _LayerWeight(ffn1_w_down, i)),
        ffn2=(row(norm_ffn2_w[i]), _LayerWeight(ffn2_w_gate, i), _LayerWeight(ffn2_w_up, i),
              _LayerWeight(ffn2_w_down, i)),
        nw=row(norm_mix_w[i]), wint=_LayerWeight(w_in_t, i), cw=ssd_conv_w[i], cb=row(ssd_conv_b[i]),
        dtb=pad_heads(dt_bias[i]), alog=pad_heads(a_log[i]),
        dskip=row(jnp.repeat(d_skip[i], HEAD_DIM)), snw=row(ssd_norm_w[i]), scw=sconv_w[i],
        wout=_LayerWeight(w_out, i))


def kernel(x_prompt, x_sample, state_ssm, state_ssd_conv, state_sconv, norm_ffn1_w, ffn1_w_gate, ffn1_w_up, ffn1_w_down, norm_mix_w, w_in, ssd_conv_w, ssd_conv_b, dt_bias, a_log, d_skip, ssd_norm_w, sconv_w, w_out, norm_ffn2_w, ffn2_w_gate, ffn2_w_up, ffn2_w_down, final_norm_w):
    depth = w_in.shape[0]
    bp, seq, d = x_prompt.shape
    bs, dec_seq, _ = x_sample.shape
    assert dec_seq == 1, "sample group is one token per sequence"
    fnw = final_norm_w.reshape(1, d)
    w_in_t = jnp.swapaxes(w_in, 1, 2)

    xp = x_prompt.reshape(bp * seq, d)
    xs = x_sample.reshape(bs, d)
    outs = [[] for _ in range(6)]
    for i in range(depth):
        p = _layer_params(i, norm_ffn1_w, ffn1_w_gate, ffn1_w_up, ffn1_w_down, norm_mix_w, w_in_t,
                          ssd_conv_w, ssd_conv_b, dt_bias, a_log, d_skip, ssd_norm_w, sconv_w,
                          w_out, norm_ffn2_w, ffn2_w_gate, ffn2_w_up, ffn2_w_down)
        last = i == depth - 1
        xp, xs = _ffn(xp, xs, *p["ffn1"], tile_m=FFN_TILE_M)
        xp3, s_p, c_p, sc_p, wint_bf, wout_bf = _mixer_prompt(
            xp.reshape(bp, seq, d), p["nw"], p["wint"], p["cw"], p["cb"], p["dtb"], p["alog"],
            p["dskip"], p["snw"], p["scw"], p["wout"])
        z, xs_conv, b_rows, c_rows, xdt_t, decay, ysc, c_s, sc_s = _sample_pre(
            xs, p["nw"], wint_bf, p["cw"], p["cb"], p["dtb"], p["alog"], p["scw"],
            state_ssd_conv[i], state_sconv[i])
        s_s, yraw = _sample_state(state_ssm[i].reshape(bs, SSD_WIDTH, D_STATE), xdt_t, decay, b_rows, c_rows)
        xs = _sample_post(xs, yraw, xs_conv, z, ysc, p["dskip"], p["snw"], wout_bf)
        xp, xs = _ffn(xp3.reshape(bp * seq, d), xs, *p["ffn2"], fnw if last else None, tile_m=FFN_TILE_M)
        for lst, v in zip(outs, (s_p.reshape(bp, N_HEADS, HEAD_DIM, D_STATE), c_p, sc_p,
                                 s_s.reshape(bs, N_HEADS, HEAD_DIM, D_STATE),
                                 c_s, sc_s)):
            lst.append(v)
    return (xp.reshape(bp, seq, d), xs.reshape(bs, dec_seq, d)) + tuple(jnp.stack(l) for l in outs)
```

```python
import functools

import jax
import jax.numpy as jnp
from jax import lax
from jax.experimental import pallas as pl
from jax.experimental.pallas import tpu as pltpu

F32 = jnp.float32
BF16 = jnp.bfloat16

D_MODEL = 1024
SSD_WIDTH = 1024
SC_WIDTH = 1024
HEAD_DIM = 64
N_HEADS = SSD_WIDTH // HEAD_DIM
N_GROUPS = 2
HEADS_PER_GROUP = N_HEADS // N_GROUPS
GROUP_WIDTH = SSD_WIDTH // N_GROUPS
D_STATE = 128
SSD_CONV_W = 4
SSD_CONV_DIM = SSD_WIDTH + 2 * N_GROUPS * D_STATE
SC_CONV_W = 3
CHUNK = 256
NORM_EPS = 1e-6

LANES = 128
SUBLANES = 8

OFF_Z = 0
OFF_XBC = OFF_Z + SSD_WIDTH
OFF_DT = OFF_XBC + SSD_CONV_DIM
OFF_SCB = OFF_DT + N_HEADS
OFF_SCC = OFF_SCB + SC_WIDTH
OFF_SCH = OFF_SCC + SC_WIDTH
D_IN_PROJ = OFF_SCH + SC_WIDTH
PROJ_BLOCK = 512

VMEM_LIMIT_BYTES = 56 * 1024 * 1024


def _rmsnorm(x, w):
    ms = jnp.mean(x * x, axis=-1, keepdims=True)
    return x * lax.rsqrt(ms + NORM_EPS) * w


def _silu(x):
    half = 0.5 * x
    return half * jnp.tanh(half) + half


def _silu_mixer(x):
    return x * (0.5 * jnp.tanh(0.5 * x) + 0.5)


def _softplus(x):
    return jnp.maximum(x, 0.0) + jnp.log1p(jnp.exp(-jnp.abs(x)))


def _dot(a, b):
    return jnp.dot(a, b, preferred_element_type=F32)


def _dot_nt(a, b):
    return lax.dot_general(a, b, (((1,), (1,)), ((), ())), preferred_element_type=F32)


def _split3(x):
    hi = x.astype(BF16)
    r1 = x - hi.astype(F32)
    mid = r1.astype(BF16)
    lo = (r1 - mid.astype(F32)).astype(BF16)
    return hi, mid, lo


def _expand_heads(v, rows):
    lane = lax.broadcasted_iota(jnp.int32, (rows, LANES), 1)
    first_half = lane < HEAD_DIM
    blocks = []
    for q in range(N_HEADS // 2):
        c0 = jnp.broadcast_to(v[:, 2 * q:2 * q + 1], (rows, LANES))
        c1 = jnp.broadcast_to(v[:, 2 * q + 1:2 * q + 2], (rows, LANES))
        blocks.append(jnp.where(first_half, c0, c1))
    return jnp.concatenate(blocks, axis=1)


def _group_rmsnorm(y, w):
    outs = []
    for g in range(N_GROUPS):
        sl = slice(g * GROUP_WIDTH, (g + 1) * GROUP_WIDTH)
        outs.append(_rmsnorm(y[:, sl], w[:, sl]))
    return jnp.concatenate(outs, axis=1)


def _shift_rows(x, prev_tile, j):
    rolled = pltpu.roll(x, j, 0)
    row = lax.broadcasted_iota(jnp.int32, prev_tile.shape, 0)
    head = jnp.where(row < j, pltpu.roll(prev_tile, j, 0), rolled[0:SUBLANES])
    return jnp.concatenate([head, rolled[SUBLANES:]], axis=0)


def _causal_conv(u, prev_tile, w):
    k = w.shape[0]
    acc = u * w[k - 1:k, :]
    for j in range(1, k):
        acc = acc + _shift_rows(u, prev_tile, j) * w[k - 1 - j:k - j, :]
    return acc


WEIGHT_LOAD_STEPS = 8


def _load_weight_rows(step, w_ref, w_bf_ref):
    rows = w_ref.shape[0]
    r0 = pl.multiple_of(step * rows, rows)
    w_bf_ref[pl.ds(r0, rows), :] = w_ref[...].astype(BF16)


def _weight_block_rows(rows, steps):
    tile = 2 * SUBLANES
    return -(-rows // (steps * tile)) * tile


def _weight_rows_spec(w, steps):
    _, rows, cols = w.stacked.shape
    return pl.BlockSpec((None, _weight_block_rows(rows, steps), cols),
                        lambda i: (w.layer, jnp.minimum(i, steps - 1), 0))


def _weight_scratch(w, steps):
    rows, cols = w.shape
    return pltpu.VMEM((steps * _weight_block_rows(rows, steps), cols), BF16)


class _LayerWeight:
    def __init__(self, stacked, layer):
        self.stacked, self.layer = stacked, layer
        self.shape = stacked.shape[1:]


def _ffn_kernel(*refs, final_norm):
    if final_norm:
        (xp_ref, xs_ref, nw_ref, wg_ref, wu_ref, wd_ref, fnw_ref, op_ref, os_ref,
         wg_bf, wu_bf, wd_bf) = refs
    else:
        xp_ref, xs_ref, nw_ref, wg_ref, wu_ref, wd_ref, op_ref, os_ref, wg_bf, wu_bf, wd_bf = refs

    def half_step(x_ref, o_ref):
        rows = x_ref.shape[0]
        sub = min(rows, FFN_SUB_TILE_M)
        for r0 in range(0, rows, sub):
            x = x_ref[r0:r0 + sub, :]
            xn = _rmsnorm(x, nw_ref[...]).astype(BF16)
            g = _dot(xn, wg_bf[...])
            u = _dot(xn, wu_bf[...])
            hmid = (_silu(g) * u).astype(BF16)
            y = x + 0.5 * _dot(hmid, wd_bf[...])
            if final_norm:
                y = _rmsnorm(y, fnw_ref[...])
            o_ref[r0:r0 + sub, :] = y

    i = pl.program_id(0)
    sample_step = pl.num_programs(0) - 1

    @pl.when(i < WEIGHT_LOAD_STEPS)
    def _():
        for w_ref, w_bf in ((wg_ref, wg_bf), (wu_ref, wu_bf), (wd_ref, wd_bf)):
            _load_weight_rows(i, w_ref, w_bf)

    @pl.when((i >= WEIGHT_LOAD_STEPS) & (i < sample_step))
    def _():
        half_step(xp_ref, op_ref)

    @pl.when(i == sample_step)
    def _():
        half_step(xs_ref, os_ref)


def _resident(shape):
    return pl.BlockSpec(shape, lambda *_: (0,) * len(shape), pipeline_mode=pl.Buffered(1))


def _ffn(xp, xs, nw, wg, wu, wd, fnw=None, *, tile_m):
    m, d = xp.shape
    ms = xs.shape[0]
    dff = wg.shape[1]
    assert m % tile_m == 0
    n = m // tile_m
    final_norm = fnw is not None
    ws = WEIGHT_LOAD_STEPS
    prompt_tile = pl.BlockSpec((tile_m, d), lambda i: (jnp.clip(i - ws, 0, n - 1), 0))
    sample_tile = pl.BlockSpec((ms, d), lambda i: (0, 0))
    in_specs = [prompt_tile, sample_tile, _resident((1, d)),
                _weight_rows_spec(wg, ws), _weight_rows_spec(wu, ws), _weight_rows_spec(wd, ws)]
    args = [xp, xs, nw, wg.stacked, wu.stacked, wd.stacked]
    if final_norm:
        in_specs.append(_resident((1, d)))
        args.append(fnw)
    return pl.pallas_call(
        functools.partial(_ffn_kernel, final_norm=final_norm),
        out_shape=(jax.ShapeDtypeStruct((m, d), F32), jax.ShapeDtypeStruct((ms, d), F32)),
        grid=(ws + n + 1,),
        in_specs=in_specs,
        out_specs=(prompt_tile, sample_tile),
        scratch_shapes=[_weight_scratch(wg, ws), _weight_scratch(wu, ws), _weight_scratch(wd, ws)],
        compiler_params=pltpu.CompilerParams(
            dimension_semantics=("arbitrary",), vmem_limit_bytes=VMEM_LIMIT_BYTES),
        name="ffn_final" if final_norm else "ffn",
    )(*args)


def _in_proj(h, wint_ref, col0, width):
    return _dot_nt(h, wint_ref[col0:col0 + width, :])


def _dt_proj(h, wint_ref):
    raw = _in_proj(h, wint_ref, OFF_DT, LANES)
    lane = lax.broadcasted_iota(jnp.int32, raw.shape, 1)
    return jnp.where(lane < N_HEADS, raw, 0.0)


def _mixer_kernel(tiles_per_seq, x_ref, nw_ref, wint_ref, cw_ref, cb_ref, dtb_ref, alog_ref,
                  dskip_ref, snw_ref, scw_ref, wout_ref,
                  o_ref, sfin_ref, cfin_ref, scfin_ref,
                  st_scr, hist_scr, uhist_scr, wint_bf, wout_bf):
    i = pl.program_id(0)

    @pl.when(i < MIXER_WEIGHT_LOAD_STEPS)
    def _():
        _load_weight_rows(i, wint_ref, wint_bf)
        _load_weight_rows(i, wout_ref, wout_bf)

    @pl.when(i >= MIXER_WEIGHT_LOAD_STEPS)
    def _():
        c = lax.rem(i - MIXER_WEIGHT_LOAD_STEPS, tiles_per_seq)
        _mixer_chunk(c == 0, c == tiles_per_seq - 1, x_ref, nw_ref, wint_bf, cw_ref,
                     cb_ref, dtb_ref, alog_ref, dskip_ref, snw_ref, scw_ref, wout_bf,
                     o_ref, sfin_ref, cfin_ref, scfin_ref, st_scr, hist_scr, uhist_scr)


def _mixer_chunk(first, last, x_ref, nw_ref, wint_ref, cw_ref, cb_ref, dtb_ref, alog_ref,
                 dskip_ref, snw_ref, scw_ref, wout_ref,
                 o_ref, sfin_ref, cfin_ref, scfin_ref,
                 st_scr, hist_scr, uhist_scr):
    L = CHUNK
    T = x_ref.shape[1]

    @pl.when(first)
    def _():
        st_scr[...] = jnp.zeros_like(st_scr)
        hist_scr[...] = jnp.zeros_like(hist_scr)
        uhist_scr[...] = jnp.zeros_like(uhist_scr)

    x = x_ref[0]
    h = _rmsnorm(x, nw_ref[...]).astype(BF16)
    dt_raw = _dt_proj(h, wint_ref)

    xbc_c_blocks = []
    for b0 in range(0, SSD_CONV_DIM, PROJ_BLOCK):
        cols = slice(b0, b0 + PROJ_BLOCK)
        xbc = _in_proj(h, wint_ref, OFF_XBC + b0, PROJ_BLOCK)
        acc = _causal_conv(xbc, hist_scr[:, cols], cw_ref[:, cols])
        hist_scr[:, cols] = xbc[T - SUBLANES:T, :]
        xbc_c_blocks.append(_silu_mixer(acc + cb_ref[:, cols]))
    cfin_ref[0] = hist_scr[SUBLANES - (SSD_CONV_W - 1):SUBLANES, :]
    xbc_c_all = jnp.concatenate(xbc_c_blocks, axis=1)

    def plus_zero_of(v, gate):
        bits = pltpu.bitcast(gate[0:SUBLANES, 0:LANES], jnp.uint32)
        half_word = jnp.uint32(16)
        zero_row = pltpu.bitcast(
            lax.shift_right_logical(lax.shift_right_logical(bits, half_word), half_word), F32)[0:1, :]
        return v + jnp.concatenate([zero_row] * (D_MODEL // LANES), axis=1).astype(BF16)

    dt_all = _softplus(dt_raw + dtb_ref[...])
    a_all = dt_all * (-jnp.exp(alog_ref[...]))
    row = lax.broadcasted_iota(jnp.int32, (L, L), 0)
    col = lax.broadcasted_iota(jnp.int32, (L, L), 1)
    causal = row >= col
    tril = jnp.where(causal, 1.0, 0.0).astype(BF16)
    lane = lax.broadcasted_iota(jnp.int32, (L, LANES), 1)
    first_half = lane < HEAD_DIM

    y_ssd_chunks = []
    out_sc_chunks = []
    for k in range(T // L):
        rows = slice(k * L, (k + 1) * L)
        xbc_c = xbc_c_all[rows]
        xs = xbc_c[:, 0:SSD_WIDTH]
        b_all = xbc_c[:, SSD_WIDTH:SSD_WIDTH + N_GROUPS * D_STATE]
        c_all = xbc_c[:, SSD_WIDTH + N_GROUPS * D_STATE:SSD_CONV_DIM]
        dt = dt_all[rows]
        a_hi, a_mid, a_lo = _split3(a_all[rows])
        acum = _dot(tril, a_hi) + _dot(tril, a_mid) + _dot(tril, a_lo)
        acum_t = acum.T

        def h_after(gate):
            return plus_zero_of(h[rows], gate)

        def z_gate_after(gate):
            h_late = h_after(gate)
            return jnp.concatenate(
                [_silu_mixer(_in_proj(h_late, wint_ref, OFF_Z + b0, PROJ_BLOCK))
                 for b0 in range(0, SSD_WIDTH, PROJ_BLOCK)], axis=1)

        def short_conv_block_after(gate, b0):
            h_late = h_after(gate)
            cols = slice(b0, b0 + PROJ_BLOCK)
            scc = _in_proj(h_late, wint_ref, OFF_SCC + b0, PROJ_BLOCK)
            sch = _in_proj(h_late, wint_ref, OFF_SCH + b0, PROJ_BLOCK)
            u = scc * sch
            v = _causal_conv(u, uhist_scr[:, cols], scw_ref[:, cols])
            uhist_scr[:, cols] = u[L - SUBLANES:L, :]
            scb = _in_proj(h_late, wint_ref, OFF_SCB + b0, PROJ_BLOCK)
            return (scb * v).astype(BF16)

        z_gate = z_gate_after(acum)
        y_sc = jnp.concatenate(
            [short_conv_block_after(acum, b0) for b0 in range(0, SC_WIDTH, PROJ_BLOCK)], axis=1)
        out_sc_chunks.append(_dot(y_sc, wout_ref[SSD_WIDTH:SSD_WIDTH + SC_WIDTH, :]))

        acum_e = _expand_heads(acum, L)
        dt_e = _expand_heads(dt, L)
        exp_acum_e = jnp.exp(acum_e)
        decay_end_e = jnp.exp(acum_e[L - 1:L, :] - acum_e)
        xdt = xs * dt_e
        xdt_b = xdt.astype(BF16)
        xdecay_b = (xdt * decay_end_e).astype(BF16)

        y_diag_blocks = []
        y_off_blocks = []
        for g in range(N_GROUPS):
            b_g = b_all[:, g * D_STATE:(g + 1) * D_STATE]
            c_g = c_all[:, g * D_STATE:(g + 1) * D_STATE]
            b_gb = b_g.astype(BF16)
            c_gb = c_g.astype(BF16)
            cb = _dot_nt(c_gb, b_gb)
            for q in range(g * HEADS_PER_GROUP // 2, (g + 1) * HEADS_PER_GROUP // 2):
                ms = []
                for hh in (2 * q, 2 * q + 1):
                    seg = acum[:, hh:hh + 1] - acum_t[hh:hh + 1, :]
                    decay = jnp.exp(jnp.where(causal, seg, -jnp.inf))
                    ms.append((cb * decay).astype(BF16))
                m_cat = jnp.concatenate(ms, axis=1)
                x2 = xdt_b[:, q * LANES:(q + 1) * LANES]
                zero = jnp.zeros_like(x2)
                rhs = jnp.concatenate([jnp.where(first_half, x2, zero),
                                       jnp.where(first_half, zero, x2)], axis=0)
                y_diag_blocks.append(_dot(m_cat, rhs))
            gs = slice(g * GROUP_WIDTH, (g + 1) * GROUP_WIDTH)
            s_enter = st_scr[:, gs]
            y_off_blocks.append(_dot(c_gb, s_enter.astype(BF16)))
            new_states = _dot(b_g.T.astype(BF16), xdecay_b[:, gs])
            st_scr[:, gs] = s_enter * exp_acum_e[L - 1:L, gs] + new_states
        y_diag = jnp.concatenate(y_diag_blocks, axis=1)
        y_off = jnp.concatenate(y_off_blocks, axis=1) * exp_acum_e

        y = (y_diag + y_off + dskip_ref[...] * xs) * z_gate
        y_ssd_chunks.append(_group_rmsnorm(y, snw_ref[...]).astype(BF16))

    scfin_ref[0] = uhist_scr[SUBLANES - (SC_CONV_W - 1):SUBLANES, :]
    out_ssd = _dot(jnp.concatenate(y_ssd_chunks, axis=0), wout_ref[0:SSD_WIDTH, :])
    o_ref[0] = x + jnp.concatenate(out_sc_chunks, axis=0) + out_ssd

    @pl.when(last)
    def _():
        sfin_ref[0] = st_scr[...].T


MIXER_CHUNKS_PER_STEP = 4
MIXER_WEIGHT_LOAD_STEPS = 16


def _mixer_prompt(x, nw, wint, cw, cb, dtb, alog, dskip, snw, scw, wout):
    nb, seq, d = x.shape
    tile_t = MIXER_CHUNKS_PER_STEP * CHUNK
    assert seq % tile_t == 0
    nc = seq // tile_t
    ws = MIXER_WEIGHT_LOAD_STEPS

    def chunk_map(i):
        t = jnp.maximum(i - ws, 0)
        return (t // nc, t % nc, 0)

    def seq_map(i):
        return (jnp.maximum(i - ws, 0) // nc, 0, 0)

    out_shape = (
        jax.ShapeDtypeStruct((nb, seq, d), F32),
        jax.ShapeDtypeStruct((nb, SSD_WIDTH, D_STATE), F32),
        jax.ShapeDtypeStruct((nb, SSD_CONV_W - 1, SSD_CONV_DIM), F32),
        jax.ShapeDtypeStruct((nb, SC_CONV_W - 1, SC_WIDTH), F32),
    )
    return pl.pallas_call(
        functools.partial(_mixer_kernel, nc),
        out_shape=out_shape,
        grid=(ws + nb * nc,),
        in_specs=[pl.BlockSpec((1, tile_t, d), chunk_map),
                  _resident((1, d)), _weight_rows_spec(wint, ws),
                  _resident((SSD_CONV_W, SSD_CONV_DIM)), _resident((1, SSD_CONV_DIM)),
                  _resident((1, LANES)), _resident((1, LANES)), _resident((1, SSD_WIDTH)),
                  _resident((1, SSD_WIDTH)), _resident((SC_CONV_W, SC_WIDTH)),
                  _weight_rows_spec(wout, ws)],
        out_specs=(pl.BlockSpec((1, tile_t, d), chunk_map),
                   pl.BlockSpec((1, SSD_WIDTH, D_STATE), seq_map),
                   pl.BlockSpec((1, SSD_CONV_W - 1, SSD_CONV_DIM), seq_map),
                   pl.BlockSpec((1, SC_CONV_W - 1, SC_WIDTH), seq_map)),
        scratch_shapes=[pltpu.VMEM((D_STATE, SSD_WIDTH), F32),
                        pltpu.VMEM((SUBLANES, SSD_CONV_DIM), F32),
                        pltpu.VMEM((SUBLANES, SC_WIDTH), F32),
                        _weight_scratch(wint, ws), _weight_scratch(wout, ws)],
        compiler_params=pltpu.CompilerParams(
            dimension_semantics=("arbitrary",), vmem_limit_bytes=VMEM_LIMIT_BYTES),
        name="mixer_prompt",
    )(x, nw, wint.stacked, cw, cb, dtb, alog, dskip, snw, scw, wout.stacked)


def _sample_pre_kernel(x_ref, nw_ref, wint_ref, cw_ref, cb_ref, dtb_ref, alog_ref,
                       scw_ref, cst_ref, scst_ref,
                       z_ref, xs_ref, b_ref, c_ref, xdt_blk_ref, da_ref, ysc_ref, cnew_ref, scnew_ref,
                       wint_bf):
    nb = x_ref.shape[0]
    x = x_ref[...]
    h = _rmsnorm(x, nw_ref[...]).astype(BF16)
    wint_bf[...] = wint_ref[...].astype(BF16)
    dt_raw = _dt_proj(h, wint_bf)
    xbc = _in_proj(h, wint_bf, OFF_XBC, SSD_CONV_DIM)
    z_ref[...] = _in_proj(h, wint_bf, OFF_Z, SSD_WIDTH)
    scb = _in_proj(h, wint_bf, OFF_SCB, SC_WIDTH)
    scc = _in_proj(h, wint_bf, OFF_SCC, SC_WIDTH)
    sch = _in_proj(h, wint_bf, OFF_SCH, SC_WIDTH)

    cw = cw_ref[...]
    acc = xbc * cw[SSD_CONV_W - 1:SSD_CONV_W, :]
    for k in range(SSD_CONV_W - 1):
        acc = acc + cst_ref[:, k, :] * cw[k:k + 1, :]
    for k in range(SSD_CONV_W - 2):
        cnew_ref[:, k, :] = cst_ref[:, k + 1, :]
    cnew_ref[:, SSD_CONV_W - 2, :] = xbc
    xbc_c = _silu(acc + cb_ref[...])
    xs = xbc_c[:, 0:SSD_WIDTH]
    xs_ref[...] = xs
    b_ref[...] = xbc_c[:, SSD_WIDTH:SSD_WIDTH + N_GROUPS * D_STATE]
    c_ref[...] = xbc_c[:, SSD_WIDTH + N_GROUPS * D_STATE:SSD_CONV_DIM]

    dt = _softplus(dt_raw + dtb_ref[...])
    da_ref[...] = jnp.exp(dt * (-jnp.exp(alog_ref[...])))
    xdt_t = (xs * _expand_heads(dt, nb)).T
    bb = xdt_blk_ref.shape[2]
    for j in range(nb // bb):
        xdt_blk_ref[j] = xdt_t[:, j * bb:(j + 1) * bb]

    u = scc * sch
    scw = scw_ref[...]
    v = u * scw[SC_CONV_W - 1:SC_CONV_W, :]
    for k in range(SC_CONV_W - 1):
        v = v + scst_ref[:, k, :] * scw[k:k + 1, :]
    for k in range(SC_CONV_W - 2):
        scnew_ref[:, k, :] = scst_ref[:, k + 1, :]
    scnew_ref[:, SC_CONV_W - 2, :] = u
    ysc_ref[...] = scb * v


def _sample_pre(x, nw, wint, cw, cb, dtb, alog, scw, cst, scst):
    nb, d = x.shape
    f = lambda *s: jax.ShapeDtypeStruct(s, F32)
    bb = STATE_BATCH_BLOCK
    assert nb % bb == 0
    out_shape = (f(nb, SSD_WIDTH), f(nb, SSD_WIDTH), f(nb, N_GROUPS * D_STATE), f(nb, N_GROUPS * D_STATE),
                 f(nb // bb, SSD_WIDTH, bb), f(nb, LANES), f(nb, SC_WIDTH),
                 f(nb, SSD_CONV_W - 1, SSD_CONV_DIM), f(nb, SC_CONV_W - 1, SC_WIDTH))
    return pl.pallas_call(
        _sample_pre_kernel,
        out_shape=out_shape,
        scratch_shapes=[pltpu.VMEM(wint.shape, BF16)],
        compiler_params=pltpu.CompilerParams(vmem_limit_bytes=VMEM_LIMIT_BYTES),
        name="sample_pre",
    )(x, nw, wint, cw, cb, dtb, alog, scw, cst, scst)


STATE_BATCH_BLOCK = 16


def _split2(x):
    hi = x.astype(BF16)
    return hi, (x - hi.astype(F32)).astype(BF16)


def _sample_state_kernel(da_ref, s0_ref, xdt_ref, b_ref, c_ref, snew_ref, y_ref):
    bb = STATE_BATCH_BLOCK
    blk = pl.program_id(0)
    xdt_t = xdt_ref[0]
    b_rows = b_ref[...]
    c_hi, c_lo = _split2(c_ref[...])
    row = lax.broadcasted_iota(jnp.int32, (bb, SSD_WIDTH), 0)
    y_blk = jnp.zeros((bb, SSD_WIDTH), F32)
    for i in range(bb):
        xdt_col = xdt_t[:, i:i + 1]
        y_parts = []
        for g in range(N_GROUPS):
            ns = slice(g * D_STATE, (g + 1) * D_STATE)
            b_row = b_rows[i:i + 1, ns]
            heads = []
            for hh in range(g * HEADS_PER_GROUP, (g + 1) * HEADS_PER_GROUP):
                rs = slice(hh * HEAD_DIM, (hh + 1) * HEAD_DIM)
                decay = da_ref[blk * bb + i, hh]
                heads.append(s0_ref[i, rs, :] * decay + xdt_col[rs] * b_row)
            s_new = jnp.concatenate(heads, axis=0)
            snew_ref[i, g * GROUP_WIDTH:(g + 1) * GROUP_WIDTH, :] = s_new
            s_hi, s_lo = _split2(s_new)
            lhs = jnp.concatenate([c_hi[:, ns], c_lo[:, ns]], axis=0)
            r_hi = _dot_nt(lhs, s_hi)
            r_lo = _dot_nt(c_hi[:, ns], s_lo)
            y_parts.append(r_hi[i:i + 1] + r_hi[bb + i:bb + i + 1] + r_lo[i:i + 1])
        y_row = jnp.concatenate(y_parts, axis=1)
        y_blk = jnp.where(row == i, y_row, y_blk)
    y_ref[...] = y_blk


def _sample_state(s0, xdt_blocks, decay, b_rows, c_rows):
    nb = s0.shape[0]
    nblk, _, bb = xdt_blocks.shape
    assert bb == STATE_BATCH_BLOCK and nblk * bb == nb
    return pl.pallas_call(
        _sample_state_kernel,
        out_shape=(jax.ShapeDtypeStruct((nb, SSD_WIDTH, D_STATE), F32),
                   jax.ShapeDtypeStruct((nb, SSD_WIDTH), F32)),
        grid=(nblk,),
        in_specs=[pl.BlockSpec(memory_space=pltpu.SMEM),
                  pl.BlockSpec((bb, SSD_WIDTH, D_STATE), lambda i: (i, 0, 0)),
                  pl.BlockSpec((1, SSD_WIDTH, bb), lambda i: (i, 0, 0)),
                  pl.BlockSpec((bb, N_GROUPS * D_STATE), lambda i: (i, 0)),
                  pl.BlockSpec((bb, N_GROUPS * D_STATE), lambda i: (i, 0))],
        out_specs=(pl.BlockSpec((bb, SSD_WIDTH, D_STATE), lambda i: (i, 0, 0)),
                   pl.BlockSpec((bb, SSD_WIDTH), lambda i: (i, 0))),
        compiler_params=pltpu.CompilerParams(
            dimension_semantics=("arbitrary",), vmem_limit_bytes=VMEM_LIMIT_BYTES),
        name="sample_state",
    )(decay, s0, xdt_blocks, b_rows, c_rows)


def _sample_post_kernel(x_ref, yraw_ref, xs_ref, z_ref, ysc_ref, dskip_ref, snw_ref, wout_ref, o_ref):
    y = yraw_ref[...] + dskip_ref[...] * xs_ref[...]
    y = y * _silu(z_ref[...])
    y_ssd = _group_rmsnorm(y, snw_ref[...])
    mixed = jnp.concatenate([y_ssd, ysc_ref[...]], axis=1).astype(BF16)
    o_ref[...] = x_ref[...] + _dot(mixed, wout_ref[...].astype(BF16))


def _sample_post(x, yraw, xs, z, ysc, dskip, snw, wout):
    return pl.pallas_call(
        _sample_post_kernel,
        out_shape=jax.ShapeDtypeStruct(x.shape, F32),
        compiler_params=pltpu.CompilerParams(vmem_limit_bytes=VMEM_LIMIT_BYTES),
        name="sample_post",
    )(x, yraw, xs, z, ysc, dskip, snw, wout)


FFN_TILE_M = 1024
FFN_SUB_TILE_M = 512


def _layer_params(i, norm_ffn1_w, ffn1_w_gate, ffn1_w_up, ffn1_w_down, norm_mix_w, w_in_t,
                  ssd_conv_w, ssd_conv_b, dt_bias, a_log, d_skip, ssd_norm_w, sconv_w, w_out,
                  norm_ffn2_w, ffn2_w_gate, ffn2_w_up, ffn2_w_down):
    pad_heads = lambda v: jnp.pad(v, (0, LANES - N_HEADS)).reshape(1, LANES)
    row = lambda v: v.reshape(1, -1)
    return dict(
        ffn1=(row(norm_ffn1_w[i]), _LayerWeight(ffn1_w_gate, i), _LayerWeight(ffn1_w_up, i),
              _LayerWeight(ffn1_w_down, i)),
        ffn2=(row(norm_ffn2_w[i]), _LayerWeight(ffn2_w_gate, i), _LayerWeight(ffn2_w_up, i),
              _LayerWeight(ffn2_w_down, i)),
        nw=row(norm_mix_w[i]), wint=_LayerWeight(w_in_t, i), cw=ssd_conv_w[i], cb=row(ssd_conv_b[i]),
        dtb=pad_heads(dt_bias[i]), alog=pad_heads(a_log[i]),
        dskip=row(jnp.repeat(d_skip[i], HEAD_DIM)), snw=row(ssd_norm_w[i]), scw=sconv_w[i],
        wout=_LayerWeight(w_out, i))


def kernel(x_prompt, x_sample, state_ssm, state_ssd_conv, state_sconv, norm_ffn1_w, ffn1_w_gate, ffn1_w_up, ffn1_w_down, norm_mix_w, w_in, ssd_conv_w, ssd_conv_b, dt_bias, a_log, d_skip, ssd_norm_w, sconv_w, w_out, norm_ffn2_w, ffn2_w_gate, ffn2_w_up, ffn2_w_down, final_norm_w):
    depth = w_in.shape[0]
    bp, seq, d = x_prompt.shape
    bs, dec_seq, _ = x_sample.shape
    assert dec_seq == 1, "sample group is one token per sequence"
    fnw = final_norm_w.reshape(1, d)
    w_in_t = jnp.swapaxes(w_in, 1, 2)

    xp = x_prompt.reshape(bp * seq, d)
    xs = x_sample.reshape(bs, d)
    outs = [[] for _ in range(6)]
    for i in range(depth):
        p = _layer_params(i, norm_ffn1_w, ffn1_w_gate, ffn1_w_up, ffn1_w_down, norm_mix_w, w_in_t,
                          ssd_conv_w, ssd_conv_b, dt_bias, a_log, d_skip, ssd_norm_w, sconv_w,
                          w_out, norm_ffn2_w, ffn2_w_gate, ffn2_w_up, ffn2_w_down)
        last = i == depth - 1
        xp, xs = _ffn(xp, xs, *p["ffn1"], tile_m=FFN_TILE_M)
        xp3, s_p, c_p, sc_p = _mixer_prompt(
            xp.reshape(bp, seq, d), p["nw"], p["wint"], p["cw"], p["cb"], p["dtb"], p["alog"],
            p["dskip"], p["snw"], p["scw"], p["wout"])
        z, xs_conv, b_rows, c_rows, xdt_t, decay, ysc, c_s, sc_s = _sample_pre(
            xs, p["nw"], w_in_t[i], p["cw"], p["cb"], p["dtb"], p["alog"], p["scw"],
            state_ssd_conv[i], state_sconv[i])
        s_s, yraw = _sample_state(state_ssm[i].reshape(bs, SSD_WIDTH, D_STATE), xdt_t, decay, b_rows, c_rows)
        xs = _sample_post(xs, yraw, xs_conv, z, ysc, p["dskip"], p["snw"], w_out[i])
        xp, xs = _ffn(xp3.reshape(bp * seq, d), xs, *p["ffn2"], fnw if last else None, tile_m=FFN_TILE_M)
        for lst, v in zip(outs, (s_p.reshape(bp, N_HEADS, HEAD_DIM, D_STATE), c_p, sc_p,
                                 s_s.reshape(bs, N_HEADS, HEAD_DIM, D_STATE),
                                 c_s, sc_s)):
            lst.append(v)
    return (xp.reshape(bp, seq, d), xs.reshape(bs, dec_seq, d)) + tuple(jnp.stack(l) for l in outs)
```

```python
import functools

import jax
import jax.numpy as jnp
from jax import lax
from jax.experimental import pallas as pl
from jax.experimental.pallas import tpu as pltpu

F32 = jnp.float32
BF16 = jnp.bfloat16

D_MODEL = 1024
SSD_WIDTH = 1024
SC_WIDTH = 1024
HEAD_DIM = 64
N_HEADS = SSD_WIDTH // HEAD_DIM
N_GROUPS = 2
HEADS_PER_GROUP = N_HEADS // N_GROUPS
GROUP_WIDTH = SSD_WIDTH // N_GROUPS
D_STATE = 128
SSD_CONV_W = 4
SSD_CONV_DIM = SSD_WIDTH + 2 * N_GROUPS * D_STATE
SC_CONV_W = 3
CHUNK = 256
NORM_EPS = 1e-6

LANES = 128
SUBLANES = 8

OFF_Z = 0
OFF_XBC = OFF_Z + SSD_WIDTH
OFF_DT = OFF_XBC + SSD_CONV_DIM
OFF_SCB = OFF_DT + N_HEADS
OFF_SCC = OFF_SCB + SC_WIDTH
OFF_SCH = OFF_SCC + SC_WIDTH
D_IN_PROJ = OFF_SCH + SC_WIDTH
PROJ_BLOCK = 512

VMEM_LIMIT_BYTES = 56 * 1024 * 1024


def _rmsnorm(x, w):
    ms = jnp.mean(x * x, axis=-1, keepdims=True)
    return x * lax.rsqrt(ms + NORM_EPS) * w


def _silu(x):
    half = 0.5 * x
    return half * jnp.tanh(half) + half


def _silu_mixer(x):
    return x * (0.5 * jnp.tanh(0.5 * x) + 0.5)


def _softplus(x):
    return jnp.maximum(x, 0.0) + jnp.log1p(jnp.exp(-jnp.abs(x)))


def _dot(a, b):
    return jnp.dot(a, b, preferred_element_type=F32)


def _dot_nt(a, b):
    return lax.dot_general(a, b, (((1,), (1,)), ((), ())), preferred_element_type=F32)


def _split3(x):
    hi = x.astype(BF16)
    r1 = x - hi.astype(F32)
    mid = r1.astype(BF16)
    lo = (r1 - mid.astype(F32)).astype(BF16)
    return hi, mid, lo


def _expand_heads(v, rows):
    lane = lax.broadcasted_iota(jnp.int32, (rows, LANES), 1)
    first_half = lane < HEAD_DIM
    blocks = []
    for q in range(N_HEADS // 2):
        c0 = jnp.broadcast_to(v[:, 2 * q:2 * q + 1], (rows, LANES))
        c1 = jnp.broadcast_to(v[:, 2 * q + 1:2 * q + 2], (rows, LANES))
        blocks.append(jnp.where(first_half, c0, c1))
    return jnp.concatenate(blocks, axis=1)


def _group_rmsnorm(y, w):
    outs = []
    for g in range(N_GROUPS):
        sl = slice(g * GROUP_WIDTH, (g + 1) * GROUP_WIDTH)
        outs.append(_rmsnorm(y[:, sl], w[:, sl]))
    return jnp.concatenate(outs, axis=1)


def _shift_rows(x, prev_tile, j):
    rolled = pltpu.roll(x, j, 0)
    row = lax.broadcasted_iota(jnp.int32, prev_tile.shape, 0)
    head = jnp.where(row < j, pltpu.roll(prev_tile, j, 0), rolled[0:SUBLANES])
    return jnp.concatenate([head, rolled[SUBLANES:]], axis=0)


def _causal_conv(u, prev_tile, w):
    k = w.shape[0]
    acc = u * w[k - 1:k, :]
    for j in range(1, k):
        acc = acc + _shift_rows(u, prev_tile, j) * w[k - 1 - j:k - j, :]
    return acc


WEIGHT_LOAD_STEPS = 8


def _load_weight_rows(step, w_ref, w_bf_ref):
    rows = w_ref.shape[0]
    r0 = pl.multiple_of(step * rows, rows)
    w_bf_ref[pl.ds(r0, rows), :] = w_ref[...].astype(BF16)


def _weight_block_rows(rows, steps):
    tile = 2 * SUBLANES
    return -(-rows // (steps * tile)) * tile


def _weight_rows_spec(w, steps):
    _, rows, cols = w.stacked.shape
    return pl.BlockSpec((None, _weight_block_rows(rows, steps), cols),
                        lambda i: (w.layer, jnp.minimum(i, steps - 1), 0))


def _weight_scratch(w, steps):
    rows, cols = w.shape
    return pltpu.VMEM((steps * _weight_block_rows(rows, steps), cols), BF16)


class _LayerWeight:
    def __init__(self, stacked, layer):
        self.stacked, self.layer = stacked, layer
        self.shape = stacked.shape[1:]


FFN_HIDDEN_BLOCK = 256


def _ffn_kernel(*refs, final_norm):
    if final_norm:
        (xp_ref, xs_ref, nw_ref, wg_ref, wu_ref, wd_ref, fnw_ref, op_ref, os_ref,
         wg_bf, wu_bf, wd_bf) = refs
    else:
        xp_ref, xs_ref, nw_ref, wg_ref, wu_ref, wd_ref, op_ref, os_ref, wg_bf, wu_bf, wd_bf = refs
    weight_steps = wg_bf.shape[1] // FFN_HIDDEN_BLOCK

    def finish(x, ffn_out):
        y = x + 0.5 * ffn_out
        return _rmsnorm(y, fnw_ref[...]) if final_norm else y

    def sub_tiles(x_ref):
        rows = x_ref.shape[0]
        sub = min(rows, FFN_SUB_TILE_M)
        return [slice(r0, r0 + sub) for r0 in range(0, rows, sub)]

    def half_step(x_ref, o_ref):
        for rs in sub_tiles(x_ref):
            x = x_ref[rs, :]
            xn = _rmsnorm(x, nw_ref[...]).astype(BF16)
            g = _dot(xn, wg_bf[...])
            u = _dot(xn, wu_bf[...])
            hmid = (_silu(g) * u).astype(BF16)
            o_ref[rs, :] = finish(x, _dot(hmid, wd_bf[...]))

    def first_tile_step(k):
        cols = slice(k * FFN_HIDDEN_BLOCK, (k + 1) * FFN_HIDDEN_BLOCK)
        wg_k = wg_ref[...].astype(BF16)
        wu_k = wu_ref[...].astype(BF16)
        wd_k = wd_ref[...].astype(BF16)
        wg_bf[:, cols] = wg_k
        wu_bf[:, cols] = wu_k
        wd_bf[cols, :] = wd_k
        for rs in sub_tiles(xp_ref):
            x = xp_ref[rs, :]
            xn = _rmsnorm(x, nw_ref[...]).astype(BF16)
            hmid = (_silu(_dot(xn, wg_k)) * _dot(xn, wu_k)).astype(BF16)
            part = _dot(hmid, wd_k)
            if k > 0:
                part = op_ref[rs, :] + part
            op_ref[rs, :] = finish(x, part) if k == weight_steps - 1 else part

    i = pl.program_id(0)
    sample_step = pl.num_programs(0) - 1

    for k in range(weight_steps):
        pl.when(i == k)(functools.partial(first_tile_step, k))

    @pl.when((i >= weight_steps) & (i < sample_step))
    def _():
        half_step(xp_ref, op_ref)

    @pl.when(i == sample_step)
    def _():
        half_step(xs_ref, os_ref)


def _resident(shape):
    return pl.BlockSpec(shape, lambda *_: (0,) * len(shape), pipeline_mode=pl.Buffered(1))


def _ffn(xp, xs, nw, wg, wu, wd, fnw=None, *, tile_m):
    m, d = xp.shape
    ms = xs.shape[0]
    dff = wg.shape[1]
    assert m % tile_m == 0 and dff % FFN_HIDDEN_BLOCK == 0
    n = m // tile_m
    final_norm = fnw is not None
    ws = dff // FFN_HIDDEN_BLOCK
    prompt_tile = pl.BlockSpec((tile_m, d), lambda i: (jnp.clip(i - (ws - 1), 0, n - 1), 0))
    sample_tile = pl.BlockSpec((ms, d), lambda i: (0, 0))
    assert wg.layer == wu.layer == wd.layer
    hidden_cols = pl.BlockSpec((None, d, FFN_HIDDEN_BLOCK),
                               lambda i: (wg.layer, 0, jnp.minimum(i, ws - 1)))
    hidden_rows = pl.BlockSpec((None, FFN_HIDDEN_BLOCK, d),
                               lambda i: (wg.layer, jnp.minimum(i, ws - 1), 0))
    in_specs = [prompt_tile, sample_tile, _resident((1, d)), hidden_cols, hidden_cols, hidden_rows]
    args = [xp, xs, nw, wg.stacked, wu.stacked, wd.stacked]
    if final_norm:
        in_specs.append(_resident((1, d)))
        args.append(fnw)
    return pl.pallas_call(
        functools.partial(_ffn_kernel, final_norm=final_norm),
        out_shape=(jax.ShapeDtypeStruct((m, d), F32), jax.ShapeDtypeStruct((ms, d), F32)),
        grid=(ws + n,),
        in_specs=in_specs,
        out_specs=(prompt_tile, sample_tile),
        scratch_shapes=[pltpu.VMEM((d, dff), BF16), pltpu.VMEM((d, dff), BF16),
                        pltpu.VMEM((dff, d), BF16)],
        compiler_params=pltpu.CompilerParams(
            dimension_semantics=("arbitrary",), vmem_limit_bytes=VMEM_LIMIT_BYTES),
        name="ffn_final" if final_norm else "ffn",
    )(*args)


def _in_proj(h, wint_ref, col0, width):
    return _dot_nt(h, wint_ref[col0:col0 + width, :])


def _dt_proj(h, wint_ref):
    raw = _in_proj(h, wint_ref, OFF_DT, LANES)
    lane = lax.broadcasted_iota(jnp.int32, raw.shape, 1)
    return jnp.where(lane < N_HEADS, raw, 0.0)


def _mixer_kernel(tiles_per_seq, x_ref, nw_ref, wint_ref, cw_ref, cb_ref, dtb_ref, alog_ref,
                  dskip_ref, snw_ref, scw_ref, wout_ref,
                  o_ref, sfin_ref, cfin_ref, scfin_ref,
                  st_scr, hist_scr, uhist_scr, wint_bf, wout_bf):
    i = pl.program_id(0)

    @pl.when(i < MIXER_WEIGHT_LOAD_STEPS)
    def _():
        _load_weight_rows(i, wint_ref, wint_bf)
        _load_weight_rows(i, wout_ref, wout_bf)

    @pl.when(i >= MIXER_WEIGHT_LOAD_STEPS)
    def _():
        c = lax.rem(i - MIXER_WEIGHT_LOAD_STEPS, tiles_per_seq)
        _mixer_chunk(c == 0, c == tiles_per_seq - 1, x_ref, nw_ref, wint_bf, cw_ref,
                     cb_ref, dtb_ref, alog_ref, dskip_ref, snw_ref, scw_ref, wout_bf,
                     o_ref, sfin_ref, cfin_ref, scfin_ref, st_scr, hist_scr, uhist_scr)


def _mixer_chunk(first, last, x_ref, nw_ref, wint_ref, cw_ref, cb_ref, dtb_ref, alog_ref,
                 dskip_ref, snw_ref, scw_ref, wout_ref,
                 o_ref, sfin_ref, cfin_ref, scfin_ref,
                 st_scr, hist_scr, uhist_scr):
    L = CHUNK
    T = x_ref.shape[1]

    @pl.when(first)
    def _():
        st_scr[...] = jnp.zeros_like(st_scr)
        hist_scr[...] = jnp.zeros_like(hist_scr)
        uhist_scr[...] = jnp.zeros_like(uhist_scr)

    x = x_ref[0]
    h = _rmsnorm(x, nw_ref[...]).astype(BF16)
    dt_raw = _dt_proj(h, wint_ref)

    xbc_c_blocks = []
    for b0 in range(0, SSD_CONV_DIM, PROJ_BLOCK):
        cols = slice(b0, b0 + PROJ_BLOCK)
        xbc = _in_proj(h, wint_ref, OFF_XBC + b0, PROJ_BLOCK)
        acc = _causal_conv(xbc, hist_scr[:, cols], cw_ref[:, cols])
        hist_scr[:, cols] = xbc[T - SUBLANES:T, :]
        xbc_c_blocks.append(_silu_mixer(acc + cb_ref[:, cols]))
    cfin_ref[0] = hist_scr[SUBLANES - (SSD_CONV_W - 1):SUBLANES, :]
    xbc_c_all = jnp.concatenate(xbc_c_blocks, axis=1)

    def plus_zero_of(v, gate):
        bits = pltpu.bitcast(gate[0:SUBLANES, 0:LANES], jnp.uint32)
        half_word = jnp.uint32(16)
        zero_row = pltpu.bitcast(
            lax.shift_right_logical(lax.shift_right_logical(bits, half_word), half_word), F32)[0:1, :]
        return v + jnp.concatenate([zero_row] * (D_MODEL // LANES), axis=1).astype(BF16)

    dt_all = _softplus(dt_raw + dtb_ref[...])
    a_all = dt_all * (-jnp.exp(alog_ref[...]))
    row = lax.broadcasted_iota(jnp.int32, (L, L), 0)
    col = lax.broadcasted_iota(jnp.int32, (L, L), 1)
    causal = row >= col
    tril = jnp.where(causal, 1.0, 0.0).astype(BF16)
    lane = lax.broadcasted_iota(jnp.int32, (L, LANES), 1)
    first_half = lane < HEAD_DIM

    y_ssd_chunks = []
    out_sc_chunks = []
    for k in range(T // L):
        rows = slice(k * L, (k + 1) * L)
        xbc_c = xbc_c_all[rows]
        xs = xbc_c[:, 0:SSD_WIDTH]
        b_all = xbc_c[:, SSD_WIDTH:SSD_WIDTH + N_GROUPS * D_STATE]
        c_all = xbc_c[:, SSD_WIDTH + N_GROUPS * D_STATE:SSD_CONV_DIM]
        dt = dt_all[rows]
        a_hi, a_mid, a_lo = _split3(a_all[rows])
        acum = _dot(tril, a_hi) + _dot(tril, a_mid) + _dot(tril, a_lo)
        acum_t = acum.T

        def h_after(gate):
            return plus_zero_of(h[rows], gate)

        def z_gate_after(gate):
            h_late = h_after(gate)
            return jnp.concatenate(
                [_silu_mixer(_in_proj(h_late, wint_ref, OFF_Z + b0, PROJ_BLOCK))
                 for b0 in range(0, SSD_WIDTH, PROJ_BLOCK)], axis=1)

        def short_conv_block_after(gate, b0):
            h_late = h_after(gate)
            cols = slice(b0, b0 + PROJ_BLOCK)
            scc = _in_proj(h_late, wint_ref, OFF_SCC + b0, PROJ_BLOCK)
            sch = _in_proj(h_late, wint_ref, OFF_SCH + b0, PROJ_BLOCK)
            u = scc * sch
            v = _causal_conv(u, uhist_scr[:, cols], scw_ref[:, cols])
            uhist_scr[:, cols] = u[L - SUBLANES:L, :]
            scb = _in_proj(h_late, wint_ref, OFF_SCB + b0, PROJ_BLOCK)
            return (scb * v).astype(BF16)

        z_gate = z_gate_after(acum)
        y_sc = jnp.concatenate(
            [short_conv_block_after(acum, b0) for b0 in range(0, SC_WIDTH, PROJ_BLOCK)], axis=1)
        out_sc_chunks.append(_dot(y_sc, wout_ref[SSD_WIDTH:SSD_WIDTH + SC_WIDTH, :]))

        acum_e = _expand_heads(acum, L)
        dt_e = _expand_heads(dt, L)
        exp_acum_e = jnp.exp(acum_e)
        decay_end_e = jnp.exp(acum_e[L - 1:L, :] - acum_e)
        xdt = xs * dt_e
        xdt_b = xdt.astype(BF16)
        xdecay_b = (xdt * decay_end_e).astype(BF16)

        y_diag_blocks = []
        y_off_blocks = []
        for g in range(N_GROUPS):
            b_g = b_all[:, g * D_STATE:(g + 1) * D_STATE]
            c_g = c_all[:, g * D_STATE:(g + 1) * D_STATE]
            b_gb = b_g.astype(BF16)
            c_gb = c_g.astype(BF16)
            cb = _dot_nt(c_gb, b_gb)
            for q in range(g * HEADS_PER_GROUP // 2, (g + 1) * HEADS_PER_GROUP // 2):
                ms = []
                for hh in (2 * q, 2 * q + 1):
                    seg = acum[:, hh:hh + 1] - acum_t[hh:hh + 1, :]
                    decay = jnp.exp(jnp.where(causal, seg, -jnp.inf))
                    ms.append((cb * decay).astype(BF16))
                m_cat = jnp.concatenate(ms, axis=1)
                x2 = xdt_b[:, q * LANES:(q + 1) * LANES]
                zero = jnp.zeros_like(x2)
                rhs = jnp.concatenate([jnp.where(first_half, x2, zero),
                                       jnp.where(first_half, zero, x2)], axis=0)
                y_diag_blocks.append(_dot(m_cat, rhs))
            gs = slice(g * GROUP_WIDTH, (g + 1) * GROUP_WIDTH)
            s_enter = st_scr[:, gs]
            y_off_blocks.append(_dot(c_gb, s_enter.astype(BF16)))
            new_states = _dot(b_g.T.astype(BF16), xdecay_b[:, gs])
            st_scr[:, gs] = s_enter * exp_acum_e[L - 1:L, gs] + new_states
        y_diag = jnp.concatenate(y_diag_blocks, axis=1)
        y_off = jnp.concatenate(y_off_blocks, axis=1) * exp_acum_e

        y = (y_diag + y_off + dskip_ref[...] * xs) * z_gate
        y_ssd_chunks.append(_group_rmsnorm(y, snw_ref[...]).astype(BF16))

    scfin_ref[0] = uhist_scr[SUBLANES - (SC_CONV_W - 1):SUBLANES, :]
    out_ssd = _dot(jnp.concatenate(y_ssd_chunks, axis=0), wout_ref[0:SSD_WIDTH, :])
    o_ref[0] = x + jnp.concatenate(out_sc_chunks, axis=0) + out_ssd

    @pl.when(last)
    def _():
        sfin_ref[0] = st_scr[...].T


MIXER_CHUNKS_PER_STEP = 2
MIXER_WEIGHT_LOAD_STEPS = WEIGHT_LOAD_STEPS


def _mixer_prompt(x, nw, wint, cw, cb, dtb, alog, dskip, snw, scw, wout):
    nb, seq, d = x.shape
    tile_t = MIXER_CHUNKS_PER_STEP * CHUNK
    assert seq % tile_t == 0
    nc = seq // tile_t
    ws = MIXER_WEIGHT_LOAD_STEPS

    def chunk_map(i):
        t = jnp.maximum(i - ws, 0)
        return (t // nc, t % nc, 0)

    def seq_map(i):
        return (jnp.maximum(i - ws, 0) // nc, 0, 0)

    out_shape = (
        jax.ShapeDtypeStruct((nb, seq, d), F32),
        jax.ShapeDtypeStruct((nb, SSD_WIDTH, D_STATE), F32),
        jax.ShapeDtypeStruct((nb, SSD_CONV_W - 1, SSD_CONV_DIM), F32),
        jax.ShapeDtypeStruct((nb, SC_CONV_W - 1, SC_WIDTH), F32),
    )
    return pl.pallas_call(
        functools.partial(_mixer_kernel, nc),
        out_shape=out_shape,
        grid=(ws + nb * nc,),
        in_specs=[pl.BlockSpec((1, tile_t, d), chunk_map),
                  _resident((1, d)), _weight_rows_spec(wint, ws),
                  _resident((SSD_CONV_W, SSD_CONV_DIM)), _resident((1, SSD_CONV_DIM)),
                  _resident((1, LANES)), _resident((1, LANES)), _resident((1, SSD_WIDTH)),
                  _resident((1, SSD_WIDTH)), _resident((SC_CONV_W, SC_WIDTH)),
                  _weight_rows_spec(wout, ws)],
        out_specs=(pl.BlockSpec((1, tile_t, d), chunk_map),
                   pl.BlockSpec((1, SSD_WIDTH, D_STATE), seq_map),
                   pl.BlockSpec((1, SSD_CONV_W - 1, SSD_CONV_DIM), seq_map),
                   pl.BlockSpec((1, SC_CONV_W - 1, SC_WIDTH), seq_map)),
        scratch_shapes=[pltpu.VMEM((D_STATE, SSD_WIDTH), F32),
                        pltpu.VMEM((SUBLANES, SSD_CONV_DIM), F32),
                        pltpu.VMEM((SUBLANES, SC_WIDTH), F32),
                        _weight_scratch(wint, ws), _weight_scratch(wout, ws)],
        compiler_params=pltpu.CompilerParams(
            dimension_semantics=("arbitrary",), vmem_limit_bytes=VMEM_LIMIT_BYTES),
        name="mixer_prompt",
    )(x, nw, wint.stacked, cw, cb, dtb, alog, dskip, snw, scw, wout.stacked)


def _sample_pre_kernel(x_ref, nw_ref, wint_ref, cw_ref, cb_ref, dtb_ref, alog_ref,
                       scw_ref, cst_ref, scst_ref,
                       z_ref, xs_ref, b_ref, c_ref, xdt_blk_ref, da_ref, ysc_ref, cnew_ref, scnew_ref,
                       wint_bf):
    nb = x_ref.shape[0]
    x = x_ref[...]
    h = _rmsnorm(x, nw_ref[...]).astype(BF16)
    wint_bf[...] = wint_ref[...].astype(BF16)
    dt_raw = _dt_proj(h, wint_bf)
    xbc = _in_proj(h, wint_bf, OFF_XBC, SSD_CONV_DIM)
    z_ref[...] = _in_proj(h, wint_bf, OFF_Z, SSD_WIDTH)
    scb = _in_proj(h, wint_bf, OFF_SCB, SC_WIDTH)
    scc = _in_proj(h, wint_bf, OFF_SCC, SC_WIDTH)
    sch = _in_proj(h, wint_bf, OFF_SCH, SC_WIDTH)

    cw = cw_ref[...]
    acc = xbc * cw[SSD_CONV_W - 1:SSD_CONV_W, :]
    for k in range(SSD_CONV_W - 1):
        acc = acc + cst_ref[:, k, :] * cw[k:k + 1, :]
    for k in range(SSD_CONV_W - 2):
        cnew_ref[:, k, :] = cst_ref[:, k + 1, :]
    cnew_ref[:, SSD_CONV_W - 2, :] = xbc
    xbc_c = _silu(acc + cb_ref[...])
    xs = xbc_c[:, 0:SSD_WIDTH]
    xs_ref[...] = xs
    b_ref[...] = xbc_c[:, SSD_WIDTH:SSD_WIDTH + N_GROUPS * D_STATE]
    c_ref[...] = xbc_c[:, SSD_WIDTH + N_GROUPS * D_STATE:SSD_CONV_DIM]

    dt = _softplus(dt_raw + dtb_ref[...])
    da_ref[...] = jnp.exp(dt * (-jnp.exp(alog_ref[...])))
    xdt_t = (xs * _expand_heads(dt, nb)).T
    bb = xdt_blk_ref.shape[2]
    for j in range(nb // bb):
        xdt_blk_ref[j] = xdt_t[:, j * bb:(j + 1) * bb]

    u = scc * sch
    scw = scw_ref[...]
    v = u * scw[SC_CONV_W - 1:SC_CONV_W, :]
    for k in range(SC_CONV_W - 1):
        v = v + scst_ref[:, k, :] * scw[k:k + 1, :]
    for k in range(SC_CONV_W - 2):
        scnew_ref[:, k, :] = scst_ref[:, k + 1, :]
    scnew_ref[:, SC_CONV_W - 2, :] = u
    ysc_ref[...] = scb * v


def _sample_pre(x, nw, wint, cw, cb, dtb, alog, scw, cst, scst):
    nb, d = x.shape
    f = lambda *s: jax.ShapeDtypeStruct(s, F32)
    bb = STATE_BATCH_BLOCK
    assert nb % bb == 0
    out_shape = (f(nb, SSD_WIDTH), f(nb, SSD_WIDTH), f(nb, N_GROUPS * D_STATE), f(nb, N_GROUPS * D_STATE),
                 f(nb // bb, SSD_WIDTH, bb), f(nb, LANES), f(nb, SC_WIDTH),
                 f(nb, SSD_CONV_W - 1, SSD_CONV_DIM), f(nb, SC_CONV_W - 1, SC_WIDTH))
    return pl.pallas_call(
        _sample_pre_kernel,
        out_shape=out_shape,
        scratch_shapes=[pltpu.VMEM(wint.shape, BF16)],
        compiler_params=pltpu.CompilerParams(vmem_limit_bytes=VMEM_LIMIT_BYTES),
        name="sample_pre",
    )(x, nw, wint, cw, cb, dtb, alog, scw, cst, scst)


STATE_BATCH_BLOCK = 16


def _split2(x):
    hi = x.astype(BF16)
    return hi, (x - hi.astype(F32)).astype(BF16)


def _sample_state_kernel(da_ref, s0_ref, xdt_ref, b_ref, c_ref, snew_ref, y_ref):
    bb = STATE_BATCH_BLOCK
    blk = pl.program_id(0)
    xdt_t = xdt_ref[0]
    b_rows = b_ref[...]
    c_hi, c_lo = _split2(c_ref[...])
    row = lax.broadcasted_iota(jnp.int32, (bb, SSD_WIDTH), 0)
    y_blk = jnp.zeros((bb, SSD_WIDTH), F32)
    for i in range(bb):
        xdt_col = xdt_t[:, i:i + 1]
        y_parts = []
        for g in range(N_GROUPS):
            ns = slice(g * D_STATE, (g + 1) * D_STATE)
            b_row = b_rows[i:i + 1, ns]
            heads = []
            for hh in range(g * HEADS_PER_GROUP, (g + 1) * HEADS_PER_GROUP):
                rs = slice(hh * HEAD_DIM, (hh + 1) * HEAD_DIM)
                decay = da_ref[blk * bb + i, hh]
                heads.append(s0_ref[i, rs, :] * decay + xdt_col[rs] * b_row)
            s_new = jnp.concatenate(heads, axis=0)
            snew_ref[i, g * GROUP_WIDTH:(g + 1) * GROUP_WIDTH, :] = s_new
            s_hi, s_lo = _split2(s_new)
            lhs = jnp.concatenate([c_hi[:, ns], c_lo[:, ns]], axis=0)
            r_hi = _dot_nt(lhs, s_hi)
            r_lo = _dot_nt(c_hi[:, ns], s_lo)
            y_parts.append(r_hi[i:i + 1] + r_hi[bb + i:bb + i + 1] + r_lo[i:i + 1])
        y_row = jnp.concatenate(y_parts, axis=1)
        y_blk = jnp.where(row == i, y_row, y_blk)
    y_ref[...] = y_blk


def _sample_state(s0, xdt_blocks, decay, b_rows, c_rows):
    nb = s0.shape[0]
    nblk, _, bb = xdt_blocks.shape
    assert bb == STATE_BATCH_BLOCK and nblk * bb == nb
    return pl.pallas_call(
        _sample_state_kernel,
        out_shape=(jax.ShapeDtypeStruct((nb, SSD_WIDTH, D_STATE), F32),
                   jax.ShapeDtypeStruct((nb, SSD_WIDTH), F32)),
        grid=(nblk,),
        in_specs=[pl.BlockSpec(memory_space=pltpu.SMEM),
                  pl.BlockSpec((bb, SSD_WIDTH, D_STATE), lambda i: (i, 0, 0)),
                  pl.BlockSpec((1, SSD_WIDTH, bb), lambda i: (i, 0, 0)),
                  pl.BlockSpec((bb, N_GROUPS * D_STATE), lambda i: (i, 0)),
                  pl.BlockSpec((bb, N_GROUPS * D_STATE), lambda i: (i, 0))],
        out_specs=(pl.BlockSpec((bb, SSD_WIDTH, D_STATE), lambda i: (i, 0, 0)),
                   pl.BlockSpec((bb, SSD_WIDTH), lambda i: (i, 0))),
        compiler_params=pltpu.CompilerParams(
            dimension_semantics=("arbitrary",), vmem_limit_bytes=VMEM_LIMIT_BYTES),
        name="sample_state",
    )(decay, s0, xdt_blocks, b_rows, c_rows)


def _sample_post_kernel(x_ref, yraw_ref, xs_ref, z_ref, ysc_ref, dskip_ref, snw_ref, wout_ref, o_ref):
    y = yraw_ref[...] + dskip_ref[...] * xs_ref[...]
    y = y * _silu(z_ref[...])
    y_ssd = _group_rmsnorm(y, snw_ref[...])
    mixed = jnp.concatenate([y_ssd, ysc_ref[...]], axis=1).astype(BF16)
    o_ref[...] = x_ref[...] + _dot(mixed, wout_ref[...].astype(BF16))


def _sample_post(x, yraw, xs, z, ysc, dskip, snw, wout):
    return pl.pallas_call(
        _sample_post_kernel,
        out_shape=jax.ShapeDtypeStruct(x.shape, F32),
        compiler_params=pltpu.CompilerParams(vmem_limit_bytes=VMEM_LIMIT_BYTES),
        name="sample_post",
    )(x, yraw, xs, z, ysc, dskip, snw, wout)


FFN_TILE_M = 1024
FFN_SUB_TILE_M = 512


def _layer_params(i, norm_ffn1_w, ffn1_w_gate, ffn1_w_up, ffn1_w_down, norm_mix_w, w_in_t,
                  ssd_conv_w, ssd_conv_b, dt_bias, a_log, d_skip, ssd_norm_w, sconv_w, w_out,
                  norm_ffn2_w, ffn2_w_gate, ffn2_w_up, ffn2_w_down):
    pad_heads = lambda v: jnp.pad(v, (0, LANES - N_HEADS)).reshape(1, LANES)
    row = lambda v: v.reshape(1, -1)
    return dict(
        ffn1=(row(norm_ffn1_w[i]), _LayerWeight(ffn1_w_gate, i), _LayerWeight(ffn1_w_up, i),
              _LayerWeight(ffn1_w_down, i)),
        ffn2=(row(norm_ffn2_w[i]), _LayerWeight(ffn2_w_gate, i), _LayerWeight(ffn2_w_up, i),
              _LayerWeight(ffn2_w_down, i)),
        nw=row(norm_mix_w[i]), wint=_LayerWeight(w_in_t, i), cw=ssd_conv_w[i], cb=row(ssd_conv_b[i]),
        dtb=pad_heads(dt_bias[i]), alog=pad_heads(a_log[i]),
        dskip=row(jnp.repeat(d_skip[i], HEAD_DIM)), snw=row(ssd_norm_w[i]), scw=sconv_w[i],
        wout=_LayerWeight(w_out, i))


def kernel(x_prompt, x_sample, state_ssm, state_ssd_conv, state_sconv, norm_ffn1_w, ffn1_w_gate, ffn1_w_up, ffn1_w_down, norm_mix_w, w_in, ssd_conv_w, ssd_conv_b, dt_bias, a_log, d_skip, ssd_norm_w, sconv_w, w_out, norm_ffn2_w, ffn2_w_gate, ffn2_w_up, ffn2_w_down, final_norm_w):
    depth = w_in.shape[0]
    bp, seq, d = x_prompt.shape
    bs, dec_seq, _ = x_sample.shape
    assert dec_seq == 1, "sample group is one token per sequence"
    fnw = final_norm_w.reshape(1, d)
    w_in_t = jnp.swapaxes(w_in, 1, 2)

    xp = x_prompt.reshape(bp * seq, d)
    xs = x_sample.reshape(bs, d)
    outs = [[] for _ in range(6)]
    for i in range(depth):
        p = _layer_params(i, norm_ffn1_w, ffn1_w_gate, ffn1_w_up, ffn1_w_down, norm_mix_w, w_in_t,
                          ssd_conv_w, ssd_conv_b, dt_bias, a_log, d_skip, ssd_norm_w, sconv_w,
                          w_out, norm_ffn2_w, ffn2_w_gate, ffn2_w_up, ffn2_w_down)
        last = i == depth - 1
        xp, xs = _ffn(xp, xs, *p["ffn1"], tile_m=FFN_TILE_M)
        xp3, s_p, c_p, sc_p = _mixer_prompt(
            xp.reshape(bp, seq, d), p["nw"], p["wint"], p["cw"], p["cb"], p["dtb"], p["alog"],
            p["dskip"], p["snw"], p["scw"], p["wout"])
        z, xs_conv, b_rows, c_rows, xdt_t, decay, ysc, c_s, sc_s = _sample_pre(
            xs, p["nw"], w_in_t[i], p["cw"], p["cb"], p["dtb"], p["alog"], p["scw"],
            state_ssd_conv[i], state_sconv[i])
        s_s, yraw = _sample_state(state_ssm[i].reshape(bs, SSD_WIDTH, D_STATE), xdt_t, decay, b_rows, c_rows)
        xs = _sample_post(xs, yraw, xs_conv, z, ysc, p["dskip"], p["snw"], w_out[i])
        xp, xs = _ffn(xp3.reshape(bp * seq, d), xs, *p["ffn2"], fnw if last else None, tile_m=FFN_TILE_M)
        for lst, v in zip(outs, (s_p.reshape(bp, N_HEADS, HEAD_DIM, D_STATE), c_p, sc_p,
                                 s_s.reshape(bs, N_HEADS, HEAD_DIM, D_STATE),
                                 c_s, sc_s)):
            lst.append(v)
    return (xp.reshape(bp, seq, d), xs.reshape(bs, dec_seq, d)) + tuple(jnp.stack(l) for l in outs)
```

```python
import functools

import jax
import jax.numpy as jnp
from jax import lax
from jax.experimental import pallas as pl
from jax.experimental.pallas import tpu as pltpu

F32 = jnp.float32
BF16 = jnp.bfloat16

D_MODEL = 1024
SSD_WIDTH = 1024
SC_WIDTH = 1024
HEAD_DIM = 64
N_HEADS = SSD_WIDTH // HEAD_DIM
N_GROUPS = 2
HEADS_PER_GROUP = N_HEADS // N_GROUPS
GROUP_WIDTH = SSD_WIDTH // N_GROUPS
D_STATE = 128
SSD_CONV_W = 4
SSD_CONV_DIM = SSD_WIDTH + 2 * N_GROUPS * D_STATE
SC_CONV_W = 3
CHUNK = 256
NORM_EPS = 1e-6

LANES = 128
SUBLANES = 8

OFF_Z = 0
OFF_XBC = OFF_Z + SSD_WIDTH
OFF_DT = OFF_XBC + SSD_CONV_DIM
OFF_SCB = OFF_DT + N_HEADS
OFF_SCC = OFF_SCB + SC_WIDTH
OFF_SCH = OFF_SCC + SC_WIDTH
D_IN_PROJ = OFF_SCH + SC_WIDTH
PROJ_BLOCK = 512

VMEM_LIMIT_BYTES = 56 * 1024 * 1024


def _rmsnorm(x, w):
    ms = jnp.mean(x * x, axis=-1, keepdims=True)
    return x * lax.rsqrt(ms + NORM_EPS) * w


def _silu(x):
    half = 0.5 * x
    return half * jnp.tanh(half) + half


def _silu_mixer(x):
    return x * (0.5 * jnp.tanh(0.5 * x) + 0.5)


def _softplus(x):
    return jnp.maximum(x, 0.0) + jnp.log1p(jnp.exp(-jnp.abs(x)))


def _dot(a, b):
    return jnp.dot(a, b, preferred_element_type=F32)


def _dot_nt(a, b):
    return lax.dot_general(a, b, (((1,), (1,)), ((), ())), preferred_element_type=F32)


def _split3(x):
    hi = x.astype(BF16)
    r1 = x - hi.astype(F32)
    mid = r1.astype(BF16)
    lo = (r1 - mid.astype(F32)).astype(BF16)
    return hi, mid, lo


def _expand_heads(v, rows):
    lane = lax.broadcasted_iota(jnp.int32, (rows, LANES), 1)
    first_half = lane < HEAD_DIM
    blocks = []
    for q in range(N_HEADS // 2):
        c0 = jnp.broadcast_to(v[:, 2 * q:2 * q + 1], (rows, LANES))
        c1 = jnp.broadcast_to(v[:, 2 * q + 1:2 * q + 2], (rows, LANES))
        blocks.append(jnp.where(first_half, c0, c1))
    return jnp.concatenate(blocks, axis=1)


def _group_rmsnorm(y, w):
    outs = []
    for g in range(N_GROUPS):
        sl = slice(g * GROUP_WIDTH, (g + 1) * GROUP_WIDTH)
        outs.append(_rmsnorm(y[:, sl], w[:, sl]))
    return jnp.concatenate(outs, axis=1)


def _shift_rows(x, prev_tile, j):
    rolled = pltpu.roll(x, j, 0)
    row = lax.broadcasted_iota(jnp.int32, prev_tile.shape, 0)
    head = jnp.where(row < j, pltpu.roll(prev_tile, j, 0), rolled[0:SUBLANES])
    return jnp.concatenate([head, rolled[SUBLANES:]], axis=0)


def _causal_conv(u, prev_tile, w):
    k = w.shape[0]
    acc = u * w[k - 1:k, :]
    for j in range(1, k):
        acc = acc + _shift_rows(u, prev_tile, j) * w[k - 1 - j:k - j, :]
    return acc


WEIGHT_LOAD_STEPS = 8


def _load_weight_rows(step, w_ref, w_bf_ref):
    rows = w_ref.shape[0]
    r0 = pl.multiple_of(step * rows, rows)
    w_bf_ref[pl.ds(r0, rows), :] = w_ref[...].astype(BF16)


def _weight_block_rows(rows, steps):
    tile = 2 * SUBLANES
    return -(-rows // (steps * tile)) * tile


def _weight_rows_spec(w, steps):
    _, rows, cols = w.stacked.shape
    return pl.BlockSpec((None, _weight_block_rows(rows, steps), cols),
                        lambda i: (w.layer, jnp.minimum(i, steps - 1), 0))


def _weight_scratch(w, steps):
    rows, cols = w.shape
    return pltpu.VMEM((steps * _weight_block_rows(rows, steps), cols), BF16)


class _LayerWeight:
    def __init__(self, stacked, layer):
        self.stacked, self.layer = stacked, layer
        self.shape = stacked.shape[1:]


def _ffn_kernel(*refs, final_norm):
    if final_norm:
        (xp_ref, xs_ref, nw_ref, wg_ref, wu_ref, wd_ref, fnw_ref, op_ref, os_ref,
         wg_bf, wu_bf, wd_bf) = refs
    else:
        xp_ref, xs_ref, nw_ref, wg_ref, wu_ref, wd_ref, op_ref, os_ref, wg_bf, wu_bf, wd_bf = refs

    def half_step(x_ref, o_ref):
        rows = x_ref.shape[0]
        sub = min(rows, FFN_SUB_TILE_M)
        for r0 in range(0, rows, sub):
            x = x_ref[r0:r0 + sub, :]
            xn = _rmsnorm(x, nw_ref[...]).astype(BF16)
            g = _dot(xn, wg_bf[...])
            u = _dot(xn, wu_bf[...])
            hmid = (_silu(g) * u).astype(BF16)
            y = x + 0.5 * _dot(hmid, wd_bf[...])
            if final_norm:
                y = _rmsnorm(y, fnw_ref[...])
            o_ref[r0:r0 + sub, :] = y

    i = pl.program_id(0)
    sample_step = pl.num_programs(0) - 1

    @pl.when(i < WEIGHT_LOAD_STEPS)
    def _():
        for w_ref, w_bf in ((wg_ref, wg_bf), (wu_ref, wu_bf), (wd_ref, wd_bf)):
            _load_weight_rows(i, w_ref, w_bf)

    @pl.when((i >= WEIGHT_LOAD_STEPS) & (i < sample_step))
    def _():
        half_step(xp_ref, op_ref)

    @pl.when(i == sample_step)
    def _():
        half_step(xs_ref, os_ref)


def _resident(shape):
    return pl.BlockSpec(shape, lambda *_: (0,) * len(shape), pipeline_mode=pl.Buffered(1))


def _ffn(xp, xs, nw, wg, wu, wd, fnw=None, *, tile_m):
    m, d = xp.shape
    ms = xs.shape[0]
    dff = wg.shape[1]
    assert m % tile_m == 0
    n = m // tile_m
    final_norm = fnw is not None
    ws = WEIGHT_LOAD_STEPS
    prompt_tile = pl.BlockSpec((tile_m, d), lambda i: (jnp.clip(i - ws, 0, n - 1), 0))
    sample_tile = pl.BlockSpec((ms, d), lambda i: (0, 0))
    in_specs = [prompt_tile, sample_tile, _resident((1, d)),
                _weight_rows_spec(wg, ws), _weight_rows_spec(wu, ws), _weight_rows_spec(wd, ws)]
    args = [xp, xs, nw, wg.stacked, wu.stacked, wd.stacked]
    if final_norm:
        in_specs.append(_resident((1, d)))
        args.append(fnw)
    return pl.pallas_call(
        functools.partial(_ffn_kernel, final_norm=final_norm),
        out_shape=(jax.ShapeDtypeStruct((m, d), F32), jax.ShapeDtypeStruct((ms, d), F32)),
        grid=(ws + n + 1,),
        in_specs=in_specs,
        out_specs=(prompt_tile, sample_tile),
        scratch_shapes=[_weight_scratch(wg, ws), _weight_scratch(wu, ws), _weight_scratch(wd, ws)],
        compiler_params=pltpu.CompilerParams(
            dimension_semantics=("arbitrary",), vmem_limit_bytes=VMEM_LIMIT_BYTES),
        name="ffn_final" if final_norm else "ffn",
    )(*args)


def _in_proj(h, wint_ref, col0, width):
    return _dot_nt(h, wint_ref[col0:col0 + width, :])


def _dt_proj(h, wint_ref):
    raw = _in_proj(h, wint_ref, OFF_DT, LANES)
    lane = lax.broadcasted_iota(jnp.int32, raw.shape, 1)
    return jnp.where(lane < N_HEADS, raw, 0.0)


def _mixer_kernel(tiles_per_seq, x_ref, nw_ref, wint_ref, cw_ref, cb_ref, dtb_ref, alog_ref,
                  dskip_ref, snw_ref, scw_ref, wout_ref,
                  o_ref, sfin_ref, cfin_ref, scfin_ref,
                  st_scr, hist_scr, uhist_scr, wint_bf, wout_bf):
    i = pl.program_id(0)

    @pl.when(i < MIXER_WEIGHT_LOAD_STEPS)
    def _():
        _load_weight_rows(i, wint_ref, wint_bf)
        _load_weight_rows(i, wout_ref, wout_bf)

    @pl.when(i >= MIXER_WEIGHT_LOAD_STEPS)
    def _():
        c = lax.rem(i - MIXER_WEIGHT_LOAD_STEPS, tiles_per_seq)
        _mixer_chunk(c == 0, c == tiles_per_seq - 1, x_ref, nw_ref, wint_bf, cw_ref,
                     cb_ref, dtb_ref, alog_ref, dskip_ref, snw_ref, scw_ref, wout_bf,
                     o_ref, sfin_ref, cfin_ref, scfin_ref, st_scr, hist_scr, uhist_scr)


def _mixer_chunk(first, last, x_ref, nw_ref, wint_ref, cw_ref, cb_ref, dtb_ref, alog_ref,
                 dskip_ref, snw_ref, scw_ref, wout_ref,
                 o_ref, sfin_ref, cfin_ref, scfin_ref,
                 st_scr, hist_scr, uhist_scr):
    L = CHUNK
    T = x_ref.shape[1]

    @pl.when(first)
    def _():
        st_scr[...] = jnp.zeros_like(st_scr)
        hist_scr[...] = jnp.zeros_like(hist_scr)
        uhist_scr[...] = jnp.zeros_like(uhist_scr)

    x = x_ref[0]
    h = _rmsnorm(x, nw_ref[...]).astype(BF16)
    dt_raw = _dt_proj(h, wint_ref)

    xbc_c_blocks = []
    for b0 in range(0, SSD_CONV_DIM, PROJ_BLOCK):
        cols = slice(b0, b0 + PROJ_BLOCK)
        xbc = _in_proj(h, wint_ref, OFF_XBC + b0, PROJ_BLOCK)
        acc = _causal_conv(xbc, hist_scr[:, cols], cw_ref[:, cols])
        hist_scr[:, cols] = xbc[T - SUBLANES:T, :]
        xbc_c_blocks.append(_silu_mixer(acc + cb_ref[:, cols]))
    cfin_ref[0] = hist_scr[SUBLANES - (SSD_CONV_W - 1):SUBLANES, :]
    xbc_c_all = jnp.concatenate(xbc_c_blocks, axis=1)

    def plus_zero_of(v, gate):
        bits = pltpu.bitcast(gate[0:SUBLANES, 0:LANES], jnp.uint32)
        half_word = jnp.uint32(16)
        zero_row = pltpu.bitcast(
            lax.shift_right_logical(lax.shift_right_logical(bits, half_word), half_word), F32)[0:1, :]
        return v + jnp.concatenate([zero_row] * (D_MODEL // LANES), axis=1).astype(BF16)

    dt_all = _softplus(dt_raw + dtb_ref[...])
    a_all = dt_all * (-jnp.exp(alog_ref[...]))
    row = lax.broadcasted_iota(jnp.int32, (L, L), 0)
    col = lax.broadcasted_iota(jnp.int32, (L, L), 1)
    causal = row >= col
    tril = jnp.where(causal, 1.0, 0.0).astype(BF16)
    lane = lax.broadcasted_iota(jnp.int32, (L, LANES), 1)
    first_half = lane < HEAD_DIM

    y_ssd_chunks = []
    out_sc_chunks = []
    for k in range(T // L):
        rows = slice(k * L, (k + 1) * L)
        xbc_c = xbc_c_all[rows]
        xs = xbc_c[:, 0:SSD_WIDTH]
        b_all = xbc_c[:, SSD_WIDTH:SSD_WIDTH + N_GROUPS * D_STATE]
        c_all = xbc_c[:, SSD_WIDTH + N_GROUPS * D_STATE:SSD_CONV_DIM]
        dt = dt_all[rows]
        a_hi, a_mid, a_lo = _split3(a_all[rows])
        acum = _dot(tril, a_hi) + _dot(tril, a_mid) + _dot(tril, a_lo)
        acum_t = acum.T

        def h_after(gate):
            return plus_zero_of(h[rows], gate)

        def z_gate_after(gate):
            h_late = h_after(gate)
            return jnp.concatenate(
                [_silu_mixer(_in_proj(h_late, wint_ref, OFF_Z + b0, PROJ_BLOCK))
                 for b0 in range(0, SSD_WIDTH, PROJ_BLOCK)], axis=1)

        def short_conv_block_after(gate, b0):
            h_late = h_after(gate)
            cols = slice(b0, b0 + PROJ_BLOCK)
            scc = _in_proj(h_late, wint_ref, OFF_SCC + b0, PROJ_BLOCK)
            sch = _in_proj(h_late, wint_ref, OFF_SCH + b0, PROJ_BLOCK)
            u = scc * sch
            v = _causal_conv(u, uhist_scr[:, cols], scw_ref[:, cols])
            uhist_scr[:, cols] = u[L - SUBLANES:L, :]
            scb = _in_proj(h_late, wint_ref, OFF_SCB + b0, PROJ_BLOCK)
            return (scb * v).astype(BF16)

        z_gate = z_gate_after(acum)
        y_sc = jnp.concatenate(
            [short_conv_block_after(acum, b0) for b0 in range(0, SC_WIDTH, PROJ_BLOCK)], axis=1)
        out_sc_chunks.append(_dot(y_sc, wout_ref[SSD_WIDTH:SSD_WIDTH + SC_WIDTH, :]))

        acum_e = _expand_heads(acum, L)
        dt_e = _expand_heads(dt, L)
        exp_acum_e = jnp.exp(acum_e)
        decay_end_e = jnp.exp(acum_e[L - 1:L, :] - acum_e)
        xdt = xs * dt_e
        xdt_b = xdt.astype(BF16)
        xdecay_b = (xdt * decay_end_e).astype(BF16)

        y_diag_blocks = []
        y_off_blocks = []
        for g in range(N_GROUPS):
            b_g = b_all[:, g * D_STATE:(g + 1) * D_STATE]
            c_g = c_all[:, g * D_STATE:(g + 1) * D_STATE]
            b_gb = b_g.astype(BF16)
            c_gb = c_g.astype(BF16)
            cb = _dot_nt(c_gb, b_gb)
            for q in range(g * HEADS_PER_GROUP // 2, (g + 1) * HEADS_PER_GROUP // 2):
                ms = []
                for hh in (2 * q, 2 * q + 1):
                    seg = acum[:, hh:hh + 1] - acum_t[hh:hh + 1, :]
                    decay = jnp.exp(jnp.where(causal, seg, -jnp.inf))
                    ms.append((cb * decay).astype(BF16))
                m_cat = jnp.concatenate(ms, axis=1)
                x2 = xdt_b[:, q * LANES:(q + 1) * LANES]
                zero = jnp.zeros_like(x2)
                rhs = jnp.concatenate([jnp.where(first_half, x2, zero),
                                       jnp.where(first_half, zero, x2)], axis=0)
                y_diag_blocks.append(_dot(m_cat, rhs))
            gs = slice(g * GROUP_WIDTH, (g + 1) * GROUP_WIDTH)
            s_enter = st_scr[:, gs]
            y_off_blocks.append(_dot(c_gb, s_enter.astype(BF16)))
            new_states = _dot(b_g.T.astype(BF16), xdecay_b[:, gs])
            st_scr[:, gs] = s_enter * exp_acum_e[L - 1:L, gs] + new_states
        y_diag = jnp.concatenate(y_diag_blocks, axis=1)
        y_off = jnp.concatenate(y_off_blocks, axis=1) * exp_acum_e

        y = (y_diag + y_off + dskip_ref[...] * xs) * z_gate
        y_ssd_chunks.append(_group_rmsnorm(y, snw_ref[...]).astype(BF16))

    scfin_ref[0] = uhist_scr[SUBLANES - (SC_CONV_W - 1):SUBLANES, :]
    out_ssd = _dot(jnp.concatenate(y_ssd_chunks, axis=0), wout_ref[0:SSD_WIDTH, :])
    o_ref[0] = x + jnp.concatenate(out_sc_chunks, axis=0) + out_ssd

    @pl.when(last)
    def _():
        sfin_ref[0] = st_scr[...].T


MIXER_CHUNKS_PER_STEP = 2
MIXER_WEIGHT_LOAD_STEPS = WEIGHT_LOAD_STEPS


def _mixer_prompt(x, nw, wint, cw, cb, dtb, alog, dskip, snw, scw, wout):
    nb, seq, d = x.shape
    tile_t = MIXER_CHUNKS_PER_STEP * CHUNK
    assert seq % tile_t == 0
    nc = seq // tile_t
    ws = MIXER_WEIGHT_LOAD_STEPS

    def chunk_map(i):
        t = jnp.maximum(i - ws, 0)
        return (t // nc, t % nc, 0)

    def seq_map(i):
        return (jnp.maximum(i - ws, 0) // nc, 0, 0)

    out_shape = (
        jax.ShapeDtypeStruct((nb, seq, d), F32),
        jax.ShapeDtypeStruct((nb, SSD_WIDTH, D_STATE), F32),
        jax.ShapeDtypeStruct((nb, SSD_CONV_W - 1, SSD_CONV_DIM), F32),
        jax.ShapeDtypeStruct((nb, SC_CONV_W - 1, SC_WIDTH), F32),
    )
    return pl.pallas_call(
        functools.partial(_mixer_kernel, nc),
        out_shape=out_shape,
        grid=(ws + nb * nc,),
        in_specs=[pl.BlockSpec((1, tile_t, d), chunk_map),
                  _resident((1, d)), _weight_rows_spec(wint, ws),
                  _resident((SSD_CONV_W, SSD_CONV_DIM)), _resident((1, SSD_CONV_DIM)),
                  _resident((1, LANES)), _resident((1, LANES)), _resident((1, SSD_WIDTH)),
                  _resident((1, SSD_WIDTH)), _resident((SC_CONV_W, SC_WIDTH)),
                  _weight_rows_spec(wout, ws)],
        out_specs=(pl.BlockSpec((1, tile_t, d), chunk_map),
                   pl.BlockSpec((1, SSD_WIDTH, D_STATE), seq_map),
                   pl.BlockSpec((1, SSD_CONV_W - 1, SSD_CONV_DIM), seq_map),
                   pl.BlockSpec((1, SC_CONV_W - 1, SC_WIDTH), seq_map)),
        scratch_shapes=[pltpu.VMEM((D_STATE, SSD_WIDTH), F32),
                        pltpu.VMEM((SUBLANES, SSD_CONV_DIM), F32),
                        pltpu.VMEM((SUBLANES, SC_WIDTH), F32),
                        _weight_scratch(wint, ws), _weight_scratch(wout, ws)],
        compiler_params=pltpu.CompilerParams(
            dimension_semantics=("arbitrary",), vmem_limit_bytes=VMEM_LIMIT_BYTES),
        name="mixer_prompt",
    )(x, nw, wint.stacked, cw, cb, dtb, alog, dskip, snw, scw, wout.stacked)


def _sample_pre_kernel(x_ref, nw_ref, wint_ref, cw_ref, cb_ref, dtb_ref, alog_ref,
                       scw_ref, cst_ref, scst_ref,
                       z_ref, xs_ref, b_ref, c_ref, xdt_blk_ref, da_ref, ysc_ref, cnew_ref, scnew_ref,
                       wint_bf):
    nb = x_ref.shape[0]
    x = x_ref[...]
    h = _rmsnorm(x, nw_ref[...]).astype(BF16)
    wint_bf[...] = wint_ref[...].astype(BF16)
    dt_raw = _dt_proj(h, wint_bf)
    xbc = _in_proj(h, wint_bf, OFF_XBC, SSD_CONV_DIM)
    z_ref[...] = _in_proj(h, wint_bf, OFF_Z, SSD_WIDTH)
    scb = _in_proj(h, wint_bf, OFF_SCB, SC_WIDTH)
    scc = _in_proj(h, wint_bf, OFF_SCC, SC_WIDTH)
    sch = _in_proj(h, wint_bf, OFF_SCH, SC_WIDTH)

    cw = cw_ref[...]
    acc = xbc * cw[SSD_CONV_W - 1:SSD_CONV_W, :]
    for k in range(SSD_CONV_W - 1):
        acc = acc + cst_ref[:, k, :] * cw[k:k + 1, :]
    for k in range(SSD_CONV_W - 2):
        cnew_ref[:, k, :] = cst_ref[:, k + 1, :]
    cnew_ref[:, SSD_CONV_W - 2, :] = xbc
    xbc_c = _silu(acc + cb_ref[...])
    xs = xbc_c[:, 0:SSD_WIDTH]
    xs_ref[...] = xs
    b_ref[...] = xbc_c[:, SSD_WIDTH:SSD_WIDTH + N_GROUPS * D_STATE]
    c_ref[...] = xbc_c[:, SSD_WIDTH + N_GROUPS * D_STATE:SSD_CONV_DIM]

    dt = _softplus(dt_raw + dtb_ref[...])
    da_ref[...] = jnp.exp(dt * (-jnp.exp(alog_ref[...])))
    xdt_t = (xs * _expand_heads(dt, nb)).T
    bb = xdt_blk_ref.shape[2]
    for j in range(nb // bb):
        xdt_blk_ref[j] = xdt_t[:, j * bb:(j + 1) * bb]

    u = scc * sch
    scw = scw_ref[...]
    v = u * scw[SC_CONV_W - 1:SC_CONV_W, :]
    for k in range(SC_CONV_W - 1):
        v = v + scst_ref[:, k, :] * scw[k:k + 1, :]
    for k in range(SC_CONV_W - 2):
        scnew_ref[:, k, :] = scst_ref[:, k + 1, :]
    scnew_ref[:, SC_CONV_W - 2, :] = u
    ysc_ref[...] = scb * v


def _sample_pre(x, nw, wint, cw, cb, dtb, alog, scw, cst, scst):
    nb, d = x.shape
    f = lambda *s: jax.ShapeDtypeStruct(s, F32)
    bb = STATE_BATCH_BLOCK
    assert nb % bb == 0
    out_shape = (f(nb, SSD_WIDTH), f(nb, SSD_WIDTH), f(nb, N_GROUPS * D_STATE), f(nb, N_GROUPS * D_STATE),
                 f(nb // bb, SSD_WIDTH, bb), f(nb, LANES), f(nb, SC_WIDTH),
                 f(nb, SSD_CONV_W - 1, SSD_CONV_DIM), f(nb, SC_CONV_W - 1, SC_WIDTH))
    return pl.pallas_call(
        _sample_pre_kernel,
        out_shape=out_shape,
        scratch_shapes=[pltpu.VMEM(wint.shape, BF16)],
        compiler_params=pltpu.CompilerParams(vmem_limit_bytes=VMEM_LIMIT_BYTES),
        name="sample_pre",
    )(x, nw, wint, cw, cb, dtb, alog, scw, cst, scst)


STATE_BATCH_BLOCK = 16


def _split2(x):
    hi = x.astype(BF16)
    return hi, (x - hi.astype(F32)).astype(BF16)


def _sample_state_kernel(da_ref, s0_ref, xdt_ref, b_ref, c_ref, snew_ref, y_ref):
    bb = STATE_BATCH_BLOCK
    blk = pl.program_id(0)
    xdt_t = xdt_ref[0]
    b_rows = b_ref[...]
    c_hi, c_lo = _split2(c_ref[...])
    row = lax.broadcasted_iota(jnp.int32, (bb, SSD_WIDTH), 0)
    y_blk = jnp.zeros((bb, SSD_WIDTH), F32)
    for i in range(bb):
        xdt_col = xdt_t[:, i:i + 1]
        y_parts = []
        for g in range(N_GROUPS):
            ns = slice(g * D_STATE, (g + 1) * D_STATE)
            b_row = b_rows[i:i + 1, ns]
            heads = []
            for hh in range(g * HEADS_PER_GROUP, (g + 1) * HEADS_PER_GROUP):
                rs = slice(hh * HEAD_DIM, (hh + 1) * HEAD_DIM)
                decay = da_ref[blk * bb + i, hh]
                heads.append(s0_ref[i, rs, :] * decay + xdt_col[rs] * b_row)
            s_new = jnp.concatenate(heads, axis=0)
            snew_ref[i, g * GROUP_WIDTH:(g + 1) * GROUP_WIDTH, :] = s_new
            s_hi, s_lo = _split2(s_new)
            lhs = jnp.concatenate([c_hi[:, ns], c_lo[:, ns]], axis=0)
            r_hi = _dot_nt(lhs, s_hi)
            r_lo = _dot_nt(c_hi[:, ns], s_lo)
            y_parts.append(r_hi[i:i + 1] + r_hi[bb + i:bb + i + 1] + r_lo[i:i + 1])
        y_row = jnp.concatenate(y_parts, axis=1)
        y_blk = jnp.where(row == i, y_row, y_blk)
    y_ref[...] = y_blk


def _sample_state(s0, xdt_blocks, decay, b_rows, c_rows):
    nb = s0.shape[0]
    nblk, _, bb = xdt_blocks.shape
    assert bb == STATE_BATCH_BLOCK and nblk * bb == nb
    return pl.pallas_call(
        _sample_state_kernel,
        out_shape=(jax.ShapeDtypeStruct((nb, SSD_WIDTH, D_STATE), F32),
                   jax.ShapeDtypeStruct((nb, SSD_WIDTH), F32)),
        grid=(nblk,),
        in_specs=[pl.BlockSpec(memory_space=pltpu.SMEM),
                  pl.BlockSpec((bb, SSD_WIDTH, D_STATE), lambda i: (i, 0, 0)),
                  pl.BlockSpec((1, SSD_WIDTH, bb), lambda i: (i, 0, 0)),
                  pl.BlockSpec((bb, N_GROUPS * D_STATE), lambda i: (i, 0)),
                  pl.BlockSpec((bb, N_GROUPS * D_STATE), lambda i: (i, 0))],
        out_specs=(pl.BlockSpec((bb, SSD_WIDTH, D_STATE), lambda i: (i, 0, 0)),
                   pl.BlockSpec((bb, SSD_WIDTH), lambda i: (i, 0))),
        compiler_params=pltpu.CompilerParams(
            dimension_semantics=("arbitrary",), vmem_limit_bytes=VMEM_LIMIT_BYTES),
        name="sample_state",
    )(decay, s0, xdt_blocks, b_rows, c_rows)


def _sample_post_kernel(x_ref, yraw_ref, xs_ref, z_ref, ysc_ref, dskip_ref, snw_ref, wout_ref, o_ref):
    y = yraw_ref[...] + dskip_ref[...] * xs_ref[...]
    y = y * _silu(z_ref[...])
    y_ssd = _group_rmsnorm(y, snw_ref[...])
    mixed = jnp.concatenate([y_ssd, ysc_ref[...]], axis=1).astype(BF16)
    o_ref[...] = x_ref[...] + _dot(mixed, wout_ref[...].astype(BF16))


def _sample_post(x, yraw, xs, z, ysc, dskip, snw, wout):
    return pl.pallas_call(
        _sample_post_kernel,
        out_shape=jax.ShapeDtypeStruct(x.shape, F32),
        compiler_params=pltpu.CompilerParams(vmem_limit_bytes=VMEM_LIMIT_BYTES),
        name="sample_post",
    )(x, yraw, xs, z, ysc, dskip, snw, wout)


FFN_TILE_M = 1024
FFN_SUB_TILE_M = 256


def _layer_params(i, norm_ffn1_w, ffn1_w_gate, ffn1_w_up, ffn1_w_down, norm_mix_w, w_in_t,
                  ssd_conv_w, ssd_conv_b, dt_bias, a_log, d_skip, ssd_norm_w, sconv_w, w_out,
                  norm_ffn2_w, ffn2_w_gate, ffn2_w_up, ffn2_w_down):
    pad_heads = lambda v: jnp.pad(v, (0, LANES - N_HEADS)).reshape(1, LANES)
    row = lambda v: v.reshape(1, -1)
    return dict(
        ffn1=(row(norm_ffn1_w[i]), _LayerWeight(ffn1_w_gate, i), _LayerWeight(ffn1_w_up, i),
              _LayerWeight(ffn1_w_down, i)),
        ffn2=(row(norm_ffn2_w[i]), _LayerWeight(ffn2_w_gate, i), _LayerWeight(ffn2_w_up, i),
              _LayerWeight(ffn2_w_down, i)),
        nw=row(norm_mix_w[i]), wint=_LayerWeight(w_in_t, i), cw=ssd_conv_w[i], cb=row(ssd_conv_b[i]),
        dtb=pad_heads(dt_bias[i]), alog=pad_heads(a_log[i]),
        dskip=row(jnp.repeat(d_skip[i], HEAD_DIM)), snw=row(ssd_norm_w[i]), scw=sconv_w[i],
        wout=_LayerWeight(w_out, i))


def kernel(x_prompt, x_sample, state_ssm, state_ssd_conv, state_sconv, norm_ffn1_w, ffn1_w_gate, ffn1_w_up, ffn1_w_down, norm_mix_w, w_in, ssd_conv_w, ssd_conv_b, dt_bias, a_log, d_skip, ssd_norm_w, sconv_w, w_out, norm_ffn2_w, ffn2_w_gate, ffn2_w_up, ffn2_w_down, final_norm_w):
    depth = w_in.shape[0]
    bp, seq, d = x_prompt.shape
    bs, dec_seq, _ = x_sample.shape
    assert dec_seq == 1, "sample group is one token per sequence"
    fnw = final_norm_w.reshape(1, d)
    w_in_t = jnp.swapaxes(w_in, 1, 2)

    xp = x_prompt.reshape(bp * seq, d)
    xs = x_sample.reshape(bs, d)
    outs = [[] for _ in range(6)]
    for i in range(depth):
        p = _layer_params(i, norm_ffn1_w, ffn1_w_gate, ffn1_w_up, ffn1_w_down, norm_mix_w, w_in_t,
                          ssd_conv_w, ssd_conv_b, dt_bias, a_log, d_skip, ssd_norm_w, sconv_w,
                          w_out, norm_ffn2_w, ffn2_w_gate, ffn2_w_up, ffn2_w_down)
        last = i == depth - 1
        xp, xs = _ffn(xp, xs, *p["ffn1"], tile_m=FFN_TILE_M)
        xp3, s_p, c_p, sc_p = _mixer_prompt(
            xp.reshape(bp, seq, d), p["nw"], p["wint"], p["cw"], p["cb"], p["dtb"], p["alog"],
            p["dskip"], p["snw"], p["scw"], p["wout"])
        z, xs_conv, b_rows, c_rows, xdt_t, decay, ysc, c_s, sc_s = _sample_pre(
            xs, p["nw"], w_in_t[i], p["cw"], p["cb"], p["dtb"], p["alog"], p["scw"],
            state_ssd_conv[i], state_sconv[i])
        s_s, yraw = _sample_state(state_ssm[i].reshape(bs, SSD_WIDTH, D_STATE), xdt_t, decay, b_rows, c_rows)
        xs = _sample_post(xs, yraw, xs_conv, z, ysc, p["dskip"], p["snw"], w_out[i])
        xp, xs = _ffn(xp3.reshape(bp * seq, d), xs, *p["ffn2"], fnw if last else None, tile_m=FFN_TILE_M)
        for lst, v in zip(outs, (s_p.reshape(bp, N_HEADS, HEAD_DIM, D_STATE), c_p, sc_p,
                                 s_s.reshape(bs, N_HEADS, HEAD_DIM, D_STATE),
                                 c_s, sc_s)):
            lst.append(v)
    return (xp.reshape(bp, seq, d), xs.reshape(bs, dec_seq, d)) + tuple(jnp.stack(l) for l in outs)
```

```python
import functools

import jax
import jax.numpy as jnp
from jax import lax
from jax.experimental import pallas as pl
from jax.experimental.pallas import tpu as pltpu

F32 = jnp.float32
BF16 = jnp.bfloat16

D_MODEL = 1024
SSD_WIDTH = 1024
SC_WIDTH = 1024
HEAD_DIM = 64
N_HEADS = SSD_WIDTH // HEAD_DIM
N_GROUPS = 2
HEADS_PER_GROUP = N_HEADS // N_GROUPS
GROUP_WIDTH = SSD_WIDTH // N_GROUPS
D_STATE = 128
SSD_CONV_W = 4
SSD_CONV_DIM = SSD_WIDTH + 2 * N_GROUPS * D_STATE
SC_CONV_W = 3
CHUNK = 256
NORM_EPS = 1e-6

LANES = 128
SUBLANES = 8

OFF_Z = 0
OFF_XBC = OFF_Z + SSD_WIDTH
OFF_DT = OFF_XBC + SSD_CONV_DIM
OFF_SCB = OFF_DT + N_HEADS
OFF_SCC = OFF_SCB + SC_WIDTH
OFF_SCH = OFF_SCC + SC_WIDTH
D_IN_PROJ = OFF_SCH + SC_WIDTH
PROJ_BLOCK = 512

VMEM_LIMIT_BYTES = 56 * 1024 * 1024


def _rmsnorm(x, w):
    ms = jnp.mean(x * x, axis=-1, keepdims=True)
    return x * lax.rsqrt(ms + NORM_EPS) * w


def _silu(x):
    half = 0.5 * x
    return half * jnp.tanh(half) + half


def _silu_mixer(x):
    return x * (0.5 * jnp.tanh(0.5 * x) + 0.5)


def _softplus(x):
    return jnp.maximum(x, 0.0) + jnp.log1p(jnp.exp(-jnp.abs(x)))


def _dot(a, b):
    return jnp.dot(a, b, preferred_element_type=F32)


def _dot_nt(a, b):
    return lax.dot_general(a, b, (((1,), (1,)), ((), ())), preferred_element_type=F32)


def _split3(x):
    hi = x.astype(BF16)
    r1 = x - hi.astype(F32)
    mid = r1.astype(BF16)
    lo = (r1 - mid.astype(F32)).astype(BF16)
    return hi, mid, lo


def _expand_heads(v, rows):
    lane = lax.broadcasted_iota(jnp.int32, (rows, LANES), 1)
    first_half = lane < HEAD_DIM
    blocks = []
    for q in range(N_HEADS // 2):
        c0 = jnp.broadcast_to(v[:, 2 * q:2 * q + 1], (rows, LANES))
        c1 = jnp.broadcast_to(v[:, 2 * q + 1:2 * q + 2], (rows, LANES))
        blocks.append(jnp.where(first_half, c0, c1))
    return jnp.concatenate(blocks, axis=1)


def _group_rmsnorm(y, w):
    outs = []
    for g in range(N_GROUPS):
        sl = slice(g * GROUP_WIDTH, (g + 1) * GROUP_WIDTH)
        outs.append(_rmsnorm(y[:, sl], w[:, sl]))
    return jnp.concatenate(outs, axis=1)


def _shift_rows(x, prev_tile, j):
    rolled = pltpu.roll(x, j, 0)
    row = lax.broadcasted_iota(jnp.int32, prev_tile.shape, 0)
    head = jnp.where(row < j, pltpu.roll(prev_tile, j, 0), rolled[0:SUBLANES])
    return jnp.concatenate([head, rolled[SUBLANES:]], axis=0)


def _causal_conv(u, prev_tile, w):
    k = w.shape[0]
    acc = u * w[k - 1:k, :]
    for j in range(1, k):
        acc = acc + _shift_rows(u, prev_tile, j) * w[k - 1 - j:k - j, :]
    return acc


WEIGHT_LOAD_STEPS = 8


def _load_weight_rows(step, w_ref, w_bf_ref):
    rows = w_ref.shape[0]
    r0 = pl.multiple_of(step * rows, rows)
    w_bf_ref[pl.ds(r0, rows), :] = w_ref[...].astype(BF16)


def _weight_block_rows(rows, steps):
    tile = 2 * SUBLANES
    return -(-rows // (steps * tile)) * tile


def _weight_rows_spec(w, steps):
    _, rows, cols = w.stacked.shape
    return pl.BlockSpec((None, _weight_block_rows(rows, steps), cols),
                        lambda i: (w.layer, jnp.minimum(i, steps - 1), 0))


def _weight_scratch(w, steps):
    rows, cols = w.shape
    return pltpu.VMEM((steps * _weight_block_rows(rows, steps), cols), BF16)


class _LayerWeight:
    def __init__(self, stacked, layer):
        self.stacked, self.layer = stacked, layer
        self.shape = stacked.shape[1:]


def _ffn_kernel(*refs, final_norm):
    if final_norm:
        (xp_ref, xs_ref, nw_ref, wg_ref, wu_ref, wd_ref, fnw_ref, op_ref, os_ref,
         wg_bf, wu_bf, wd_bf) = refs
    else:
        xp_ref, xs_ref, nw_ref, wg_ref, wu_ref, wd_ref, op_ref, os_ref, wg_bf, wu_bf, wd_bf = refs

    def half_step(x_ref, o_ref):
        rows = x_ref.shape[0]
        sub = min(rows, FFN_SUB_TILE_M)
        for r0 in range(0, rows, sub):
            x = x_ref[r0:r0 + sub, :]
            xn = _rmsnorm(x, nw_ref[...]).astype(BF16)
            g = _dot(xn, wg_bf[...])
            u = _dot(xn, wu_bf[...])
            hmid = (_silu(g) * u).astype(BF16)
            y = x + 0.5 * _dot(hmid, wd_bf[...])
            if final_norm:
                y = _rmsnorm(y, fnw_ref[...])
            o_ref[r0:r0 + sub, :] = y

    i = pl.program_id(0)
    sample_step = pl.num_programs(0) - 1

    @pl.when(i < WEIGHT_LOAD_STEPS)
    def _():
        for w_ref, w_bf in ((wg_ref, wg_bf), (wu_ref, wu_bf), (wd_ref, wd_bf)):
            _load_weight_rows(i, w_ref, w_bf)

    @pl.when((i >= WEIGHT_LOAD_STEPS) & (i < sample_step))
    def _():
        half_step(xp_ref, op_ref)

    @pl.when(i == sample_step)
    def _():
        half_step(xs_ref, os_ref)


def _resident(shape):
    return pl.BlockSpec(shape, lambda *_: (0,) * len(shape), pipeline_mode=pl.Buffered(1))


def _ffn(xp, xs, nw, wg, wu, wd, fnw=None, *, tile_m):
    m, d = xp.shape
    ms = xs.shape[0]
    dff = wg.shape[1]
    assert m % tile_m == 0
    n = m // tile_m
    final_norm = fnw is not None
    ws = WEIGHT_LOAD_STEPS
    prompt_tile = pl.BlockSpec((tile_m, d), lambda i: (jnp.clip(i - ws, 0, n - 1), 0))
    sample_tile = pl.BlockSpec((ms, d), lambda i: (0, 0))
    in_specs = [prompt_tile, sample_tile, _resident((1, d)),
                _weight_rows_spec(wg, ws), _weight_rows_spec(wu, ws), _weight_rows_spec(wd, ws)]
    args = [xp, xs, nw, wg.stacked, wu.stacked, wd.stacked]
    if final_norm:
        in_specs.append(_resident((1, d)))
        args.append(fnw)
    return pl.pallas_call(
        functools.partial(_ffn_kernel, final_norm=final_norm),
        out_shape=(jax.ShapeDtypeStruct((m, d), F32), jax.ShapeDtypeStruct((ms, d), F32)),
        grid=(ws + n + 1,),
        in_specs=in_specs,
        out_specs=(prompt_tile, sample_tile),
        scratch_shapes=[_weight_scratch(wg, ws), _weight_scratch(wu, ws), _weight_scratch(wd, ws)],
        compiler_params=pltpu.CompilerParams(
            dimension_semantics=("arbitrary",), vmem_limit_bytes=VMEM_LIMIT_BYTES),
        name="ffn_final" if final_norm else "ffn",
    )(*args)


def _in_proj(h, wint_ref, col0, width):
    return _dot_nt(h, wint_ref[col0:col0 + width, :])


def _dt_proj(h, wint_ref):
    raw = _in_proj(h, wint_ref, OFF_DT, LANES)
    lane = lax.broadcasted_iota(jnp.int32, raw.shape, 1)
    return jnp.where(lane < N_HEADS, raw, 0.0)


def _mixer_kernel(tiles_per_seq, x_ref, nw_ref, wint_ref, cw_ref, cb_ref, dtb_ref, alog_ref,
                  dskip_ref, snw_ref, scw_ref, wout_ref,
                  o_ref, sfin_ref, cfin_ref, scfin_ref,
                  st_scr, hist_scr, uhist_scr, wint_bf, wout_bf):
    i = pl.program_id(0)

    @pl.when(i < MIXER_WEIGHT_LOAD_STEPS)
    def _():
        _load_weight_rows(i, wint_ref, wint_bf)
        _load_weight_rows(i, wout_ref, wout_bf)

    @pl.when(i >= MIXER_WEIGHT_LOAD_STEPS)
    def _():
        c = lax.rem(i - MIXER_WEIGHT_LOAD_STEPS, tiles_per_seq)
        _mixer_chunk(c == 0, c == tiles_per_seq - 1, x_ref, nw_ref, wint_bf, cw_ref,
                     cb_ref, dtb_ref, alog_ref, dskip_ref, snw_ref, scw_ref, wout_bf,
                     o_ref, sfin_ref, cfin_ref, scfin_ref, st_scr, hist_scr, uhist_scr)


def _mixer_chunk(first, last, x_ref, nw_ref, wint_ref, cw_ref, cb_ref, dtb_ref, alog_ref,
                 dskip_ref, snw_ref, scw_ref, wout_ref,
                 o_ref, sfin_ref, cfin_ref, scfin_ref,
                 st_scr, hist_scr, uhist_scr):
    L = CHUNK
    T = x_ref.shape[1]

    @pl.when(first)
    def _():
        st_scr[...] = jnp.zeros_like(st_scr)
        hist_scr[...] = jnp.zeros_like(hist_scr)
        uhist_scr[...] = jnp.zeros_like(uhist_scr)

    x = x_ref[0]
    h = _rmsnorm(x, nw_ref[...]).astype(BF16)
    dt_raw = _dt_proj(h, wint_ref)

    xbc_c_blocks = []
    for b0 in range(0, SSD_CONV_DIM, PROJ_BLOCK):
        cols = slice(b0, b0 + PROJ_BLOCK)
        xbc = _in_proj(h, wint_ref, OFF_XBC + b0, PROJ_BLOCK)
        acc = _causal_conv(xbc, hist_scr[:, cols], cw_ref[:, cols])
        hist_scr[:, cols] = xbc[T - SUBLANES:T, :]
        xbc_c_blocks.append(_silu_mixer(acc + cb_ref[:, cols]))
    cfin_ref[0] = hist_scr[SUBLANES - (SSD_CONV_W - 1):SUBLANES, :]
    xbc_c_all = jnp.concatenate(xbc_c_blocks, axis=1)

    def plus_zero_of(v, gate):
        bits = pltpu.bitcast(gate[0:SUBLANES, 0:LANES], jnp.uint32)
        half_word = jnp.uint32(16)
        zero_row = pltpu.bitcast(
            lax.shift_right_logical(lax.shift_right_logical(bits, half_word), half_word), F32)[0:1, :]
        return v + jnp.concatenate([zero_row] * (D_MODEL // LANES), axis=1).astype(BF16)

    dt_all = _softplus(dt_raw + dtb_ref[...])
    a_all = dt_all * (-jnp.exp(alog_ref[...]))
    row = lax.broadcasted_iota(jnp.int32, (L, L), 0)
    col = lax.broadcasted_iota(jnp.int32, (L, L), 1)
    causal = row >= col
    tril = jnp.where(causal, 1.0, 0.0).astype(BF16)
    lane = lax.broadcasted_iota(jnp.int32, (L, LANES), 1)
    first_half = lane < HEAD_DIM

    y_ssd_chunks = []
    out_sc_chunks = []
    for k in range(T // L):
        rows = slice(k * L, (k + 1) * L)
        xbc_c = xbc_c_all[rows]
        xs = xbc_c[:, 0:SSD_WIDTH]
        b_all = xbc_c[:, SSD_WIDTH:SSD_WIDTH + N_GROUPS * D_STATE]
        c_all = xbc_c[:, SSD_WIDTH + N_GROUPS * D_STATE:SSD_CONV_DIM]
        dt = dt_all[rows]
        a_hi, a_mid, a_lo = _split3(a_all[rows])
        acum = _dot(tril, a_hi) + _dot(tril, a_mid) + _dot(tril, a_lo)
        acum_t = acum.T

        def h_after(gate):
            return plus_zero_of(h[rows], gate)

        def z_gate_after(gate):
            h_late = h_after(gate)
            return jnp.concatenate(
                [_silu_mixer(_in_proj(h_late, wint_ref, OFF_Z + b0, PROJ_BLOCK))
                 for b0 in range(0, SSD_WIDTH, PROJ_BLOCK)], axis=1)

        def short_conv_block_after(gate, b0):
            h_late = h_after(gate)
            cols = slice(b0, b0 + PROJ_BLOCK)
            scc = _in_proj(h_late, wint_ref, OFF_SCC + b0, PROJ_BLOCK)
            sch = _in_proj(h_late, wint_ref, OFF_SCH + b0, PROJ_BLOCK)
            u = scc * sch
            v = _causal_conv(u, uhist_scr[:, cols], scw_ref[:, cols])
            uhist_scr[:, cols] = u[L - SUBLANES:L, :]
            scb = _in_proj(h_late, wint_ref, OFF_SCB + b0, PROJ_BLOCK)
            return (scb * v).astype(BF16)

        z_gate = z_gate_after(acum)
        y_sc = jnp.concatenate(
            [short_conv_block_after(acum, b0) for b0 in range(0, SC_WIDTH, PROJ_BLOCK)], axis=1)
        out_sc_chunks.append(_dot(y_sc, wout_ref[SSD_WIDTH:SSD_WIDTH + SC_WIDTH, :]))

        acum_e = _expand_heads(acum, L)
        dt_e = _expand_heads(dt, L)
        exp_acum_e = jnp.exp(acum_e)
        decay_end_e = jnp.exp(acum_e[L - 1:L, :] - acum_e)
        xdt = xs * dt_e
        xdt_b = xdt.astype(BF16)
        xdecay_b = (xdt * decay_end_e).astype(BF16)

        y_diag_blocks = []
        y_off_blocks = []
        for g in range(N_GROUPS):
            b_g = b_all[:, g * D_STATE:(g + 1) * D_STATE]
            c_g = c_all[:, g * D_STATE:(g + 1) * D_STATE]
            b_gb = b_g.astype(BF16)
            c_gb = c_g.astype(BF16)
            cb = _dot_nt(c_gb, b_gb)
            for q in range(g * HEADS_PER_GROUP // 2, (g + 1) * HEADS_PER_GROUP // 2):
                ms = []
                for hh in (2 * q, 2 * q + 1):
                    seg = acum[:, hh:hh + 1] - acum_t[hh:hh + 1, :]
                    decay = jnp.exp(jnp.where(causal, seg, -jnp.inf))
                    ms.append((cb * decay).astype(BF16))
                m_cat = jnp.concatenate(ms, axis=1)
                x2 = xdt_b[:, q * LANES:(q + 1) * LANES]
                zero = jnp.zeros_like(x2)
                rhs = jnp.concatenate([jnp.where(first_half, x2, zero),
                                       jnp.where(first_half, zero, x2)], axis=0)
                y_diag_blocks.append(_dot(m_cat, rhs))
            gs = slice(g * GROUP_WIDTH, (g + 1) * GROUP_WIDTH)
            s_enter = st_scr[:, gs]
            y_off_blocks.append(_dot(c_gb, s_enter.astype(BF16)))
            new_states = _dot(b_g.T.astype(BF16), xdecay_b[:, gs])
            st_scr[:, gs] = s_enter * exp_acum_e[L - 1:L, gs] + new_states
        y_diag = jnp.concatenate(y_diag_blocks, axis=1)
        y_off = jnp.concatenate(y_off_blocks, axis=1) * exp_acum_e

        y = (y_diag + y_off + dskip_ref[...] * xs) * z_gate
        y_ssd_chunks.append(_group_rmsnorm(y, snw_ref[...]).astype(BF16))

    scfin_ref[0] = uhist_scr[SUBLANES - (SC_CONV_W - 1):SUBLANES, :]
    out_ssd = _dot(jnp.concatenate(y_ssd_chunks, axis=0), wout_ref[0:SSD_WIDTH, :])
    o_ref[0] = x + jnp.concatenate(out_sc_chunks, axis=0) + out_ssd

    @pl.when(last)
    def _():
        sfin_ref[0] = st_scr[...].T


MIXER_CHUNKS_PER_STEP = 2
MIXER_WEIGHT_LOAD_STEPS = WEIGHT_LOAD_STEPS


def _mixer_prompt(x, nw, wint, cw, cb, dtb, alog, dskip, snw, scw, wout):
    nb, seq, d = x.shape
    tile_t = MIXER_CHUNKS_PER_STEP * CHUNK
    assert seq % tile_t == 0
    nc = seq // tile_t
    ws = MIXER_WEIGHT_LOAD_STEPS

    def chunk_map(i):
        t = jnp.maximum(i - ws, 0)
        return (t // nc, t % nc, 0)

    def seq_map(i):
        return (jnp.maximum(i - ws, 0) // nc, 0, 0)

    out_shape = (
        jax.ShapeDtypeStruct((nb, seq, d), F32),
        jax.ShapeDtypeStruct((nb, SSD_WIDTH, D_STATE), F32),
        jax.ShapeDtypeStruct((nb, SSD_CONV_W - 1, SSD_CONV_DIM), F32),
        jax.ShapeDtypeStruct((nb, SC_CONV_W - 1, SC_WIDTH), F32),
    )
    return pl.pallas_call(
        functools.partial(_mixer_kernel, nc),
        out_shape=out_shape,
        grid=(ws + nb * nc,),
        in_specs=[pl.BlockSpec((1, tile_t, d), chunk_map),
                  _resident((1, d)), _weight_rows_spec(wint, ws),
                  _resident((SSD_CONV_W, SSD_CONV_DIM)), _resident((1, SSD_CONV_DIM)),
                  _resident((1, LANES)), _resident((1, LANES)), _resident((1, SSD_WIDTH)),
                  _resident((1, SSD_WIDTH)), _resident((SC_CONV_W, SC_WIDTH)),
                  _weight_rows_spec(wout, ws)],
        out_specs=(pl.BlockSpec((1, tile_t, d), chunk_map),
                   pl.BlockSpec((1, SSD_WIDTH, D_STATE), seq_map),
                   pl.BlockSpec((1, SSD_CONV_W - 1, SSD_CONV_DIM), seq_map),
                   pl.BlockSpec((1, SC_CONV_W - 1, SC_WIDTH), seq_map)),
        scratch_shapes=[pltpu.VMEM((D_STATE, SSD_WIDTH), F32),
                        pltpu.VMEM((SUBLANES, SSD_CONV_DIM), F32),
                        pltpu.VMEM((SUBLANES, SC_WIDTH), F32),
                        _weight_scratch(wint, ws), _weight_scratch(wout, ws)],
        compiler_params=pltpu.CompilerParams(
            dimension_semantics=("arbitrary",), vmem_limit_bytes=VMEM_LIMIT_BYTES),
        name="mixer_prompt",
    )(x, nw, wint.stacked, cw, cb, dtb, alog, dskip, snw, scw, wout.stacked)


def _sample_pre_kernel(x_ref, nw_ref, wint_ref, cw_ref, cb_ref, dtb_ref, alog_ref,
                       scw_ref, cst_ref, scst_ref,
                       z_ref, xs_ref, b_ref, c_ref, xdt_blk_ref, da_ref, ysc_ref, cnew_ref, scnew_ref,
                       wint_bf):
    nb = x_ref.shape[0]
    x = x_ref[...]
    h = _rmsnorm(x, nw_ref[...]).astype(BF16)
    wint_bf[...] = wint_ref[...].astype(BF16)
    dt_raw = _dt_proj(h, wint_bf)
    xbc = _in_proj(h, wint_bf, OFF_XBC, SSD_CONV_DIM)
    z_ref[...] = _in_proj(h, wint_bf, OFF_Z, SSD_WIDTH)
    scb = _in_proj(h, wint_bf, OFF_SCB, SC_WIDTH)
    scc = _in_proj(h, wint_bf, OFF_SCC, SC_WIDTH)
    sch = _in_proj(h, wint_bf, OFF_SCH, SC_WIDTH)

    cw = cw_ref[...]
    acc = xbc * cw[SSD_CONV_W - 1:SSD_CONV_W, :]
    for k in range(SSD_CONV_W - 1):
        acc = acc + cst_ref[k] * cw[k:k + 1, :]
    for k in range(SSD_CONV_W - 2):
        cnew_ref[k] = cst_ref[k + 1]
    cnew_ref[SSD_CONV_W - 2] = xbc
    xbc_c = _silu(acc + cb_ref[...])
    xs = xbc_c[:, 0:SSD_WIDTH]
    xs_ref[...] = xs
    b_ref[...] = xbc_c[:, SSD_WIDTH:SSD_WIDTH + N_GROUPS * D_STATE]
    c_ref[...] = xbc_c[:, SSD_WIDTH + N_GROUPS * D_STATE:SSD_CONV_DIM]

    dt = _softplus(dt_raw + dtb_ref[...])
    da_ref[...] = jnp.exp(dt * (-jnp.exp(alog_ref[...])))
    xdt_t = (xs * _expand_heads(dt, nb)).T
    bb = xdt_blk_ref.shape[2]
    for j in range(nb // bb):
        xdt_blk_ref[j] = xdt_t[:, j * bb:(j + 1) * bb]

    u = scc * sch
    scw = scw_ref[...]
    v = u * scw[SC_CONV_W - 1:SC_CONV_W, :]
    for k in range(SC_CONV_W - 1):
        v = v + scst_ref[:, k, :] * scw[k:k + 1, :]
    for k in range(SC_CONV_W - 2):
        scnew_ref[:, k, :] = scst_ref[:, k + 1, :]
    scnew_ref[:, SC_CONV_W - 2, :] = u
    ysc_ref[...] = scb * v


def _sample_pre(x, nw, wint, cw, cb, dtb, alog, scw, cst, scst):
    nb, d = x.shape
    f = lambda *s: jax.ShapeDtypeStruct(s, F32)
    bb = STATE_BATCH_BLOCK
    assert nb % bb == 0
    out_shape = (f(nb, SSD_WIDTH), f(nb, SSD_WIDTH), f(nb, N_GROUPS * D_STATE), f(nb, N_GROUPS * D_STATE),
                 f(nb // bb, SSD_WIDTH, bb), f(nb, LANES), f(nb, SC_WIDTH),
                 f(SSD_CONV_W - 1, nb, SSD_CONV_DIM), f(nb, SC_CONV_W - 1, SC_WIDTH))
    return pl.pallas_call(
        _sample_pre_kernel,
        out_shape=out_shape,
        scratch_shapes=[pltpu.VMEM(wint.shape, BF16)],
        compiler_params=pltpu.CompilerParams(vmem_limit_bytes=VMEM_LIMIT_BYTES),
        name="sample_pre",
    )(x, nw, wint, cw, cb, dtb, alog, scw, cst, scst)


STATE_BATCH_BLOCK = 16


def _split2(x):
    hi = x.astype(BF16)
    return hi, (x - hi.astype(F32)).astype(BF16)


def _sample_state_kernel(da_ref, s0_ref, xdt_ref, b_ref, c_ref, snew_ref, y_ref):
    bb = STATE_BATCH_BLOCK
    blk = pl.program_id(0)
    xdt_t = xdt_ref[0]
    b_rows = b_ref[...]
    c_hi, c_lo = _split2(c_ref[...])
    row = lax.broadcasted_iota(jnp.int32, (bb, SSD_WIDTH), 0)
    y_blk = jnp.zeros((bb, SSD_WIDTH), F32)
    for i in range(bb):
        xdt_col = xdt_t[:, i:i + 1]
        y_parts = []
        for g in range(N_GROUPS):
            ns = slice(g * D_STATE, (g + 1) * D_STATE)
            b_row = b_rows[i:i + 1, ns]
            heads = []
            for hh in range(g * HEADS_PER_GROUP, (g + 1) * HEADS_PER_GROUP):
                rs = slice(hh * HEAD_DIM, (hh + 1) * HEAD_DIM)
                decay = da_ref[blk * bb + i, hh]
                heads.append(s0_ref[i, rs, :] * decay + xdt_col[rs] * b_row)
            s_new = jnp.concatenate(heads, axis=0)
            snew_ref[i, g * GROUP_WIDTH:(g + 1) * GROUP_WIDTH, :] = s_new
            s_hi, s_lo = _split2(s_new)
            lhs = jnp.concatenate([c_hi[:, ns], c_lo[:, ns]], axis=0)
            r_hi = _dot_nt(lhs, s_hi)
            r_lo = _dot_nt(c_hi[:, ns], s_lo)
            y_parts.append(r_hi[i:i + 1] + r_hi[bb + i:bb + i + 1] + r_lo[i:i + 1])
        y_row = jnp.concatenate(y_parts, axis=1)
        y_blk = jnp.where(row == i, y_row, y_blk)
    y_ref[...] = y_blk


def _sample_state(s0, xdt_blocks, decay, b_rows, c_rows):
    nb = s0.shape[0]
    nblk, _, bb = xdt_blocks.shape
    assert bb == STATE_BATCH_BLOCK and nblk * bb == nb
    return pl.pallas_call(
        _sample_state_kernel,
        out_shape=(jax.ShapeDtypeStruct((nb, SSD_WIDTH, D_STATE), F32),
                   jax.ShapeDtypeStruct((nb, SSD_WIDTH), F32)),
        grid=(nblk,),
        in_specs=[pl.BlockSpec(memory_space=pltpu.SMEM),
                  pl.BlockSpec((bb, SSD_WIDTH, D_STATE), lambda i: (i, 0, 0)),
                  pl.BlockSpec((1, SSD_WIDTH, bb), lambda i: (i, 0, 0)),
                  pl.BlockSpec((bb, N_GROUPS * D_STATE), lambda i: (i, 0)),
                  pl.BlockSpec((bb, N_GROUPS * D_STATE), lambda i: (i, 0))],
        out_specs=(pl.BlockSpec((bb, SSD_WIDTH, D_STATE), lambda i: (i, 0, 0)),
                   pl.BlockSpec((bb, SSD_WIDTH), lambda i: (i, 0))),
        compiler_params=pltpu.CompilerParams(
            dimension_semantics=("arbitrary",), vmem_limit_bytes=VMEM_LIMIT_BYTES),
        name="sample_state",
    )(decay, s0, xdt_blocks, b_rows, c_rows)


def _sample_post_kernel(x_ref, yraw_ref, xs_ref, z_ref, ysc_ref, dskip_ref, snw_ref, wout_ref, o_ref):
    y = yraw_ref[...] + dskip_ref[...] * xs_ref[...]
    y = y * _silu(z_ref[...])
    y_ssd = _group_rmsnorm(y, snw_ref[...])
    mixed = jnp.concatenate([y_ssd, ysc_ref[...]], axis=1).astype(BF16)
    o_ref[...] = x_ref[...] + _dot(mixed, wout_ref[...].astype(BF16))


def _sample_post(x, yraw, xs, z, ysc, dskip, snw, wout):
    return pl.pallas_call(
        _sample_post_kernel,
        out_shape=jax.ShapeDtypeStruct(x.shape, F32),
        compiler_params=pltpu.CompilerParams(vmem_limit_bytes=VMEM_LIMIT_BYTES),
        name="sample_post",
    )(x, yraw, xs, z, ysc, dskip, snw, wout)


FFN_TILE_M = 1024
FFN_SUB_TILE_M = 256


def _layer_params(i, norm_ffn1_w, ffn1_w_gate, ffn1_w_up, ffn1_w_down, norm_mix_w, w_in_t,
                  ssd_conv_w, ssd_conv_b, dt_bias, a_log, d_skip, ssd_norm_w, sconv_w, w_out,
                  norm_ffn2_w, ffn2_w_gate, ffn2_w_up, ffn2_w_down):
    pad_heads = lambda v: jnp.pad(v, (0, LANES - N_HEADS)).reshape(1, LANES)
    row = lambda v: v.reshape(1, -1)
    return dict(
        ffn1=(row(norm_ffn1_w[i]), _LayerWeight(ffn1_w_gate, i), _LayerWeight(ffn1_w_up, i),
              _LayerWeight(ffn1_w_down, i)),
        ffn2=(row(norm_ffn2_w[i]), _LayerWeight(ffn2_w_gate, i), _LayerWeight(ffn2_w_up, i),
              _LayerWeight(ffn2_w_down, i)),
        nw=row(norm_mix_w[i]), wint=_LayerWeight(w_in_t, i), cw=ssd_conv_w[i], cb=row(ssd_conv_b[i]),
        dtb=pad_heads(dt_bias[i]), alog=pad_heads(a_log[i]),
        dskip=row(jnp.repeat(d_skip[i], HEAD_DIM)), snw=row(ssd_norm_w[i]), scw=sconv_w[i],
        wout=_LayerWeight(w_out, i))


def kernel(x_prompt, x_sample, state_ssm, state_ssd_conv, state_sconv, norm_ffn1_w, ffn1_w_gate, ffn1_w_up, ffn1_w_down, norm_mix_w, w_in, ssd_conv_w, ssd_conv_b, dt_bias, a_log, d_skip, ssd_norm_w, sconv_w, w_out, norm_ffn2_w, ffn2_w_gate, ffn2_w_up, ffn2_w_down, final_norm_w):
    depth = w_in.shape[0]
    bp, seq, d = x_prompt.shape
    bs, dec_seq, _ = x_sample.shape
    assert dec_seq == 1, "sample group is one token per sequence"
    fnw = final_norm_w.reshape(1, d)
    w_in_t = jnp.swapaxes(w_in, 1, 2)

    xp = x_prompt.reshape(bp * seq, d)
    xs = x_sample.reshape(bs, d)
    outs = [[] for _ in range(6)]
    for i in range(depth):
        p = _layer_params(i, norm_ffn1_w, ffn1_w_gate, ffn1_w_up, ffn1_w_down, norm_mix_w, w_in_t,
                          ssd_conv_w, ssd_conv_b, dt_bias, a_log, d_skip, ssd_norm_w, sconv_w,
                          w_out, norm_ffn2_w, ffn2_w_gate, ffn2_w_up, ffn2_w_down)
        last = i == depth - 1
        xp, xs = _ffn(xp, xs, *p["ffn1"], tile_m=FFN_TILE_M)
        xp3, s_p, c_p, sc_p = _mixer_prompt(
            xp.reshape(bp, seq, d), p["nw"], p["wint"], p["cw"], p["cb"], p["dtb"], p["alog"],
            p["dskip"], p["snw"], p["scw"], p["wout"])
        z, xs_conv, b_rows, c_rows, xdt_t, decay, ysc, c_s, sc_s = _sample_pre(
            xs, p["nw"], w_in_t[i], p["cw"], p["cb"], p["dtb"], p["alog"], p["scw"],
            jnp.swapaxes(state_ssd_conv[i], 0, 1), state_sconv[i])
        s_s, yraw = _sample_state(state_ssm[i].reshape(bs, SSD_WIDTH, D_STATE), xdt_t, decay, b_rows, c_rows)
        xs = _sample_post(xs, yraw, xs_conv, z, ysc, p["dskip"], p["snw"], w_out[i])
        xp, xs = _ffn(xp3.reshape(bp * seq, d), xs, *p["ffn2"], fnw if last else None, tile_m=FFN_TILE_M)
        for lst, v in zip(outs, (s_p.reshape(bp, N_HEADS, HEAD_DIM, D_STATE), c_p, sc_p,
                                 s_s.reshape(bs, N_HEADS, HEAD_DIM, D_STATE),
                                 jnp.swapaxes(c_s, 0, 1), sc_s)):
            lst.append(v)
    return (xp.reshape(bp, seq, d), xs.reshape(bs, dec_seq, d)) + tuple(jnp.stack(l) for l in outs)
```

```python
import functools

import jax
import jax.numpy as jnp
from jax import lax
from jax.experimental import pallas as pl
from jax.experimental.pallas import tpu as pltpu

F32 = jnp.float32
BF16 = jnp.bfloat16

D_MODEL = 1024
SSD_WIDTH = 1024
SC_WIDTH = 1024
HEAD_DIM = 64
N_HEADS = SSD_WIDTH // HEAD_DIM
N_GROUPS = 2
HEADS_PER_GROUP = N_HEADS // N_GROUPS
GROUP_WIDTH = SSD_WIDTH // N_GROUPS
D_STATE = 128
SSD_CONV_W = 4
SSD_CONV_DIM = SSD_WIDTH + 2 * N_GROUPS * D_STATE
SC_CONV_W = 3
CHUNK = 256
NORM_EPS = 1e-6

LANES = 128
SUBLANES = 8

OFF_Z = 0
OFF_XBC = OFF_Z + SSD_WIDTH
OFF_DT = OFF_XBC + SSD_CONV_DIM
OFF_SCB = OFF_DT + N_HEADS
OFF_SCC = OFF_SCB + SC_WIDTH
OFF_SCH = OFF_SCC + SC_WIDTH
D_IN_PROJ = OFF_SCH + SC_WIDTH
PROJ_BLOCK = 512

VMEM_LIMIT_BYTES = 56 * 1024 * 1024


def _rmsnorm(x, w):
    ms = jnp.mean(x * x, axis=-1, keepdims=True)
    return x * lax.rsqrt(ms + NORM_EPS) * w


def _silu(x):
    half = 0.5 * x
    return half * jnp.tanh(half) + half


def _silu_mixer(x):
    return x * (0.5 * jnp.tanh(0.5 * x) + 0.5)


def _softplus(x):
    return jnp.maximum(x, 0.0) + jnp.log1p(jnp.exp(-jnp.abs(x)))


def _dot(a, b):
    return jnp.dot(a, b, preferred_element_type=F32)


def _dot_nt(a, b):
    return lax.dot_general(a, b, (((1,), (1,)), ((), ())), preferred_element_type=F32)


def _split3(x):
    hi = x.astype(BF16)
    r1 = x - hi.astype(F32)
    mid = r1.astype(BF16)
    lo = (r1 - mid.astype(F32)).astype(BF16)
    return hi, mid, lo


def _expand_heads(v, rows):
    lane = lax.broadcasted_iota(jnp.int32, (rows, LANES), 1)
    first_half = lane < HEAD_DIM
    blocks = []
    for q in range(N_HEADS // 2):
        c0 = jnp.broadcast_to(v[:, 2 * q:2 * q + 1], (rows, LANES))
        c1 = jnp.broadcast_to(v[:, 2 * q + 1:2 * q + 2], (rows, LANES))
        blocks.append(jnp.where(first_half, c0, c1))
    return jnp.concatenate(blocks, axis=1)


def _group_rmsnorm(y, w):
    outs = []
    for g in range(N_GROUPS):
        sl = slice(g * GROUP_WIDTH, (g + 1) * GROUP_WIDTH)
        outs.append(_rmsnorm(y[:, sl], w[:, sl]))
    return jnp.concatenate(outs, axis=1)


def _shift_rows(x, prev_tile, j):
    rolled = pltpu.roll(x, j, 0)
    row = lax.broadcasted_iota(jnp.int32, prev_tile.shape, 0)
    head = jnp.where(row < j, pltpu.roll(prev_tile, j, 0), rolled[0:SUBLANES])
    return jnp.concatenate([head, rolled[SUBLANES:]], axis=0)


def _causal_conv(u, prev_tile, w):
    k = w.shape[0]
    acc = u * w[k - 1:k, :]
    for j in range(1, k):
        acc = acc + _shift_rows(u, prev_tile, j) * w[k - 1 - j:k - j, :]
    return acc


WEIGHT_LOAD_STEPS = 8


def _load_weight_rows(step, w_ref, w_bf_ref):
    rows = w_ref.shape[0]
    r0 = pl.multiple_of(step * rows, rows)
    w_bf_ref[pl.ds(r0, rows), :] = w_ref[...].astype(BF16)


def _weight_block_rows(rows, steps):
    tile = 2 * SUBLANES
    return -(-rows // (steps * tile)) * tile


def _weight_rows_spec(w, steps):
    _, rows, cols = w.stacked.shape
    return pl.BlockSpec((None, _weight_block_rows(rows, steps), cols),
                        lambda i: (w.layer, jnp.minimum(i, steps - 1), 0))


def _weight_scratch(w, steps):
    rows, cols = w.shape
    return pltpu.VMEM((steps * _weight_block_rows(rows, steps), cols), BF16)


class _LayerWeight:
    def __init__(self, stacked, layer):
        self.stacked, self.layer = stacked, layer
        self.shape = stacked.shape[1:]


def _ffn_kernel(*refs, final_norm):
    if final_norm:
        (xp_ref, xs_ref, nw_ref, wg_ref, wu_ref, wd_ref, fnw_ref, op_ref, os_ref,
         wg_bf, wu_bf, wd_bf) = refs
    else:
        xp_ref, xs_ref, nw_ref, wg_ref, wu_ref, wd_ref, op_ref, os_ref, wg_bf, wu_bf, wd_bf = refs

    def half_step(x_ref, o_ref):
        rows = x_ref.shape[0]
        sub = min(rows, FFN_SUB_TILE_M)
        for r0 in range(0, rows, sub):
            x = x_ref[r0:r0 + sub, :] if len(x_ref.shape) == 2 else x_ref[r0:r0 + sub, 0, :]
            xn = _rmsnorm(x, nw_ref[...]).astype(BF16)
            g = _dot(xn, wg_bf[...])
            u = _dot(xn, wu_bf[...])
            hmid = (_silu(g) * u).astype(BF16)
            y = x + 0.5 * _dot(hmid, wd_bf[...])
            if final_norm:
                y = _rmsnorm(y, fnw_ref[...])
            if len(o_ref.shape) == 2:
                o_ref[r0:r0 + sub, :] = y
            else:
                o_ref[r0:r0 + sub, 0, :] = y

    i = pl.program_id(0)
    sample_step = pl.num_programs(0) - 1

    @pl.when(i < WEIGHT_LOAD_STEPS)
    def _():
        for w_ref, w_bf in ((wg_ref, wg_bf), (wu_ref, wu_bf), (wd_ref, wd_bf)):
            _load_weight_rows(i, w_ref, w_bf)

    @pl.when((i >= WEIGHT_LOAD_STEPS) & (i < sample_step))
    def _():
        half_step(xp_ref, op_ref)

    @pl.when(i == sample_step)
    def _():
        half_step(xs_ref, os_ref)


def _resident(shape):
    return pl.BlockSpec(shape, lambda *_: (0,) * len(shape), pipeline_mode=pl.Buffered(1))


def _ffn(xp, xs, nw, wg, wu, wd, fnw=None, *, tile_m, sample_out_shape):
    m, d = xp.shape
    ms = xs.shape[0]
    dff = wg.shape[1]
    assert m % tile_m == 0
    n = m // tile_m
    final_norm = fnw is not None
    ws = WEIGHT_LOAD_STEPS
    prompt_tile = pl.BlockSpec((tile_m, d), lambda i: (jnp.clip(i - ws, 0, n - 1), 0))
    whole = lambda shape: pl.BlockSpec(shape, lambda i: (0,) * len(shape))
    in_specs = [prompt_tile, whole(xs.shape), _resident((1, d)),
                _weight_rows_spec(wg, ws), _weight_rows_spec(wu, ws), _weight_rows_spec(wd, ws)]
    args = [xp, xs, nw, wg.stacked, wu.stacked, wd.stacked]
    if final_norm:
        in_specs.append(_resident((1, d)))
        args.append(fnw)
    return pl.pallas_call(
        functools.partial(_ffn_kernel, final_norm=final_norm),
        out_shape=(jax.ShapeDtypeStruct((m, d), F32), jax.ShapeDtypeStruct(sample_out_shape, F32)),
        grid=(ws + n + 1,),
        in_specs=in_specs,
        out_specs=(prompt_tile, whole(sample_out_shape)),
        scratch_shapes=[_weight_scratch(wg, ws), _weight_scratch(wu, ws), _weight_scratch(wd, ws)],
        compiler_params=pltpu.CompilerParams(
            dimension_semantics=("arbitrary",), vmem_limit_bytes=VMEM_LIMIT_BYTES),
        name="ffn_final" if final_norm else "ffn",
    )(*args)


def _in_proj(h, wint_ref, col0, width):
    return _dot_nt(h, wint_ref[col0:col0 + width, :])


def _dt_proj(h, wint_ref):
    raw = _in_proj(h, wint_ref, OFF_DT, LANES)
    lane = lax.broadcasted_iota(jnp.int32, raw.shape, 1)
    return jnp.where(lane < N_HEADS, raw, 0.0)


def _mixer_kernel(tiles_per_seq, x_ref, nw_ref, wint_ref, cw_ref, cb_ref, dtb_ref, alog_ref,
                  dskip_ref, snw_ref, scw_ref, wout_ref,
                  o_ref, sfin_ref, cfin_ref, scfin_ref,
                  st_scr, hist_scr, uhist_scr, wint_bf, wout_bf):
    i = pl.program_id(0)

    @pl.when(i < MIXER_WEIGHT_LOAD_STEPS)
    def _():
        _load_weight_rows(i, wint_ref, wint_bf)
        _load_weight_rows(i, wout_ref, wout_bf)

    @pl.when(i >= MIXER_WEIGHT_LOAD_STEPS)
    def _():
        c = lax.rem(i - MIXER_WEIGHT_LOAD_STEPS, tiles_per_seq)
        _mixer_chunk(c == 0, c == tiles_per_seq - 1, x_ref, nw_ref, wint_bf, cw_ref,
                     cb_ref, dtb_ref, alog_ref, dskip_ref, snw_ref, scw_ref, wout_bf,
                     o_ref, sfin_ref, cfin_ref, scfin_ref, st_scr, hist_scr, uhist_scr)


def _mixer_chunk(first, last, x_ref, nw_ref, wint_ref, cw_ref, cb_ref, dtb_ref, alog_ref,
                 dskip_ref, snw_ref, scw_ref, wout_ref,
                 o_ref, sfin_ref, cfin_ref, scfin_ref,
                 st_scr, hist_scr, uhist_scr):
    L = CHUNK
    T = x_ref.shape[1]

    @pl.when(first)
    def _():
        st_scr[...] = jnp.zeros_like(st_scr)
        hist_scr[...] = jnp.zeros_like(hist_scr)
        uhist_scr[...] = jnp.zeros_like(uhist_scr)

    x = x_ref[0]
    h = _rmsnorm(x, nw_ref[...]).astype(BF16)
    dt_raw = _dt_proj(h, wint_ref)

    xbc_c_blocks = []
    for b0 in range(0, SSD_CONV_DIM, PROJ_BLOCK):
        cols = slice(b0, b0 + PROJ_BLOCK)
        xbc = _in_proj(h, wint_ref, OFF_XBC + b0, PROJ_BLOCK)
        acc = _causal_conv(xbc, hist_scr[:, cols], cw_ref[:, cols])
        hist_scr[:, cols] = xbc[T - SUBLANES:T, :]
        xbc_c_blocks.append(_silu_mixer(acc + cb_ref[:, cols]))
    cfin_ref[0] = hist_scr[SUBLANES - (SSD_CONV_W - 1):SUBLANES, :]
    xbc_c_all = jnp.concatenate(xbc_c_blocks, axis=1)

    def plus_zero_of(v, gate):
        bits = pltpu.bitcast(gate[0:SUBLANES, 0:LANES], jnp.uint32)
        half_word = jnp.uint32(16)
        zero_row = pltpu.bitcast(
            lax.shift_right_logical(lax.shift_right_logical(bits, half_word), half_word), F32)[0:1, :]
        return v + jnp.concatenate([zero_row] * (D_MODEL // LANES), axis=1).astype(BF16)

    dt_all = _softplus(dt_raw + dtb_ref[...])
    a_all = dt_all * (-jnp.exp(alog_ref[...]))
    row = lax.broadcasted_iota(jnp.int32, (L, L), 0)
    col = lax.broadcasted_iota(jnp.int32, (L, L), 1)
    causal = row >= col
    tril = jnp.where(causal, 1.0, 0.0).astype(BF16)
    lane = lax.broadcasted_iota(jnp.int32, (L, LANES), 1)
    first_half = lane < HEAD_DIM

    y_ssd_chunks = []
    out_sc_chunks = []
    for k in range(T // L):
        rows = slice(k * L, (k + 1) * L)
        xbc_c = xbc_c_all[rows]
        xs = xbc_c[:, 0:SSD_WIDTH]
        b_all = xbc_c[:, SSD_WIDTH:SSD_WIDTH + N_GROUPS * D_STATE]
        c_all = xbc_c[:, SSD_WIDTH + N_GROUPS * D_STATE:SSD_CONV_DIM]
        dt = dt_all[rows]
        a_hi, a_mid, a_lo = _split3(a_all[rows])
        acum = _dot(tril, a_hi) + _dot(tril, a_mid) + _dot(tril, a_lo)
        acum_t = acum.T

        def h_after(gate):
            return plus_zero_of(h[rows], gate)

        def z_gate_after(gate):
            h_late = h_after(gate)
            return jnp.concatenate(
                [_silu_mixer(_in_proj(h_late, wint_ref, OFF_Z + b0, PROJ_BLOCK))
                 for b0 in range(0, SSD_WIDTH, PROJ_BLOCK)], axis=1)

        def short_conv_block_after(gate, b0):
            h_late = h_after(gate)
            cols = slice(b0, b0 + PROJ_BLOCK)
            scc = _in_proj(h_late, wint_ref, OFF_SCC + b0, PROJ_BLOCK)
            sch = _in_proj(h_late, wint_ref, OFF_SCH + b0, PROJ_BLOCK)
            u = scc * sch
            v = _causal_conv(u, uhist_scr[:, cols], scw_ref[:, cols])
            uhist_scr[:, cols] = u[L - SUBLANES:L, :]
            scb = _in_proj(h_late, wint_ref, OFF_SCB + b0, PROJ_BLOCK)
            return (scb * v).astype(BF16)

        z_gate = z_gate_after(acum)
        y_sc = jnp.concatenate(
            [short_conv_block_after(acum, b0) for b0 in range(0, SC_WIDTH, PROJ_BLOCK)], axis=1)
        out_sc_chunks.append(_dot(y_sc, wout_ref[SSD_WIDTH:SSD_WIDTH + SC_WIDTH, :]))

        acum_e = _expand_heads(acum, L)
        dt_e = _expand_heads(dt, L)
        exp_acum_e = jnp.exp(acum_e)
        decay_end_e = jnp.exp(acum_e[L - 1:L, :] - acum_e)
        xdt = xs * dt_e
        xdt_b = xdt.astype(BF16)
        xdecay_b = (xdt * decay_end_e).astype(BF16)

        y_diag_blocks = []
        y_off_blocks = []
        for g in range(N_GROUPS):
            b_g = b_all[:, g * D_STATE:(g + 1) * D_STATE]
            c_g = c_all[:, g * D_STATE:(g + 1) * D_STATE]
            b_gb = b_g.astype(BF16)
            c_gb = c_g.astype(BF16)
            cb = _dot_nt(c_gb, b_gb)
            for q in range(g * HEADS_PER_GROUP // 2, (g + 1) * HEADS_PER_GROUP // 2):
                ms = []
                for hh in (2 * q, 2 * q + 1):
                    seg = acum[:, hh:hh + 1] - acum_t[hh:hh + 1, :]
                    decay = jnp.exp(jnp.where(causal, seg, -jnp.inf))
                    ms.append((cb * decay).astype(BF16))
                m_cat = jnp.concatenate(ms, axis=1)
                x2 = xdt_b[:, q * LANES:(q + 1) * LANES]
                zero = jnp.zeros_like(x2)
                rhs = jnp.concatenate([jnp.where(first_half, x2, zero),
                                       jnp.where(first_half, zero, x2)], axis=0)
                y_diag_blocks.append(_dot(m_cat, rhs))
            gs = slice(g * GROUP_WIDTH, (g + 1) * GROUP_WIDTH)
            s_enter = st_scr[:, gs]
            y_off_blocks.append(_dot(c_gb, s_enter.astype(BF16)))
            new_states = _dot(b_g.T.astype(BF16), xdecay_b[:, gs])
            st_scr[:, gs] = s_enter * exp_acum_e[L - 1:L, gs] + new_states
        y_diag = jnp.concatenate(y_diag_blocks, axis=1)
        y_off = jnp.concatenate(y_off_blocks, axis=1) * exp_acum_e

        y = (y_diag + y_off + dskip_ref[...] * xs) * z_gate
        y_ssd_chunks.append(_group_rmsnorm(y, snw_ref[...]).astype(BF16))

    scfin_ref[0] = uhist_scr[SUBLANES - (SC_CONV_W - 1):SUBLANES, :]
    out_ssd = _dot(jnp.concatenate(y_ssd_chunks, axis=0), wout_ref[0:SSD_WIDTH, :])
    o_ref[0] = x + jnp.concatenate(out_sc_chunks, axis=0) + out_ssd

    @pl.when(last)
    def _():
        sfin_ref[0] = st_scr[...].T


MIXER_CHUNKS_PER_STEP = 2
MIXER_WEIGHT_LOAD_STEPS = WEIGHT_LOAD_STEPS


def _mixer_prompt(x, nw, wint, cw, cb, dtb, alog, dskip, snw, scw, wout):
    nb, seq, d = x.shape
    tile_t = MIXER_CHUNKS_PER_STEP * CHUNK
    assert seq % tile_t == 0
    nc = seq // tile_t
    ws = MIXER_WEIGHT_LOAD_STEPS

    def chunk_map(i):
        t = jnp.maximum(i - ws, 0)
        return (t // nc, t % nc, 0)

    def seq_map(i):
        return (jnp.maximum(i - ws, 0) // nc, 0, 0)

    out_shape = (
        jax.ShapeDtypeStruct((nb, seq, d), F32),
        jax.ShapeDtypeStruct((nb, SSD_WIDTH, D_STATE), F32),
        jax.ShapeDtypeStruct((nb, SSD_CONV_W - 1, SSD_CONV_DIM), F32),
        jax.ShapeDtypeStruct((nb, SC_CONV_W - 1, SC_WIDTH), F32),
    )
    return pl.pallas_call(
        functools.partial(_mixer_kernel, nc),
        out_shape=out_shape,
        grid=(ws + nb * nc,),
        in_specs=[pl.BlockSpec((1, tile_t, d), chunk_map),
                  _resident((1, d)), _weight_rows_spec(wint, ws),
                  _resident((SSD_CONV_W, SSD_CONV_DIM)), _resident((1, SSD_CONV_DIM)),
                  _resident((1, LANES)), _resident((1, LANES)), _resident((1, SSD_WIDTH)),
                  _resident((1, SSD_WIDTH)), _resident((SC_CONV_W, SC_WIDTH)),
                  _weight_rows_spec(wout, ws)],
        out_specs=(pl.BlockSpec((1, tile_t, d), chunk_map),
                   pl.BlockSpec((1, SSD_WIDTH, D_STATE), seq_map),
                   pl.BlockSpec((1, SSD_CONV_W - 1, SSD_CONV_DIM), seq_map),
                   pl.BlockSpec((1, SC_CONV_W - 1, SC_WIDTH), seq_map)),
        scratch_shapes=[pltpu.VMEM((D_STATE, SSD_WIDTH), F32),
                        pltpu.VMEM((SUBLANES, SSD_CONV_DIM), F32),
                        pltpu.VMEM((SUBLANES, SC_WIDTH), F32),
                        _weight_scratch(wint, ws), _weight_scratch(wout, ws)],
        compiler_params=pltpu.CompilerParams(
            dimension_semantics=("arbitrary",), vmem_limit_bytes=VMEM_LIMIT_BYTES),
        name="mixer_prompt",
    )(x, nw, wint.stacked, cw, cb, dtb, alog, dskip, snw, scw, wout.stacked)


def _sample_pre_kernel(x_ref, nw_ref, wint_ref, cw_ref, cb_ref, dtb_ref, alog_ref,
                       scw_ref, cst_ref, scst_ref,
                       z_ref, xs_ref, b_ref, c_ref, xdt_blk_ref, da_ref, ysc_ref, cnew_ref, scnew_ref,
                       wint_bf):
    nb = x_ref.shape[0]
    x = x_ref[...]
    h = _rmsnorm(x, nw_ref[...]).astype(BF16)
    wint_bf[...] = wint_ref[...].astype(BF16)
    dt_raw = _dt_proj(h, wint_bf)
    xbc = _in_proj(h, wint_bf, OFF_XBC, SSD_CONV_DIM)
    z_ref[...] = _in_proj(h, wint_bf, OFF_Z, SSD_WIDTH)
    scb = _in_proj(h, wint_bf, OFF_SCB, SC_WIDTH)
    scc = _in_proj(h, wint_bf, OFF_SCC, SC_WIDTH)
    sch = _in_proj(h, wint_bf, OFF_SCH, SC_WIDTH)

    cw = cw_ref[...]
    acc = xbc * cw[SSD_CONV_W - 1:SSD_CONV_W, :]
    for k in range(SSD_CONV_W - 1):
        acc = acc + cst_ref[k] * cw[k:k + 1, :]
    for k in range(SSD_CONV_W - 2):
        cnew_ref[k] = cst_ref[k + 1]
    cnew_ref[SSD_CONV_W - 2] = xbc
    xbc_c = _silu(acc + cb_ref[...])
    xs = xbc_c[:, 0:SSD_WIDTH]
    xs_ref[...] = xs
    b_ref[...] = xbc_c[:, SSD_WIDTH:SSD_WIDTH + N_GROUPS * D_STATE]
    c_ref[...] = xbc_c[:, SSD_WIDTH + N_GROUPS * D_STATE:SSD_CONV_DIM]

    dt = _softplus(dt_raw + dtb_ref[...])
    da_ref[...] = jnp.exp(dt * (-jnp.exp(alog_ref[...])))
    xdt_t = (xs * _expand_heads(dt, nb)).T
    bb = xdt_blk_ref.shape[2]
    for j in range(nb // bb):
        xdt_blk_ref[j] = xdt_t[:, j * bb:(j + 1) * bb]

    u = scc * sch
    scw = scw_ref[...]
    v = u * scw[SC_CONV_W - 1:SC_CONV_W, :]
    for k in range(SC_CONV_W - 1):
        v = v + scst_ref[:, k, :] * scw[k:k + 1, :]
    for k in range(SC_CONV_W - 2):
        scnew_ref[:, k, :] = scst_ref[:, k + 1, :]
    scnew_ref[:, SC_CONV_W - 2, :] = u
    ysc_ref[...] = scb * v


def _sample_pre(x, nw, wint, cw, cb, dtb, alog, scw, cst, scst):
    nb, d = x.shape
    f = lambda *s: jax.ShapeDtypeStruct(s, F32)
    bb = STATE_BATCH_BLOCK
    assert nb % bb == 0
    out_shape = (f(nb, SSD_WIDTH), f(nb, SSD_WIDTH), f(nb, N_GROUPS * D_STATE), f(nb, N_GROUPS * D_STATE),
                 f(nb // bb, SSD_WIDTH, bb), f(nb, LANES), f(nb, SC_WIDTH),
                 f(SSD_CONV_W - 1, nb, SSD_CONV_DIM), f(nb, SC_CONV_W - 1, SC_WIDTH))
    return pl.pallas_call(
        _sample_pre_kernel,
        out_shape=out_shape,
        scratch_shapes=[pltpu.VMEM(wint.shape, BF16)],
        compiler_params=pltpu.CompilerParams(vmem_limit_bytes=VMEM_LIMIT_BYTES),
        name="sample_pre",
    )(x, nw, wint, cw, cb, dtb, alog, scw, cst, scst)


STATE_BATCH_BLOCK = 16


def _split2(x):
    hi = x.astype(BF16)
    return hi, (x - hi.astype(F32)).astype(BF16)


def _sample_state_kernel(da_ref, s0_ref, xdt_ref, b_ref, c_ref, snew_ref, y_ref):
    bb = STATE_BATCH_BLOCK
    blk = pl.program_id(0)
    xdt_t = xdt_ref[0]
    b_rows = b_ref[...]
    c_hi, c_lo = _split2(c_ref[...])
    row = lax.broadcasted_iota(jnp.int32, (bb, SSD_WIDTH), 0)
    y_blk = jnp.zeros((bb, SSD_WIDTH), F32)
    for i in range(bb):
        xdt_col = xdt_t[:, i:i + 1]
        y_parts = []
        for g in range(N_GROUPS):
            ns = slice(g * D_STATE, (g + 1) * D_STATE)
            b_row = b_rows[i:i + 1, ns]
            heads = []
            for hh in range(g * HEADS_PER_GROUP, (g + 1) * HEADS_PER_GROUP):
                rs = slice(hh * HEAD_DIM, (hh + 1) * HEAD_DIM)
                decay = da_ref[blk * bb + i, hh]
                heads.append(s0_ref[i, rs, :] * decay + xdt_col[rs] * b_row)
            s_new = jnp.concatenate(heads, axis=0)
            snew_ref[i, g * GROUP_WIDTH:(g + 1) * GROUP_WIDTH, :] = s_new
            s_hi, s_lo = _split2(s_new)
            lhs = jnp.concatenate([c_hi[:, ns], c_lo[:, ns]], axis=0)
            r_hi = _dot_nt(lhs, s_hi)
            r_lo = _dot_nt(c_hi[:, ns], s_lo)
            y_parts.append(r_hi[i:i + 1] + r_hi[bb + i:bb + i + 1] + r_lo[i:i + 1])
        y_row = jnp.concatenate(y_parts, axis=1)
        y_blk = jnp.where(row == i, y_row, y_blk)
    y_ref[...] = y_blk


def _sample_state(s0, xdt_blocks, decay, b_rows, c_rows):
    nb = s0.shape[0]
    nblk, _, bb = xdt_blocks.shape
    assert bb == STATE_BATCH_BLOCK and nblk * bb == nb
    return pl.pallas_call(
        _sample_state_kernel,
        out_shape=(jax.ShapeDtypeStruct((nb, SSD_WIDTH, D_STATE), F32),
                   jax.ShapeDtypeStruct((nb, SSD_WIDTH), F32)),
        grid=(nblk,),
        in_specs=[pl.BlockSpec(memory_space=pltpu.SMEM),
                  pl.BlockSpec((bb, SSD_WIDTH, D_STATE), lambda i: (i, 0, 0)),
                  pl.BlockSpec((1, SSD_WIDTH, bb), lambda i: (i, 0, 0)),
                  pl.BlockSpec((bb, N_GROUPS * D_STATE), lambda i: (i, 0)),
                  pl.BlockSpec((bb, N_GROUPS * D_STATE), lambda i: (i, 0))],
        out_specs=(pl.BlockSpec((bb, SSD_WIDTH, D_STATE), lambda i: (i, 0, 0)),
                   pl.BlockSpec((bb, SSD_WIDTH), lambda i: (i, 0))),
        compiler_params=pltpu.CompilerParams(
            dimension_semantics=("arbitrary",), vmem_limit_bytes=VMEM_LIMIT_BYTES),
        name="sample_state",
    )(decay, s0, xdt_blocks, b_rows, c_rows)


def _sample_post_kernel(x_ref, yraw_ref, xs_ref, z_ref, ysc_ref, dskip_ref, snw_ref, wout_ref, o_ref):
    y = yraw_ref[...] + dskip_ref[...] * xs_ref[...]
    y = y * _silu(z_ref[...])
    y_ssd = _group_rmsnorm(y, snw_ref[...])
    mixed = jnp.concatenate([y_ssd, ysc_ref[...]], axis=1).astype(BF16)
    o_ref[...] = x_ref[...] + _dot(mixed, wout_ref[...].astype(BF16))


def _sample_post(x, yraw, xs, z, ysc, dskip, snw, wout):
    return pl.pallas_call(
        _sample_post_kernel,
        out_shape=jax.ShapeDtypeStruct(x.shape, F32),
        compiler_params=pltpu.CompilerParams(vmem_limit_bytes=VMEM_LIMIT_BYTES),
        name="sample_post",
    )(x, yraw, xs, z, ysc, dskip, snw, wout)


FFN_TILE_M = 1024
FFN_SUB_TILE_M = 256


def _layer_params(i, norm_ffn1_w, ffn1_w_gate, ffn1_w_up, ffn1_w_down, norm_mix_w, w_in_t,
                  ssd_conv_w, ssd_conv_b, dt_bias, a_log, d_skip, ssd_norm_w, sconv_w, w_out,
                  norm_ffn2_w, ffn2_w_gate, ffn2_w_up, ffn2_w_down):
    pad_heads = lambda v: jnp.pad(v, (0, LANES - N_HEADS)).reshape(1, LANES)
    row = lambda v: v.reshape(1, -1)
    return dict(
        ffn1=(row(norm_ffn1_w[i]), _LayerWeight(ffn1_w_gate, i), _LayerWeight(ffn1_w_up, i),
              _LayerWeight(ffn1_w_down, i)),
        ffn2=(row(norm_ffn2_w[i]), _LayerWeight(ffn2_w_gate, i), _LayerWeight(ffn2_w_up, i),
              _LayerWeight(ffn2_w_down, i)),
        nw=row(norm_mix_w[i]), wint=_LayerWeight(w_in_t, i), cw=ssd_conv_w[i], cb=row(ssd_conv_b[i]),
        dtb=pad_heads(dt_bias[i]), alog=pad_heads(a_log[i]),
        dskip=row(jnp.repeat(d_skip[i], HEAD_DIM)), snw=row(ssd_norm_w[i]), scw=sconv_w[i],
        wout=_LayerWeight(w_out, i))


def kernel(x_prompt, x_sample, state_ssm, state_ssd_conv, state_sconv, norm_ffn1_w, ffn1_w_gate, ffn1_w_up, ffn1_w_down, norm_mix_w, w_in, ssd_conv_w, ssd_conv_b, dt_bias, a_log, d_skip, ssd_norm_w, sconv_w, w_out, norm_ffn2_w, ffn2_w_gate, ffn2_w_up, ffn2_w_down, final_norm_w):
    depth = w_in.shape[0]
    bp, seq, d = x_prompt.shape
    bs, dec_seq, _ = x_sample.shape
    assert dec_seq == 1, "sample group is one token per sequence"
    fnw = final_norm_w.reshape(1, d)
    w_in_t = jnp.swapaxes(w_in, 1, 2)

    xp = x_prompt.reshape(bp * seq, d)
    xs = x_sample
    outs = [[] for _ in range(6)]
    for i in range(depth):
        p = _layer_params(i, norm_ffn1_w, ffn1_w_gate, ffn1_w_up, ffn1_w_down, norm_mix_w, w_in_t,
                          ssd_conv_w, ssd_conv_b, dt_bias, a_log, d_skip, ssd_norm_w, sconv_w,
                          w_out, norm_ffn2_w, ffn2_w_gate, ffn2_w_up, ffn2_w_down)
        last = i == depth - 1
        xp, xs = _ffn(xp, xs, *p["ffn1"], tile_m=FFN_TILE_M, sample_out_shape=(bs, d))
        xp3, s_p, c_p, sc_p = _mixer_prompt(
            xp.reshape(bp, seq, d), p["nw"], p["wint"], p["cw"], p["cb"], p["dtb"], p["alog"],
            p["dskip"], p["snw"], p["scw"], p["wout"])
        z, xs_conv, b_rows, c_rows, xdt_t, decay, ysc, c_s, sc_s = _sample_pre(
            xs, p["nw"], w_in_t[i], p["cw"], p["cb"], p["dtb"], p["alog"], p["scw"],
            jnp.swapaxes(state_ssd_conv[i], 0, 1), state_sconv[i])
        s_s, yraw = _sample_state(state_ssm[i].reshape(bs, SSD_WIDTH, D_STATE), xdt_t, decay, b_rows, c_rows)
        xs = _sample_post(xs, yraw, xs_conv, z, ysc, p["dskip"], p["snw"], w_out[i])
        xp, xs = _ffn(xp3.reshape(bp * seq, d), xs, *p["ffn2"], fnw if last else None, tile_m=FFN_TILE_M,
                      sample_out_shape=(bs, dec_seq, d) if last else (bs, d))
        for lst, v in zip(outs, (s_p.reshape(bp, N_HEADS, HEAD_DIM, D_STATE), c_p, sc_p,
                                 s_s.reshape(bs, N_HEADS, HEAD_DIM, D_STATE),
                                 jnp.swapaxes(c_s, 0, 1), sc_s)):
            lst.append(v)
    return (xp.reshape(bp, seq, d), xs) + tuple(jnp.stack(l) for l in outs)
```

```python
import functools

import jax
import jax.numpy as jnp
from jax import lax
from jax.experimental import pallas as pl
from jax.experimental.pallas import tpu as pltpu

F32 = jnp.float32
BF16 = jnp.bfloat16

D_MODEL = 1024
SSD_WIDTH = 1024
SC_WIDTH = 1024
HEAD_DIM = 64
N_HEADS = SSD_WIDTH // HEAD_DIM
N_GROUPS = 2
HEADS_PER_GROUP = N_HEADS // N_GROUPS
GROUP_WIDTH = SSD_WIDTH // N_GROUPS
D_STATE = 128
SSD_CONV_W = 4
SSD_CONV_DIM = SSD_WIDTH + 2 * N_GROUPS * D_STATE
SC_CONV_W = 3
CHUNK = 256
NORM_EPS = 1e-6

LANES = 128
SUBLANES = 8

OFF_Z = 0
OFF_XBC = OFF_Z + SSD_WIDTH
OFF_DT = OFF_XBC + SSD_CONV_DIM
OFF_SCB = OFF_DT + N_HEADS
OFF_SCC = OFF_SCB + SC_WIDTH
OFF_SCH = OFF_SCC + SC_WIDTH
D_IN_PROJ = OFF_SCH + SC_WIDTH
PROJ_BLOCK = 512

VMEM_LIMIT_BYTES = 56 * 1024 * 1024


def _rmsnorm(x, w):
    ms = jnp.mean(x * x, axis=-1, keepdims=True)
    return x * lax.rsqrt(ms + NORM_EPS) * w


def _silu(x):
    half = 0.5 * x
    return half * jnp.tanh(half) + half


def _silu_mixer(x):
    return x * (0.5 * jnp.tanh(0.5 * x) + 0.5)


def _softplus(x):
    return jnp.maximum(x, 0.0) + jnp.log1p(jnp.exp(-jnp.abs(x)))


def _dot(a, b):
    return jnp.dot(a, b, preferred_element_type=F32)


def _dot_nt(a, b):
    return lax.dot_general(a, b, (((1,), (1,)), ((), ())), preferred_element_type=F32)


def _split3(x):
    hi = x.astype(BF16)
    r1 = x - hi.astype(F32)
    mid = r1.astype(BF16)
    lo = (r1 - mid.astype(F32)).astype(BF16)
    return hi, mid, lo


def _expand_heads(v, rows):
    lane = lax.broadcasted_iota(jnp.int32, (rows, LANES), 1)
    first_half = lane < HEAD_DIM
    blocks = []
    for q in range(N_HEADS // 2):
        c0 = jnp.broadcast_to(v[:, 2 * q:2 * q + 1], (rows, LANES))
        c1 = jnp.broadcast_to(v[:, 2 * q + 1:2 * q + 2], (rows, LANES))
        blocks.append(jnp.where(first_half, c0, c1))
    return jnp.concatenate(blocks, axis=1)


def _group_rmsnorm(y, w):
    outs = []
    for g in range(N_GROUPS):
        sl = slice(g * GROUP_WIDTH, (g + 1) * GROUP_WIDTH)
        outs.append(_rmsnorm(y[:, sl], w[:, sl]))
    return jnp.concatenate(outs, axis=1)


def _shift_rows(x, prev_tile, j):
    rolled = pltpu.roll(x, j, 0)
    row = lax.broadcasted_iota(jnp.int32, prev_tile.shape, 0)
    head = jnp.where(row < j, pltpu.roll(prev_tile, j, 0), rolled[0:SUBLANES])
    return jnp.concatenate([head, rolled[SUBLANES:]], axis=0)


def _causal_conv(u, prev_tile, w):
    k = w.shape[0]
    acc = u * w[k - 1:k, :]
    for j in range(1, k):
        acc = acc + _shift_rows(u, prev_tile, j) * w[k - 1 - j:k - j, :]
    return acc


WEIGHT_LOAD_STEPS = 8


def _load_weight_rows(step, w_ref, w_bf_ref):
    rows = w_ref.shape[0]
    r0 = pl.multiple_of(step * rows, rows)
    w_bf_ref[pl.ds(r0, rows), :] = w_ref[...].astype(BF16)


def _weight_block_rows(rows, steps):
    tile = 2 * SUBLANES
    return -(-rows // (steps * tile)) * tile


def _weight_rows_spec(w, steps):
    _, rows, cols = w.stacked.shape
    return pl.BlockSpec((None, _weight_block_rows(rows, steps), cols),
                        lambda i: (w.layer, jnp.minimum(i, steps - 1), 0))


def _weight_scratch(w, steps):
    rows, cols = w.shape
    return pltpu.VMEM((steps * _weight_block_rows(rows, steps), cols), BF16)


class _LayerWeight:
    def __init__(self, stacked, layer):
        self.stacked, self.layer = stacked, layer
        self.shape = stacked.shape[1:]


def _ffn_kernel(*refs, final_norm):
    if final_norm:
        (xp_ref, xs_ref, nw_ref, wg_ref, wu_ref, wd_ref, fnw_ref, op_ref, os_ref,
         wg_bf, wu_bf, wd_bf) = refs
    else:
        xp_ref, xs_ref, nw_ref, wg_ref, wu_ref, wd_ref, op_ref, os_ref, wg_bf, wu_bf, wd_bf = refs

    def half_step(x_ref, o_ref):
        rows = x_ref.shape[0]
        sub = min(rows, FFN_SUB_TILE_M)
        for r0 in range(0, rows, sub):
            x = x_ref[r0:r0 + sub, :] if len(x_ref.shape) == 2 else x_ref[r0:r0 + sub, 0, :]
            xn = _rmsnorm(x, nw_ref[...]).astype(BF16)
            g = _dot(xn, wg_bf[...])
            u = _dot(xn, wu_bf[...])
            hmid = (_silu(g) * u).astype(BF16)
            y = x + 0.5 * _dot(hmid, wd_bf[...])
            if final_norm:
                y = _rmsnorm(y, fnw_ref[...])
            if len(o_ref.shape) == 2:
                o_ref[r0:r0 + sub, :] = y
            else:
                o_ref[r0:r0 + sub, 0, :] = y

    i = pl.program_id(0)
    sample_step = pl.num_programs(0) - 1

    @pl.when(i < WEIGHT_LOAD_STEPS)
    def _():
        for w_ref, w_bf in ((wg_ref, wg_bf), (wu_ref, wu_bf), (wd_ref, wd_bf)):
            _load_weight_rows(i, w_ref, w_bf)

    @pl.when((i >= WEIGHT_LOAD_STEPS) & (i < sample_step))
    def _():
        half_step(xp_ref, op_ref)

    @pl.when(i == sample_step)
    def _():
        half_step(xs_ref, os_ref)


def _resident(shape):
    return pl.BlockSpec(shape, lambda *_: (0,) * len(shape), pipeline_mode=pl.Buffered(1))


def _ffn(xp, xs, nw, wg, wu, wd, fnw=None, *, tile_m, sample_out_shape):
    m, d = xp.shape
    ms = xs.shape[0]
    dff = wg.shape[1]
    assert m % tile_m == 0
    n = m // tile_m
    final_norm = fnw is not None
    ws = WEIGHT_LOAD_STEPS
    prompt_tile = pl.BlockSpec((tile_m, d), lambda i: (jnp.clip(i - ws, 0, n - 1), 0))
    whole = lambda shape: pl.BlockSpec(shape, lambda i: (0,) * len(shape))
    in_specs = [prompt_tile, whole(xs.shape), _resident((1, d)),
                _weight_rows_spec(wg, ws), _weight_rows_spec(wu, ws), _weight_rows_spec(wd, ws)]
    args = [xp, xs, nw, wg.stacked, wu.stacked, wd.stacked]
    if final_norm:
        in_specs.append(_resident((1, d)))
        args.append(fnw)
    return pl.pallas_call(
        functools.partial(_ffn_kernel, final_norm=final_norm),
        out_shape=(jax.ShapeDtypeStruct((m, d), F32), jax.ShapeDtypeStruct(sample_out_shape, F32)),
        grid=(ws + n + 1,),
        in_specs=in_specs,
        out_specs=(prompt_tile, whole(sample_out_shape)),
        scratch_shapes=[_weight_scratch(wg, ws), _weight_scratch(wu, ws), _weight_scratch(wd, ws)],
        compiler_params=pltpu.CompilerParams(
            dimension_semantics=("arbitrary",), vmem_limit_bytes=VMEM_LIMIT_BYTES),
        name="ffn_final" if final_norm else "ffn",
    )(*args)


def _in_proj(h, wint_ref, col0, width):
    return _dot_nt(h, wint_ref[col0:col0 + width, :])


def _dt_proj(h, wint_ref):
    raw = _in_proj(h, wint_ref, OFF_DT, LANES)
    lane = lax.broadcasted_iota(jnp.int32, raw.shape, 1)
    return jnp.where(lane < N_HEADS, raw, 0.0)


def _mixer_kernel(tiles_per_seq, x_ref, nw_ref, wint_ref, cw_ref, cb_ref, dtb_ref, alog_ref,
                  dskip_ref, snw_ref, scw_ref, wout_ref,
                  o_ref, sfin_ref, cfin_ref, scfin_ref,
                  st_scr, hist_scr, uhist_scr, wint_bf, wout_bf):
    i = pl.program_id(0)

    @pl.when(i < MIXER_WEIGHT_LOAD_STEPS)
    def _():
        _load_weight_rows(i, wint_ref, wint_bf)
        _load_weight_rows(i, wout_ref, wout_bf)

    @pl.when(i >= MIXER_WEIGHT_LOAD_STEPS)
    def _():
        c = lax.rem(i - MIXER_WEIGHT_LOAD_STEPS, tiles_per_seq)
        _mixer_chunk(c == 0, c == tiles_per_seq - 1, x_ref, nw_ref, wint_bf, cw_ref,
                     cb_ref, dtb_ref, alog_ref, dskip_ref, snw_ref, scw_ref, wout_bf,
                     o_ref, sfin_ref, cfin_ref, scfin_ref, st_scr, hist_scr, uhist_scr)


def _mixer_chunk(first, last, x_ref, nw_ref, wint_ref, cw_ref, cb_ref, dtb_ref, alog_ref,
                 dskip_ref, snw_ref, scw_ref, wout_ref,
                 o_ref, sfin_ref, cfin_ref, scfin_ref,
                 st_scr, hist_scr, uhist_scr):
    L = CHUNK
    T = x_ref.shape[1]

    @pl.when(first)
    def _():
        st_scr[...] = jnp.zeros_like(st_scr)
        hist_scr[...] = jnp.zeros_like(hist_scr)
        uhist_scr[...] = jnp.zeros_like(uhist_scr)

    x = x_ref[0]
    h = _rmsnorm(x, nw_ref[...]).astype(BF16)
    dt_raw = _dt_proj(h, wint_ref)

    xbc_c_blocks = []
    for b0 in range(0, SSD_CONV_DIM, PROJ_BLOCK):
        cols = slice(b0, b0 + PROJ_BLOCK)
        xbc = _in_proj(h, wint_ref, OFF_XBC + b0, PROJ_BLOCK)
        acc = _causal_conv(xbc, hist_scr[:, cols], cw_ref[:, cols])
        hist_scr[:, cols] = xbc[T - SUBLANES:T, :]
        xbc_c_blocks.append(_silu_mixer(acc + cb_ref[:, cols]))
    cfin_ref[0] = hist_scr[SUBLANES - (SSD_CONV_W - 1):SUBLANES, :]
    xbc_c_all = jnp.concatenate(xbc_c_blocks, axis=1)

    def plus_zero_of(v, gate):
        bits = pltpu.bitcast(gate[0:SUBLANES, 0:LANES], jnp.uint32)
        half_word = jnp.uint32(16)
        zero_row = pltpu.bitcast(
            lax.shift_right_logical(lax.shift_right_logical(bits, half_word), half_word), F32)[0:1, :]
        return v + jnp.concatenate([zero_row] * (D_MODEL // LANES), axis=1).astype(BF16)

    dt_all = _softplus(dt_raw + dtb_ref[...])
    a_all = dt_all * (-jnp.exp(alog_ref[...]))
    row = lax.broadcasted_iota(jnp.int32, (L, L), 0)
    col = lax.broadcasted_iota(jnp.int32, (L, L), 1)
    causal = row >= col
    tril = jnp.where(causal, 1.0, 0.0).astype(BF16)
    lane = lax.broadcasted_iota(jnp.int32, (L, LANES), 1)
    first_half = lane < HEAD_DIM

    y_ssd_chunks = []
    out_sc_chunks = []
    for k in range(T // L):
        rows = slice(k * L, (k + 1) * L)
        xbc_c = xbc_c_all[rows]
        xs = xbc_c[:, 0:SSD_WIDTH]
        b_all = xbc_c[:, SSD_WIDTH:SSD_WIDTH + N_GROUPS * D_STATE]
        c_all = xbc_c[:, SSD_WIDTH + N_GROUPS * D_STATE:SSD_CONV_DIM]
        dt = dt_all[rows]
        a_hi, a_mid, a_lo = _split3(a_all[rows])
        acum = _dot(tril, a_hi) + _dot(tril, a_mid) + _dot(tril, a_lo)
        acum_t = acum.T

        def h_after(gate):
            return plus_zero_of(h[rows], gate)

        def z_gate_after(gate):
            h_late = h_after(gate)
            return jnp.concatenate(
                [_silu_mixer(_in_proj(h_late, wint_ref, OFF_Z + b0, PROJ_BLOCK))
                 for b0 in range(0, SSD_WIDTH, PROJ_BLOCK)], axis=1)

        def short_conv_block_after(gate, b0):
            h_late = h_after(gate)
            cols = slice(b0, b0 + PROJ_BLOCK)
            scc = _in_proj(h_late, wint_ref, OFF_SCC + b0, PROJ_BLOCK)
            sch = _in_proj(h_late, wint_ref, OFF_SCH + b0, PROJ_BLOCK)
            u = scc * sch
            v = _causal_conv(u, uhist_scr[:, cols], scw_ref[:, cols])
            uhist_scr[:, cols] = u[L - SUBLANES:L, :]
            scb = _in_proj(h_late, wint_ref, OFF_SCB + b0, PROJ_BLOCK)
            return (scb * v).astype(BF16)

        z_gate = z_gate_after(acum)
        y_sc = jnp.concatenate(
            [short_conv_block_after(acum, b0) for b0 in range(0, SC_WIDTH, PROJ_BLOCK)], axis=1)
        out_sc_chunks.append(_dot(y_sc, wout_ref[SSD_WIDTH:SSD_WIDTH + SC_WIDTH, :]))

        acum_e = _expand_heads(acum, L)
        dt_e = _expand_heads(dt, L)
        exp_acum_e = jnp.exp(acum_e)
        decay_end_e = jnp.exp(acum_e[L - 1:L, :] - acum_e)
        xdt = xs * dt_e
        xdt_b = xdt.astype(BF16)
        xdecay_b = (xdt * decay_end_e).astype(BF16)

        y_diag_blocks = []
        y_off_blocks = []
        for g in range(N_GROUPS):
            b_g = b_all[:, g * D_STATE:(g + 1) * D_STATE]
            c_g = c_all[:, g * D_STATE:(g + 1) * D_STATE]
            b_gb = b_g.astype(BF16)
            c_gb = c_g.astype(BF16)
            cb = _dot_nt(c_gb, b_gb)
            for q in range(g * HEADS_PER_GROUP // 2, (g + 1) * HEADS_PER_GROUP // 2):
                ms = []
                for hh in (2 * q, 2 * q + 1):
                    seg = acum[:, hh:hh + 1] - acum_t[hh:hh + 1, :]
                    decay = jnp.exp(jnp.where(causal, seg, -jnp.inf))
                    ms.append((cb * decay).astype(BF16))
                m_cat = jnp.concatenate(ms, axis=1)
                x2 = xdt_b[:, q * LANES:(q + 1) * LANES]
                zero = jnp.zeros_like(x2)
                rhs = jnp.concatenate([jnp.where(first_half, x2, zero),
                                       jnp.where(first_half, zero, x2)], axis=0)
                y_diag_blocks.append(_dot(m_cat, rhs))
            gs = slice(g * GROUP_WIDTH, (g + 1) * GROUP_WIDTH)
            s_enter = st_scr[:, gs]
            y_off_blocks.append(_dot(c_gb, s_enter.astype(BF16)))
            new_states = _dot(b_g.T.astype(BF16), xdecay_b[:, gs])
            st_scr[:, gs] = s_enter * exp_acum_e[L - 1:L, gs] + new_states
        y_diag = jnp.concatenate(y_diag_blocks, axis=1)
        y_off = jnp.concatenate(y_off_blocks, axis=1) * exp_acum_e

        y = (y_diag + y_off + dskip_ref[...] * xs) * z_gate
        y_ssd_chunks.append(_group_rmsnorm(y, snw_ref[...]).astype(BF16))

    scfin_ref[0] = uhist_scr[SUBLANES - (SC_CONV_W - 1):SUBLANES, :]
    out_ssd = _dot(jnp.concatenate(y_ssd_chunks, axis=0), wout_ref[0:SSD_WIDTH, :])
    o_ref[0] = x + jnp.concatenate(out_sc_chunks, axis=0) + out_ssd

    @pl.when(last)
    def _():
        sfin_ref[0] = st_scr[...].T


MIXER_CHUNKS_PER_STEP = 2
MIXER_WEIGHT_LOAD_STEPS = WEIGHT_LOAD_STEPS


def _mixer_prompt(x, nw, wint, cw, cb, dtb, alog, dskip, snw, scw, wout):
    nb, seq, d = x.shape
    tile_t = MIXER_CHUNKS_PER_STEP * CHUNK
    assert seq % tile_t == 0
    nc = seq // tile_t
    ws = MIXER_WEIGHT_LOAD_STEPS

    def chunk_map(i):
        t = jnp.maximum(i - ws, 0)
        return (t // nc, t % nc, 0)

    def seq_map(i):
        return (jnp.maximum(i - ws, 0) // nc, 0, 0)

    out_shape = (
        jax.ShapeDtypeStruct((nb, seq, d), F32),
        jax.ShapeDtypeStruct((nb, SSD_WIDTH, D_STATE), F32),
        jax.ShapeDtypeStruct((nb, SSD_CONV_W - 1, SSD_CONV_DIM), F32),
        jax.ShapeDtypeStruct((nb, SC_CONV_W - 1, SC_WIDTH), F32),
    )
    return pl.pallas_call(
        functools.partial(_mixer_kernel, nc),
        out_shape=out_shape,
        grid=(ws + nb * nc,),
        in_specs=[pl.BlockSpec((1, tile_t, d), chunk_map),
                  _resident((1, d)), _weight_rows_spec(wint, ws),
                  _resident((SSD_CONV_W, SSD_CONV_DIM)), _resident((1, SSD_CONV_DIM)),
                  _resident((1, LANES)), _resident((1, LANES)), _resident((1, SSD_WIDTH)),
                  _resident((1, SSD_WIDTH)), _resident((SC_CONV_W, SC_WIDTH)),
                  _weight_rows_spec(wout, ws)],
        out_specs=(pl.BlockSpec((1, tile_t, d), chunk_map),
                   pl.BlockSpec((1, SSD_WIDTH, D_STATE), seq_map),
                   pl.BlockSpec((1, SSD_CONV_W - 1, SSD_CONV_DIM), seq_map),
                   pl.BlockSpec((1, SC_CONV_W - 1, SC_WIDTH), seq_map)),
        scratch_shapes=[pltpu.VMEM((D_STATE, SSD_WIDTH), F32),
                        pltpu.VMEM((SUBLANES, SSD_CONV_DIM), F32),
                        pltpu.VMEM((SUBLANES, SC_WIDTH), F32),
                        _weight_scratch(wint, ws), _weight_scratch(wout, ws)],
        compiler_params=pltpu.CompilerParams(
            dimension_semantics=("arbitrary",), vmem_limit_bytes=VMEM_LIMIT_BYTES),
        name="mixer_prompt",
    )(x, nw, wint.stacked, cw, cb, dtb, alog, dskip, snw, scw, wout.stacked)


def _sample_pre_kernel(x_ref, nw_ref, wint_ref, cw_ref, cb_ref, dtb_ref, alog_ref,
                       scw_ref, cst_ref, scst_ref,
                       z_ref, xs_ref, b_ref, c_ref, xdt_blk_ref, da_ref, ysc_ref, cnew_ref, scnew_ref,
                       proj_scr):
    block = wint_ref.shape[0]
    steps = proj_scr.shape[1] // block
    i = pl.program_id(0)
    h = _rmsnorm(x_ref[...], nw_ref[...]).astype(BF16)
    for k in range(steps):
        @pl.when(i == k)
        def _(k=k):
            proj_scr[:, k * block:(k + 1) * block] = _dot_nt(h, wint_ref[...].astype(BF16))

    pl.when(i == steps - 1)(functools.partial(
        _sample_pre_finish, cw_ref, cb_ref, dtb_ref, alog_ref, scw_ref, cst_ref, scst_ref,
        z_ref, xs_ref, b_ref, c_ref, xdt_blk_ref, da_ref, ysc_ref, cnew_ref, scnew_ref, proj_scr))


def _sample_pre_finish(cw_ref, cb_ref, dtb_ref, alog_ref, scw_ref, cst_ref, scst_ref,
                       z_ref, xs_ref, b_ref, c_ref, xdt_blk_ref, da_ref, ysc_ref, cnew_ref, scnew_ref,
                       proj_scr):
    nb = proj_scr.shape[0]
    proj = lambda col0, width: proj_scr[:, col0:col0 + width]
    dt_lanes = proj(OFF_DT, LANES)
    lane = lax.broadcasted_iota(jnp.int32, dt_lanes.shape, 1)
    dt_raw = jnp.where(lane < N_HEADS, dt_lanes, 0.0)
    xbc = proj(OFF_XBC, SSD_CONV_DIM)
    z_ref[...] = proj(OFF_Z, SSD_WIDTH)
    scb = proj(OFF_SCB, SC_WIDTH)
    scc = proj(OFF_SCC, SC_WIDTH)
    sch = proj(OFF_SCH, SC_WIDTH)

    cw = cw_ref[...]
    acc = xbc * cw[SSD_CONV_W - 1:SSD_CONV_W, :]
    for k in range(SSD_CONV_W - 1):
        acc = acc + cst_ref[k] * cw[k:k + 1, :]
    for k in range(SSD_CONV_W - 2):
        cnew_ref[k] = cst_ref[k + 1]
    cnew_ref[SSD_CONV_W - 2] = xbc
    xbc_c = _silu(acc + cb_ref[...])
    xs = xbc_c[:, 0:SSD_WIDTH]
    xs_ref[...] = xs
    b_ref[...] = xbc_c[:, SSD_WIDTH:SSD_WIDTH + N_GROUPS * D_STATE]
    c_ref[...] = xbc_c[:, SSD_WIDTH + N_GROUPS * D_STATE:SSD_CONV_DIM]

    dt = _softplus(dt_raw + dtb_ref[...])
    da_ref[...] = jnp.exp(dt * (-jnp.exp(alog_ref[...])))
    xdt_t = (xs * _expand_heads(dt, nb)).T
    bb = xdt_blk_ref.shape[2]
    for j in range(nb // bb):
        xdt_blk_ref[j] = xdt_t[:, j * bb:(j + 1) * bb]

    u = scc * sch
    scw = scw_ref[...]
    v = u * scw[SC_CONV_W - 1:SC_CONV_W, :]
    for k in range(SC_CONV_W - 1):
        v = v + scst_ref[:, k, :] * scw[k:k + 1, :]
    for k in range(SC_CONV_W - 2):
        scnew_ref[:, k, :] = scst_ref[:, k + 1, :]
    scnew_ref[:, SC_CONV_W - 2, :] = u
    ysc_ref[...] = scb * v


def _sample_pre(x, nw, wint, cw, cb, dtb, alog, scw, cst, scst):
    nb, d = x.shape
    f = lambda *s: jax.ShapeDtypeStruct(s, F32)
    bb = STATE_BATCH_BLOCK
    assert nb % bb == 0
    out_shape = (f(nb, SSD_WIDTH), f(nb, SSD_WIDTH), f(nb, N_GROUPS * D_STATE), f(nb, N_GROUPS * D_STATE),
                 f(nb // bb, SSD_WIDTH, bb), f(nb, LANES), f(nb, SC_WIDTH),
                 f(SSD_CONV_W - 1, nb, SSD_CONV_DIM), f(nb, SC_CONV_W - 1, SC_WIDTH))
    steps = SAMPLE_PRE_STEPS
    block = _weight_block_rows(wint.shape[0], steps)
    whole = lambda a: pl.BlockSpec(a.shape, lambda i: (0,) * len(a.shape))
    ins = (x, nw, wint, cw, cb, dtb, alog, scw, cst, scst)
    in_specs = [whole(a) for a in ins]
    in_specs[2] = pl.BlockSpec((block, d), lambda i: (i, 0))
    return pl.pallas_call(
        _sample_pre_kernel,
        out_shape=out_shape,
        grid=(steps,),
        in_specs=in_specs,
        out_specs=tuple(whole(o) for o in out_shape),
        scratch_shapes=[pltpu.VMEM((nb, steps * block), F32)],
        compiler_params=pltpu.CompilerParams(
            dimension_semantics=("arbitrary",), vmem_limit_bytes=VMEM_LIMIT_BYTES),
        name="sample_pre",
    )(*ins)


SAMPLE_PRE_STEPS = 9


STATE_BATCH_BLOCK = 16


def _split2(x):
    hi = x.astype(BF16)
    return hi, (x - hi.astype(F32)).astype(BF16)


def _sample_state_kernel(da_ref, s0_ref, xdt_ref, b_ref, c_ref, snew_ref, y_ref):
    bb = STATE_BATCH_BLOCK
    blk = pl.program_id(0)
    xdt_t = xdt_ref[0]
    b_rows = b_ref[...]
    c_hi, c_lo = _split2(c_ref[...])
    row = lax.broadcasted_iota(jnp.int32, (bb, SSD_WIDTH), 0)
    y_blk = jnp.zeros((bb, SSD_WIDTH), F32)
    for i in range(bb):
        xdt_col = xdt_t[:, i:i + 1]
        y_parts = []
        for g in range(N_GROUPS):
            ns = slice(g * D_STATE, (g + 1) * D_STATE)
            b_row = b_rows[i:i + 1, ns]
            heads = []
            for hh in range(g * HEADS_PER_GROUP, (g + 1) * HEADS_PER_GROUP):
                rs = slice(hh * HEAD_DIM, (hh + 1) * HEAD_DIM)
                decay = da_ref[blk * bb + i, hh]
                heads.append(s0_ref[i, rs, :] * decay + xdt_col[rs] * b_row)
            s_new = jnp.concatenate(heads, axis=0)
            snew_ref[i, g * GROUP_WIDTH:(g + 1) * GROUP_WIDTH, :] = s_new
            s_hi, s_lo = _split2(s_new)
            lhs = jnp.concatenate([c_hi[:, ns], c_lo[:, ns]], axis=0)
            r_hi = _dot_nt(lhs, s_hi)
            r_lo = _dot_nt(c_hi[:, ns], s_lo)
            y_parts.append(r_hi[i:i + 1] + r_hi[bb + i:bb + i + 1] + r_lo[i:i + 1])
        y_row = jnp.concatenate(y_parts, axis=1)
        y_blk = jnp.where(row == i, y_row, y_blk)
    y_ref[...] = y_blk


def _sample_state(s0, xdt_blocks, decay, b_rows, c_rows):
    nb = s0.shape[0]
    nblk, _, bb = xdt_blocks.shape
    assert bb == STATE_BATCH_BLOCK and nblk * bb == nb
    return pl.pallas_call(
        _sample_state_kernel,
        out_shape=(jax.ShapeDtypeStruct((nb, SSD_WIDTH, D_STATE), F32),
                   jax.ShapeDtypeStruct((nb, SSD_WIDTH), F32)),
        grid=(nblk,),
        in_specs=[pl.BlockSpec(memory_space=pltpu.SMEM),
                  pl.BlockSpec((bb, SSD_WIDTH, D_STATE), lambda i: (i, 0, 0)),
                  pl.BlockSpec((1, SSD_WIDTH, bb), lambda i: (i, 0, 0)),
                  pl.BlockSpec((bb, N_GROUPS * D_STATE), lambda i: (i, 0)),
                  pl.BlockSpec((bb, N_GROUPS * D_STATE), lambda i: (i, 0))],
        out_specs=(pl.BlockSpec((bb, SSD_WIDTH, D_STATE), lambda i: (i, 0, 0)),
                   pl.BlockSpec((bb, SSD_WIDTH), lambda i: (i, 0))),
        compiler_params=pltpu.CompilerParams(
            dimension_semantics=("arbitrary",), vmem_limit_bytes=VMEM_LIMIT_BYTES),
        name="sample_state",
    )(decay, s0, xdt_blocks, b_rows, c_rows)


def _sample_post_kernel(x_ref, yraw_ref, xs_ref, z_ref, ysc_ref, dskip_ref, snw_ref, wout_ref, o_ref):
    y = yraw_ref[...] + dskip_ref[...] * xs_ref[...]
    y = y * _silu(z_ref[...])
    y_ssd = _group_rmsnorm(y, snw_ref[...])
    mixed = jnp.concatenate([y_ssd, ysc_ref[...]], axis=1).astype(BF16)
    o_ref[...] = x_ref[...] + _dot(mixed, wout_ref[...].astype(BF16))


def _sample_post(x, yraw, xs, z, ysc, dskip, snw, wout):
    return pl.pallas_call(
        _sample_post_kernel,
        out_shape=jax.ShapeDtypeStruct(x.shape, F32),
        compiler_params=pltpu.CompilerParams(vmem_limit_bytes=VMEM_LIMIT_BYTES),
        name="sample_post",
    )(x, yraw, xs, z, ysc, dskip, snw, wout)


FFN_TILE_M = 1024
FFN_SUB_TILE_M = 256


def _layer_params(i, norm_ffn1_w, ffn1_w_gate, ffn1_w_up, ffn1_w_down, norm_mix_w, w_in_t,
                  ssd_conv_w, ssd_conv_b, dt_bias, a_log, d_skip, ssd_norm_w, sconv_w, w_out,
                  norm_ffn2_w, ffn2_w_gate, ffn2_w_up, ffn2_w_down):
    pad_heads = lambda v: jnp.pad(v, (0, LANES - N_HEADS)).reshape(1, LANES)
    row = lambda v: v.reshape(1, -1)
    return dict(
        ffn1=(row(norm_ffn1_w[i]), _LayerWeight(ffn1_w_gate, i), _LayerWeight(ffn1_w_up, i),
              _LayerWeight(ffn1_w_down, i)),
        ffn2=(row(norm_ffn2_w[i]), _LayerWeight(ffn2_w_gate, i), _LayerWeight(ffn2_w_up, i),
              _LayerWeight(ffn2_w_down, i)),
        nw=row(norm_mix_w[i]), wint=_LayerWeight(w_in_t, i), cw=ssd_conv_w[i], cb=row(ssd_conv_b[i]),
        dtb=pad_heads(dt_bias[i]), alog=pad_heads(a_log[i]),
        dskip=row(jnp.repeat(d_skip[i], HEAD_DIM)), snw=row(ssd_norm_w[i]), scw=sconv_w[i],
        wout=_LayerWeight(w_out, i))


def kernel(x_prompt, x_sample, state_ssm, state_ssd_conv, state_sconv, norm_ffn1_w, ffn1_w_gate, ffn1_w_up, ffn1_w_down, norm_mix_w, w_in, ssd_conv_w, ssd_conv_b, dt_bias, a_log, d_skip, ssd_norm_w, sconv_w, w_out, norm_ffn2_w, ffn2_w_gate, ffn2_w_up, ffn2_w_down, final_norm_w):
    depth = w_in.shape[0]
    bp, seq, d = x_prompt.shape
    bs, dec_seq, _ = x_sample.shape
    assert dec_seq == 1, "sample group is one token per sequence"
    fnw = final_norm_w.reshape(1, d)
    w_in_t = jnp.swapaxes(w_in, 1, 2)

    xp = x_prompt.reshape(bp * seq, d)
    xs = x_sample
    outs = [[] for _ in range(6)]
    for i in range(depth):
        p = _layer_params(i, norm_ffn1_w, ffn1_w_gate, ffn1_w_up, ffn1_w_down, norm_mix_w, w_in_t,
                          ssd_conv_w, ssd_conv_b, dt_bias, a_log, d_skip, ssd_norm_w, sconv_w,
                          w_out, norm_ffn2_w, ffn2_w_gate, ffn2_w_up, ffn2_w_down)
        last = i == depth - 1
        xp, xs = _ffn(xp, xs, *p["ffn1"], tile_m=FFN_TILE_M, sample_out_shape=(bs, d))
        xp3, s_p, c_p, sc_p = _mixer_prompt(
            xp.reshape(bp, seq, d), p["nw"], p["wint"], p["cw"], p["cb"], p["dtb"], p["alog"],
            p["dskip"], p["snw"], p["scw"], p["wout"])
        z, xs_conv, b_rows, c_rows, xdt_t, decay, ysc, c_s, sc_s = _sample_pre(
            xs, p["nw"], w_in_t[i], p["cw"], p["cb"], p["dtb"], p["alog"], p["scw"],
            jnp.swapaxes(state_ssd_conv[i], 0, 1), state_sconv[i])
        s_s, yraw = _sample_state(state_ssm[i].reshape(bs, SSD_WIDTH, D_STATE), xdt_t, decay, b_rows, c_rows)
        xs = _sample_post(xs, yraw, xs_conv, z, ysc, p["dskip"], p["snw"], w_out[i])
        xp, xs = _ffn(xp3.reshape(bp * seq, d), xs, *p["ffn2"], fnw if last else None, tile_m=FFN_TILE_M,
                      sample_out_shape=(bs, dec_seq, d) if last else (bs, d))
        for lst, v in zip(outs, (s_p.reshape(bp, N_HEADS, HEAD_DIM, D_STATE), c_p, sc_p,
                                 s_s.reshape(bs, N_HEADS, HEAD_DIM, D_STATE),
                                 jnp.swapaxes(c_s, 0, 1), sc_s)):
            lst.append(v)
    return (xp.reshape(bp, seq, d), xs) + tuple(jnp.stack(l) for l in outs)
```
